```python
import jax, jax.numpy as jnp
from jax import lax
import numpy as np

D_MODEL = 1024
BATCH = 2
SEQ = 8192
DEPTH = 1

EPS = 1e-6
D_FF = 2816
MLA_HEADS = 8
MLA_Q_RANK = 384
MLA_KV_RANK = 256
MLA_NOPE = 64
MLA_ROPE = 32
MLA_V = 64
MLA_THETA = 10000.0
MLA_WIDTH = MLA_HEADS * MLA_V
DIL_HEADS = 8
DIL_HEAD_DIM = 64
DIL_PATTERNS = ((128, 1), (512, 4), (2048, 16))
DIL_WIDTH = DIL_HEADS * DIL_HEAD_DIM
ROPE_THETA = 500000.0
ROPE_DIM = DIL_HEAD_DIM // 4
N_BRANCH = 2
Q_BLOCK = 128
NEG = -1e30
IN_SPLITS = (MLA_Q_RANK, MLA_KV_RANK, MLA_ROPE, DIL_WIDTH, DIL_WIDTH, DIL_WIDTH, N_BRANCH * D_MODEL)
IN_DIM = int(sum(IN_SPLITS))

kernel_name = 'hybrid_mla_dilated_macaron_block'


def rms_norm(x, g):
    xf = x.astype(jnp.float32)
    y = xf * lax.rsqrt(jnp.mean(xf * xf, axis=-1, keepdims=True) + EPS)
    return (y * g.astype(jnp.float32)).astype(x.dtype)


def rope(x, positions, theta, rot_dim):
    half = rot_dim // 2
    inv = 1.0 / (jnp.float32(theta) ** (jnp.arange(half, dtype=jnp.float32) / half))
    ang = positions.astype(jnp.float32)[:, :, None] * inv
    cos = jnp.cos(ang)[:, :, None, :]
    sin = jnp.sin(ang)[:, :, None, :]
    xr = x[..., :rot_dim].astype(jnp.float32)
    x1, x2 = xr[..., :half], xr[..., half:]
    rot = jnp.concatenate([x1 * cos - x2 * sin, x2 * cos + x1 * sin], axis=-1).astype(x.dtype)
    return jnp.concatenate([rot, x[..., rot_dim:]], axis=-1)


def swiglu(x, w_gate, w_up, w_down):
    return (jax.nn.silu(x @ w_gate) * (x @ w_up)) @ w_down


def dense_attention(q, k, v):
    B, S, H, Dk = q.shape
    scale = Dk ** -0.5
    nqb = S // Q_BLOCK
    qb = q.reshape(B, nqb, Q_BLOCK, H, Dk).transpose(1, 0, 2, 3, 4)

    def block(qi):
        s = jnp.einsum('bqhd,bkhd->bhqk', qi, k, preferred_element_type=jnp.float32) * scale
        p = jax.nn.softmax(s, axis=-1)
        return jnp.einsum('bhqk,bkhd->bqhd', p, v.astype(jnp.float32)).astype(v.dtype)

    o = lax.map(block, qb)
    return o.transpose(1, 0, 2, 3, 4).reshape(B, S, H, v.shape[-1])


def dilated_pattern(q, k, v, window, dilation):
    B, S, H, D = q.shape
    half = window // (2 * dilation)
    blk = half
    L = -(-S // dilation)
    Sp = L * dilation
    nb = -(-L // blk)
    Lp = nb * blk

    def split(t):
        t = jnp.pad(t, ((0, 0), (0, Sp - S), (0, 0), (0, 0)))
        return t.reshape(B, L, dilation, H, t.shape[-1]).transpose(0, 2, 3, 1, 4)

    def pad_l(t, lo, hi):
        return jnp.pad(t, ((0, 0), (0, 0), (0, 0), (lo, hi), (0, 0)))

    qs = pad_l(split(q), 0, Lp - L).reshape(B, dilation, H, nb, blk, D)

    def windows(t):
        dt = t.shape[-1]
        t = pad_l(split(t), blk, Lp - L + blk).reshape(B, dilation, H, nb + 2, blk, dt)
        return jnp.concatenate([t[:, :, :, :-2], t[:, :, :, 1:-1], t[:, :, :, 2:]], axis=-2)

    kw = windows(k)
    vw = windows(v)
    s = jnp.einsum('brhnqd,brhnkd->brhnqk', qs, kw, preferred_element_type=jnp.float32) * (D ** -0.5)
    a = jnp.arange(blk)[:, None]
    c = jnp.arange(3 * blk)[None, :]
    band = jnp.abs(c - blk - a) <= half
    jk = jnp.arange(nb)[:, None] * blk - blk + jnp.arange(3 * blk)[None, :]
    pos_k = jk[None] * dilation + jnp.arange(dilation)[:, None, None]
    valid = (jk[None] >= 0) & (pos_k < S)
    mask = band[None, None] & valid[:, :, None, :]
    s = jnp.where(mask[None, :, None], s, NEG)
    m = jnp.max(s, axis=-1, keepdims=True)
    p = jnp.exp(s - m)
    den = jnp.sum(p, axis=-1, keepdims=True)
    num = jnp.einsum('brhnqk,brhnkd->brhnqd', p, vw.astype(jnp.float32))

    def merge(t):
        x_ = t.shape[-1]
        t = t.reshape(B, dilation, H, Lp, x_)[:, :, :, :L]
        return t.transpose(0, 3, 1, 2, 4).reshape(B, Sp, H, x_)[:, :S]

    return merge(m), merge(den), merge(num)


def dilated_mixture(q, k, v):
    parts = [dilated_pattern(q, k, v, w, d) for (w, d) in DIL_PATTERNS]
    m_all = jnp.max(jnp.concatenate([pm for pm, _, _ in parts], axis=-1), axis=-1, keepdims=True)
    num = None
    den = None
    for pm, ps, pn in parts:
        wgt = jnp.exp(pm - m_all)
        num = wgt * pn if num is None else num + wgt * pn
        den = wgt * ps if den is None else den + wgt * ps
    return (num / den).astype(q.dtype)


def hybrid_mixer(u, positions, w_in, b_gate, q_norm_g, w_uq, kv_norm_g, w_uk, w_uv,
                 w_branch_a, w_branch_b, w_out):
    B, S, _ = u.shape
    proj = u @ w_in
    cq, ckv, kr, dq, dk, dv, g = jnp.split(proj, np.cumsum(IN_SPLITS)[:-1].tolist(), axis=-1)

    q = (rms_norm(cq, q_norm_g) @ w_uq).reshape(B, S, MLA_HEADS, MLA_NOPE + MLA_ROPE)
    q = jnp.concatenate([q[..., :MLA_NOPE], rope(q[..., MLA_NOPE:], positions, MLA_THETA, MLA_ROPE)], axis=-1)
    k_rope = rope(kr[:, :, None, :], positions, MLA_THETA, MLA_ROPE)
    c_kv = rms_norm(ckv, kv_norm_g)
    k_nope = (c_kv @ w_uk).reshape(B, S, MLA_HEADS, MLA_NOPE)
    v_a = (c_kv @ w_uv).reshape(B, S, MLA_HEADS, MLA_V)
    k_a = jnp.concatenate([k_nope, jnp.broadcast_to(k_rope, (B, S, MLA_HEADS, MLA_ROPE))], axis=-1)
    o_a = dense_attention(q, k_a, v_a).reshape(B, S, MLA_WIDTH)

    qd = rope(dq.reshape(B, S, DIL_HEADS, DIL_HEAD_DIM), positions, ROPE_THETA, ROPE_DIM)
    kd = rope(dk.reshape(B, S, DIL_HEADS, DIL_HEAD_DIM), positions, ROPE_THETA, ROPE_DIM)
    vd = dv.reshape(B, S, DIL_HEADS, DIL_HEAD_DIM)
    o_b = dilated_mixture(qd, kd, vd).reshape(B, S, DIL_WIDTH)

    gates = jax.nn.sigmoid((g + b_gate).reshape(B, S, N_BRANCH, D_MODEL))
    merged = gates[:, :, 0] * (o_a @ w_branch_a) + gates[:, :, 1] * (o_b @ w_branch_b)
    return merged @ w_out


def setup_inputs(seed: int = 0) -> dict:
    key = jax.random.key(seed)
    ks = iter(jax.random.split(key, 32))

    def w(shape, fan_in):
        return jax.random.normal(next(ks), (DEPTH,) + shape, jnp.float32) * (fan_in ** -0.5)

    def gain(n):
        return 1.0 + 0.05 * jax.random.normal(next(ks), (DEPTH, n), jnp.float32)

    x = jax.random.normal(next(ks), (BATCH, SEQ, D_MODEL), jnp.float32)
    offset = jax.random.randint(next(ks), (BATCH, 1), 0, 4096, dtype=jnp.int32)
    positions = (offset + jnp.arange(SEQ, dtype=jnp.int32)[None, :]).astype(jnp.int32)
    return {
        'x': x,
        'positions': positions,
        'ffn1_pre_g': gain(D_MODEL),
        'ffn1_post_g': gain(D_MODEL),
        'ffn1_w_gate': w((D_MODEL, D_FF), D_MODEL),
        'ffn1_w_up': w((D_MODEL, D_FF), D_MODEL),
        'ffn1_w_down': w((D_FF, D_MODEL), D_FF),
        'mix_pre_g': gain(D_MODEL),
        'w_in': w((D_MODEL, IN_DIM), D_MODEL),
        'b_gate': 0.1 * jax.random.normal(next(ks), (DEPTH, N_BRANCH * D_MODEL), jnp.float32),
        'q_norm_g': gain(MLA_Q_RANK),
        'w_uq': w((MLA_Q_RANK, MLA_HEADS * (MLA_NOPE + MLA_ROPE)), MLA_Q_RANK),
        'kv_norm_g': gain(MLA_KV_RANK),
        'w_uk': w((MLA_KV_RANK, MLA_HEADS * MLA_NOPE), MLA_KV_RANK),
        'w_uv': w((MLA_KV_RANK, MLA_HEADS * MLA_V), MLA_KV_RANK),
        'w_branch_a': w((MLA_WIDTH, D_MODEL), MLA_WIDTH),
        'w_branch_b': w((DIL_WIDTH, D_MODEL), DIL_WIDTH),
        'w_out': w((D_MODEL, D_MODEL), D_MODEL),
        'mix_post_g': gain(D_MODEL),
        'ffn2_pre_g': gain(D_MODEL),
        'ffn2_post_g': gain(D_MODEL),
        'ffn2_w_gate': w((D_MODEL, D_FF), D_MODEL),
        'ffn2_w_up': w((D_MODEL, D_FF), D_MODEL),
        'ffn2_w_down': w((D_FF, D_MODEL), D_FF),
    }


def reference(x, positions, ffn1_pre_g, ffn1_post_g, ffn1_w_gate, ffn1_w_up, ffn1_w_down,
              mix_pre_g, w_in, b_gate, q_norm_g, w_uq, kv_norm_g, w_uk, w_uv,
              w_branch_a, w_branch_b, w_out, mix_post_g,
              ffn2_pre_g, ffn2_post_g, ffn2_w_gate, ffn2_w_up, ffn2_w_down):
    h = x
    for l in range(DEPTH):
        f1 = swiglu(rms_norm(h, ffn1_pre_g[l]), ffn1_w_gate[l], ffn1_w_up[l], ffn1_w_down[l])
        h = h + 0.5 * rms_norm(f1, ffn1_post_g[l])
        mix = hybrid_mixer(rms_norm(h, mix_pre_g[l]), positions, w_in[l], b_gate[l],
                           q_norm_g[l], w_uq[l], kv_norm_g[l], w_uk[l], w_uv[l],
                           w_branch_a[l], w_branch_b[l], w_out[l])
        h = h + rms_norm(mix, mix_post_g[l])
        f2 = swiglu(rms_norm(h, ffn2_pre_g[l]), ffn2_w_gate[l], ffn2_w_up[l], ffn2_w_down[l])
        h = h + 0.5 * rms_norm(f2, ffn2_post_g[l])
    return h
```

```python
import functools

import jax
import jax.numpy as jnp
import numpy as np
from jax import lax
from jax.experimental import pallas as pl
from jax.experimental.pallas import tpu as pltpu

F32 = jnp.float32
BF16 = jnp.bfloat16

D_MODEL = 1024
D_FF = 2816
EPS = 1e-6
MLA_HEADS = 8
MLA_Q_RANK = 384
MLA_KV_RANK = 256
MLA_NOPE = 64
MLA_ROPE = 32
MLA_V = 64
MLA_THETA = 10000.0
MLA_WIDTH = MLA_HEADS * MLA_V
DIL_HEADS = 8
DIL_HEAD_DIM = 64
DIL_PATTERNS = ((128, 1), (512, 4), (2048, 16))
DIL_WIDTH = DIL_HEADS * DIL_HEAD_DIM
ROPE_THETA = 500000.0
ROPE_DIM = DIL_HEAD_DIM // 4
N_BRANCH = 2
NEG = -1e30

LANES = 128
HEAD_PAD = LANES
VMEM_LIMIT = 56 * 1024 * 1024

FFN_TM = 512
FFN_FC = 256
MIX_TM = 512
FLASH_TQ = 256
FLASH_TK = MIX_TM
DIL_LB = 512
DIL_SUB = 128
DIL_HALF = 64


def _rms(x, g):
    ms = jnp.mean(x * x, axis=-1, keepdims=True)
    return x * lax.rsqrt(ms + EPS) * g


def _const_spec(shape):
    nd = len(shape)
    return pl.BlockSpec(shape, lambda *_: (0,) * nd, pipeline_mode=pl.Buffered(1))


def _ffn_kernel(x_ref, pre_ref, post_ref, wgu_ref, wd_ref, o_ref, hm_ref):
    x = x_ref[...]
    xn = _rms(x, pre_ref[...]).astype(BF16)
    for c in range(D_FF // FFN_FC):
        gu = jnp.dot(xn, wgu_ref[:, 2 * FFN_FC * c:2 * FFN_FC * (c + 1)], preferred_element_type=F32)
        g = gu[:, :FFN_FC]
        u = gu[:, FFN_FC:]
        hm_ref[:, FFN_FC * c:FFN_FC * (c + 1)] = (g * jax.nn.sigmoid(g) * u).astype(BF16)
    f = jnp.dot(hm_ref[...], wd_ref[...], preferred_element_type=F32)
    o_ref[...] = x + 0.5 * _rms(f, post_ref[...])


def _ffn(h, pre_g, post_g, w_gate, w_up, w_down):
    t = h.shape[0]
    nc = D_FF // FFN_FC
    wgu = jnp.concatenate([w_gate.reshape(D_MODEL, nc, FFN_FC), w_up.reshape(D_MODEL, nc, FFN_FC)],
                          axis=-1).reshape(D_MODEL, 2 * D_FF).astype(BF16)
    wd = w_down.astype(BF16)
    return pl.pallas_call(
        _ffn_kernel,
        name="ffn",
        grid=(t // FFN_TM,),
        in_specs=[
            pl.BlockSpec((FFN_TM, D_MODEL), lambda i: (i, 0)),
            _const_spec((1, D_MODEL)),
            _const_spec((1, D_MODEL)),
            _const_spec((D_MODEL, 2 * D_FF)),
            _const_spec((D_FF, D_MODEL)),
        ],
        out_specs=pl.BlockSpec((FFN_TM, D_MODEL), lambda i: (i, 0)),
        out_shape=jax.ShapeDtypeStruct((t, D_MODEL), F32),
        scratch_shapes=[pltpu.VMEM((FFN_TM, D_FF), BF16)],
        compiler_params=pltpu.CompilerParams(dimension_semantics=("arbitrary",), vmem_limit_bytes=VMEM_LIMIT),
    )(h, pre_g.reshape(1, D_MODEL), post_g.reshape(1, D_MODEL), wgu, wd)


def _rope_tile(x, cos, sin_lo, sin_hi, half):
    return x * cos + pltpu.roll(x, LANES - half, 1) * sin_lo + pltpu.roll(x, half, 1) * sin_hi


def _mixer_in_kernel(h_ref, pos_ref, pre_ref, wa_ref, wd_ref, wg_ref, bg_ref, qg_ref, kvg_ref,
                     wuq_ref, wuk_ref, wuvt_ref, tab_ref,
                     q_ref, k_ref, vt_ref, dq_ref, dk_ref, dv_ref, gate_ref):
    u = _rms(h_ref[0], pre_ref[...]).astype(BF16)
    pos = pos_ref[0]
    tab = tab_ref[...]
    ang_a = pos * tab[0:1, :]
    cos_a = jnp.cos(ang_a)
    sin_a = jnp.sin(ang_a)
    sin_a_lo = sin_a * tab[1:2, :]
    sin_a_hi = sin_a * tab[2:3, :]
    ang_b = pos * tab[3:4, :]
    cos_b = jnp.cos(ang_b)
    sin_b = jnp.sin(ang_b)
    sin_b_lo = sin_b * tab[4:5, :]
    sin_b_hi = sin_b * tab[5:6, :]

    pa = jnp.dot(u, wa_ref[...], preferred_element_type=F32)
    cq = pa[:, :MLA_Q_RANK]
    ckv = pa[:, MLA_Q_RANK:MLA_Q_RANK + MLA_KV_RANK]
    kr = _rope_tile(pa[:, MLA_Q_RANK + MLA_KV_RANK:], cos_a, sin_a_lo, sin_a_hi, MLA_ROPE // 2)
    qn = _rms(cq, qg_ref[...]).astype(BF16)
    ckvn = _rms(ckv, kvg_ref[...]).astype(BF16)
    q = jnp.dot(qn, wuq_ref[...], preferred_element_type=F32)
    kn = jnp.dot(ckvn, wuk_ref[...], preferred_element_type=F32)
    scale = (MLA_NOPE + MLA_ROPE) ** -0.5
    for h in range(MLA_HEADS):
        sl = slice(HEAD_PAD * h, HEAD_PAD * (h + 1))
        qh = _rope_tile(q[:, sl], cos_a, sin_a_lo, sin_a_hi, MLA_ROPE // 2)
        q_ref[0, :, sl] = (qh * scale).astype(BF16)
        k_ref[0, :, sl] = (kn[:, sl] + kr).astype(BF16)
    vt = lax.dot_general(wuvt_ref[...], ckvn, (((1,), (1,)), ((), ())), preferred_element_type=F32)
    vt_ref[0, 0] = vt.astype(BF16)

    pd = jnp.dot(u, wd_ref[...], preferred_element_type=F32)
    dscale = DIL_HEAD_DIM ** -0.5
    for c in range(DIL_WIDTH // LANES):
        sl = slice(LANES * c, LANES * (c + 1))
        ks = slice(DIL_WIDTH + LANES * c, DIL_WIDTH + LANES * (c + 1))
        dq_ref[0, :, sl] = (_rope_tile(pd[:, sl], cos_b, sin_b_lo, sin_b_hi, ROPE_DIM // 2) * dscale).astype(BF16)
        dk_ref[0, :, sl] = _rope_tile(pd[:, ks], cos_b, sin_b_lo, sin_b_hi, ROPE_DIM // 2).astype(BF16)
    dv_ref[0] = pd[:, 2 * DIL_WIDTH:].astype(BF16)

    pg = jnp.dot(u, wg_ref[...], preferred_element_type=F32) + bg_ref[...]
    gate_ref[0] = jax.nn.sigmoid(pg).astype(BF16)


def _rope_tables():
    half_a = MLA_ROPE // 2
    inv_a = 1.0 / (jnp.float32(MLA_THETA) ** (jnp.arange(half_a, dtype=F32) / half_a))
    half_b = ROPE_DIM // 2
    inv_b = 1.0 / (jnp.float32(ROPE_THETA) ** (jnp.arange(half_b, dtype=F32) / half_b))
    z = lambda n: jnp.zeros((n,), F32)
    o = lambda n: jnp.ones((n,), F32)
    row_a = jnp.concatenate([z(MLA_NOPE), inv_a, inv_a, z(HEAD_PAD - MLA_NOPE - MLA_ROPE)])
    lo_a = jnp.concatenate([z(MLA_NOPE), -o(half_a), z(half_a), z(HEAD_PAD - MLA_NOPE - MLA_ROPE)])
    hi_a = jnp.concatenate([z(MLA_NOPE), z(half_a), o(half_a), z(HEAD_PAD - MLA_NOPE - MLA_ROPE)])
    rest = DIL_HEAD_DIM - ROPE_DIM
    head_b = jnp.concatenate([inv_b, inv_b, z(rest)])
    head_lo = jnp.concatenate([-o(half_b), z(half_b), z(rest)])
    head_hi = jnp.concatenate([z(half_b), o(half_b), z(rest)])
    rep = LANES // DIL_HEAD_DIM
    rows = [row_a, lo_a, hi_a, jnp.tile(head_b, rep), jnp.tile(head_lo, rep), jnp.tile(head_hi, rep), z(LANES), z(LANES)]
    return jnp.stack(rows)


def _mixer_in(h1, positions, mix_pre_g, w_in, b_gate, q_norm_g, w_uq, kv_norm_g, w_uk, w_uv):
    b, s, _ = h1.shape
    o0 = 0
    o1 = o0 + MLA_Q_RANK
    o2 = o1 + MLA_KV_RANK
    o3 = o2 + MLA_ROPE
    o4 = o3 + 3 * DIL_WIDTH
    pad_hi = HEAD_PAD - MLA_NOPE - MLA_ROPE
    w_kr = jnp.pad(w_in[:, o2:o3], ((0, 0), (MLA_NOPE, pad_hi)))
    wa = jnp.concatenate([w_in[:, o0:o2], w_kr], axis=1).astype(BF16)
    wd = w_in[:, o3:o4].astype(BF16)
    wg = w_in[:, o4:].astype(BF16)
    wuq = jnp.pad(w_uq.reshape(MLA_Q_RANK, MLA_HEADS, MLA_NOPE + MLA_ROPE),
                  ((0, 0), (0, 0), (0, pad_hi))).reshape(MLA_Q_RANK, MLA_HEADS * HEAD_PAD).astype(BF16)
    wuk = jnp.pad(w_uk.reshape(MLA_KV_RANK, MLA_HEADS, MLA_NOPE),
                  ((0, 0), (0, 0), (0, HEAD_PAD - MLA_NOPE))).reshape(MLA_KV_RANK, MLA_HEADS * HEAD_PAD).astype(BF16)
    wuvt = w_uv.T.astype(BF16)
    pos = positions.astype(F32).reshape(b, s, 1)
    tm = MIX_TM
    hp = MLA_HEADS * HEAD_PAD
    tok = lambda c: pl.BlockSpec((1, tm, c), lambda bi, i: (bi, i, 0))
    out_shapes = (
        jax.ShapeDtypeStruct((b, s, hp), BF16),
        jax.ShapeDtypeStruct((b, s, hp), BF16),
        jax.ShapeDtypeStruct((b, s // tm, MLA_WIDTH, tm), BF16),
        jax.ShapeDtypeStruct((b, s, DIL_WIDTH), BF16),
        jax.ShapeDtypeStruct((b, s, DIL_WIDTH), BF16),
        jax.ShapeDtypeStruct((b, s, DIL_WIDTH), BF16),
        jax.ShapeDtypeStruct((b, s, N_BRANCH * D_MODEL), BF16),
    )
    out_specs = (
        tok(hp), tok(hp),
        pl.BlockSpec((1, 1, MLA_WIDTH, tm), lambda bi, i: (bi, i, 0, 0)),
        tok(DIL_WIDTH), tok(DIL_WIDTH), tok(DIL_WIDTH), tok(N_BRANCH * D_MODEL),
    )
    return pl.pallas_call(
        _mixer_in_kernel,
        name="mixer_in",
        grid=(b, s // tm),
        in_specs=[
            tok(D_MODEL), tok(1),
            _const_spec((1, D_MODEL)),
            _const_spec(wa.shape), _const_spec(wd.shape), _const_spec(wg.shape),
            _const_spec((1, N_BRANCH * D_MODEL)),
            _const_spec((1, MLA_Q_RANK)), _const_spec((1, MLA_KV_RANK)),
            _const_spec(wuq.shape), _const_spec(wuk.shape), _const_spec(wuvt.shape),
            _const_spec((8, LANES)),
        ],
        out_specs=out_specs,
        out_shape=out_shapes,
        compiler_params=pltpu.CompilerParams(dimension_semantics=("arbitrary", "arbitrary"),
                                             vmem_limit_bytes=VMEM_LIMIT),
    )(h1, pos, mix_pre_g.reshape(1, D_MODEL), wa, wd, wg, b_gate.reshape(1, -1),
      q_norm_g.reshape(1, -1), kv_norm_g.reshape(1, -1), wuq, wuk, wuvt, _rope_tables())


def _flash_kernel(q_ref, k_ref, vt_ref, o_ref):
    nk = k_ref.shape[1] // FLASH_TK
    outs = []
    for hh in range(2):
        q = q_ref[0, :, HEAD_PAD * hh:HEAD_PAD * (hh + 1)]

        def step(c, carry, q=q, hh=hh):
            m, l, acc = carry
            k = k_ref[0, pl.ds(pl.multiple_of(c * FLASH_TK, FLASH_TK), FLASH_TK), HEAD_PAD * hh:HEAD_PAD * (hh + 1)]
            st = lax.dot_general(k, q, (((1,), (1,)), ((), ())), preferred_element_type=F32)
            m_new = jnp.maximum(m, jnp.max(st, axis=0, keepdims=True))
            alpha = jnp.exp(m - m_new)
            p = jnp.exp(st - m_new)
            l_new = alpha * l + jnp.sum(p, axis=0, keepdims=True)
            vt = vt_ref[0, c, MLA_V * hh:MLA_V * (hh + 1), :]
            acc_new = alpha * acc + jnp.dot(vt, p.astype(BF16), preferred_element_type=F32)
            return m_new, l_new, acc_new

        init = (jnp.full((1, FLASH_TQ), -jnp.inf, F32), jnp.zeros((1, FLASH_TQ), F32),
                jnp.zeros((MLA_V, FLASH_TQ), F32))
        m, l, acc = lax.fori_loop(0, nk, step, init)
        outs.append(acc / l)
    o_ref[0] = jnp.concatenate(outs, axis=0).T.astype(BF16)


def _mla_flash(q, k, vt):
    b, s, _ = q.shape
    nk = s // FLASH_TK
    return pl.pallas_call(
        _flash_kernel,
        name="mla_flash",
        grid=(b, MLA_HEADS // 2, s // FLASH_TQ),
        in_specs=[
            pl.BlockSpec((1, FLASH_TQ, 2 * HEAD_PAD), lambda bi, hp, qi: (bi, qi, hp)),
            pl.BlockSpec((1, s, 2 * HEAD_PAD), lambda bi, hp, qi: (bi, 0, hp)),
            pl.BlockSpec((1, nk, 2 * MLA_V, FLASH_TK), lambda bi, hp, qi: (bi, 0, hp, 0)),
        ],
        out_specs=pl.BlockSpec((1, FLASH_TQ, 2 * MLA_V), lambda bi, hp, qi: (bi, qi, hp)),
        out_shape=jax.ShapeDtypeStruct((b, s, MLA_WIDTH), BF16),
        compiler_params=pltpu.CompilerParams(dimension_semantics=("arbitrary", "arbitrary", "arbitrary"),
                                             vmem_limit_bytes=VMEM_LIMIT),
    )(q, k, vt)


def _dilated_kernel(q_ref, kc_ref, kp_ref, kn_ref, vc_ref, vp_ref, vn_ref, o_ref, lse_ref, kw_ref, vw_ref, *, n_rows):
    lb = q_ref.shape[1]
    l0 = pl.program_id(2) * lb
    kw_ref[0:DIL_HALF] = kp_ref[0]
    kw_ref[DIL_HALF:DIL_HALF + lb] = kc_ref[0]
    kw_ref[DIL_HALF + lb:] = kn_ref[0]
    vw_ref[0:DIL_HALF] = vp_ref[0]
    vw_ref[DIL_HALF:DIL_HALF + lb] = vc_ref[0]
    vw_ref[DIL_HALF + lb:] = vn_ref[0]
    win = DIL_SUB + 2 * DIL_HALF
    a = lax.broadcasted_iota(jnp.int32, (DIL_SUB, win), 0)
    c = lax.broadcasted_iota(jnp.int32, (DIL_SUB, win), 1)
    band = (c >= a) & (c <= a + 2 * DIL_HALF)

    def sub(j, carry):
        r0 = pl.multiple_of(j * DIL_SUB, DIL_SUB)
        kidx = l0 - DIL_HALF + r0 + c
        mask = band & (kidx >= 0) & (kidx < n_rows)
        for h in range(DIL_HEADS):
            hs = slice(DIL_HEAD_DIM * h, DIL_HEAD_DIM * (h + 1))
            qh = q_ref[0, pl.ds(r0, DIL_SUB), hs]
            kh = kw_ref[pl.ds(r0, win), hs]
            vh = vw_ref[pl.ds(r0, win), hs]
            s = lax.dot_general(qh, kh, (((1,), (1,)), ((), ())), preferred_element_type=F32)
            s = jnp.where(mask, s, NEG)
            m = jnp.max(s, axis=-1, keepdims=True)
            p = jnp.exp(s - m)
            den = jnp.sum(p, axis=-1, keepdims=True)
            num = jnp.dot(p.astype(BF16), vh, preferred_element_type=F32)
            o_ref[0, pl.ds(r0, DIL_SUB), hs] = num / den
            lse_ref[0, pl.ds(r0, DIL_SUB), hs] = jnp.broadcast_to(m + jnp.log(den), (DIL_SUB, DIL_HEAD_DIM))
        return carry

    lax.fori_loop(0, lb // DIL_SUB, sub, 0)


def _dilated(dq, dk, dv, dilation):
    b, s, w = dq.shape
    n_rows = s // dilation
    lb = min(DIL_LB, n_rows)
    hb = lb // DIL_HALF
    last = n_rows // DIL_HALF - 1
    view = lambda t: t.reshape(b, n_rows, dilation * w)
    cur = pl.BlockSpec((1, lb, w), lambda bi, r, l: (bi, l, r))
    prev = pl.BlockSpec((1, DIL_HALF, w), lambda bi, r, l: (bi, jnp.maximum(l * hb - 1, 0), r))
    nxt = pl.BlockSpec((1, DIL_HALF, w), lambda bi, r, l: (bi, jnp.minimum((l + 1) * hb, last), r))
    o, lse = pl.pallas_call(
        functools.partial(_dilated_kernel, n_rows=n_rows),
        name=f"dilated_{dilation}",
        grid=(b, dilation, n_rows // lb),
        in_specs=[cur, cur, prev, nxt, cur, prev, nxt],
        out_specs=(cur, cur),
        out_shape=(jax.ShapeDtypeStruct((b, n_rows, dilation * w), F32),) * 2,
        scratch_shapes=[pltpu.VMEM((lb + 2 * DIL_HALF, w), BF16)] * 2,
        compiler_params=pltpu.CompilerParams(dimension_semantics=("arbitrary",) * 3, vmem_limit_bytes=VMEM_LIMIT),
    )(view(dq), view(dk), view(dk), view(dk), view(dv), view(dv), view(dv))
    return o.reshape(b, s, w), lse.reshape(b, s, w)


def _mixer_out_kernel(h_ref, oa_ref, o1_ref, l1_ref, o2_ref, l2_ref, o3_ref, l3_ref, gate_ref,
                      wba_ref, wbb_ref, wo_ref, post_ref, out_ref):
    l1 = l1_ref[0]
    l2 = l2_ref[0]
    l3 = l3_ref[0]
    lm = jnp.maximum(jnp.maximum(l1, l2), l3)
    e1 = jnp.exp(l1 - lm)
    e2 = jnp.exp(l2 - lm)
    e3 = jnp.exp(l3 - lm)
    ob = (e1 * o1_ref[0] + e2 * o2_ref[0] + e3 * o3_ref[0]) / (e1 + e2 + e3)
    ya = jnp.dot(oa_ref[0], wba_ref[...], preferred_element_type=F32)
    yb = jnp.dot(ob.astype(BF16), wbb_ref[...], preferred_element_type=F32)
    gate = gate_ref[0].astype(F32)
    merged = gate[:, :D_MODEL] * ya + gate[:, D_MODEL:] * yb
    mix = jnp.dot(merged.astype(BF16), wo_ref[...], preferred_element_type=F32)
    out_ref[0] = h_ref[0] + _rms(mix, post_ref[...])


def _mixer_out(h1, oa, parts, gates, w_branch_a, w_branch_b, w_out, mix_post_g):
    b, s, _ = h1.shape
    tm = MIX_TM
    tok = lambda c: pl.BlockSpec((1, tm, c), lambda bi, i: (bi, i, 0))
    flat = [t for pair in parts for t in pair]
    return pl.pallas_call(
        _mixer_out_kernel,
        name="mixer_out",
        grid=(b, s // tm),
        in_specs=[tok(D_MODEL), tok(MLA_WIDTH)] + [tok(DIL_WIDTH)] * 6 + [tok(N_BRANCH * D_MODEL),
                  _const_spec((MLA_WIDTH, D_MODEL)), _const_spec((DIL_WIDTH, D_MODEL)),
                  _const_spec((D_MODEL, D_MODEL)), _const_spec((1, D_MODEL))],
        out_specs=tok(D_MODEL),
        out_shape=jax.ShapeDtypeStruct((b, s, D_MODEL), F32),
        compiler_params=pltpu.CompilerParams(dimension_semantics=("arbitrary", "arbitrary"),
                                             vmem_limit_bytes=VMEM_LIMIT),
    )(h1, oa, *flat, gates, w_branch_a.astype(BF16), w_branch_b.astype(BF16), w_out.astype(BF16),
      mix_post_g.reshape(1, D_MODEL))


def kernel(x, positions, ffn1_pre_g, ffn1_post_g, ffn1_w_gate, ffn1_w_up, ffn1_w_down, mix_pre_g, w_in, b_gate,
           q_norm_g, w_uq, kv_norm_g, w_uk, w_uv, w_branch_a, w_branch_b, w_out, mix_post_g,
           ffn2_pre_g, ffn2_post_g, ffn2_w_gate, ffn2_w_up, ffn2_w_down):
    b, s, d = x.shape
    depth = ffn1_pre_g.shape[0]
    h = x
    for l in range(depth):
        h = _ffn(h.reshape(b * s, d), ffn1_pre_g[l], ffn1_post_g[l], ffn1_w_gate[l], ffn1_w_up[l],
                 ffn1_w_down[l]).reshape(b, s, d)
        q, k, vt, dq, dk, dv, gates = _mixer_in(h, positions, mix_pre_g[l], w_in[l], b_gate[l], q_norm_g[l],
                                                w_uq[l], kv_norm_g[l], w_uk[l], w_uv[l])
        oa = _mla_flash(q, k, vt)
        parts = [_dilated(dq, dk, dv, dil) for (_, dil) in DIL_PATTERNS]
        h = _mixer_out(h, oa, parts, gates, w_branch_a[l], w_branch_b[l], w_out[l], mix_post_g[l])
        h = _ffn(h.reshape(b * s, d), ffn2_pre_g[l], ffn2_post_g[l], ffn2_w_gate[l], ffn2_w_up[l],
                 ffn2_w_down[l]).reshape(b, s, d)
    return h
```

```python
import functools

import jax
import jax.numpy as jnp
import numpy as np
from jax import lax
from jax.experimental import pallas as pl
from jax.experimental.pallas import tpu as pltpu

F32 = jnp.float32
BF16 = jnp.bfloat16

D_MODEL = 1024
D_FF = 2816
EPS = 1e-6
MLA_HEADS = 8
MLA_Q_RANK = 384
MLA_KV_RANK = 256
MLA_NOPE = 64
MLA_ROPE = 32
MLA_V = 64
MLA_THETA = 10000.0
MLA_WIDTH = MLA_HEADS * MLA_V
DIL_HEADS = 8
DIL_HEAD_DIM = 64
DIL_PATTERNS = ((128, 1), (512, 4), (2048, 16))
DIL_WIDTH = DIL_HEADS * DIL_HEAD_DIM
ROPE_THETA = 500000.0
ROPE_DIM = DIL_HEAD_DIM // 4
N_BRANCH = 2
NEG = -1e30

LANES = 128
HEAD_PAD = LANES
VMEM_LIMIT = 56 * 1024 * 1024

FFN_TM = 512
FFN_FC = 256
MIX_TM = 512
FLASH_TQ = 256
FLASH_TK = MIX_TM
FLASH_TS = 128
VT_ROWS = MLA_V + 16
LOG2E = 1.4426950408889634
DIL_LB = 512
DIL_SUB = 128
DIL_HALF = 64


def _rms(x, g):
    ms = jnp.mean(x * x, axis=-1, keepdims=True)
    return x * lax.rsqrt(ms + EPS) * g


def _const_spec(shape):
    nd = len(shape)
    return pl.BlockSpec(shape, lambda *_: (0,) * nd, pipeline_mode=pl.Buffered(1))


def _ffn_kernel(x_ref, pre_ref, post_ref, wgu_ref, wd_ref, o_ref, hm_ref):
    x = x_ref[...]
    xn = _rms(x, pre_ref[...]).astype(BF16)
    for c in range(D_FF // FFN_FC):
        gu = jnp.dot(xn, wgu_ref[:, 2 * FFN_FC * c:2 * FFN_FC * (c + 1)], preferred_element_type=F32)
        g = gu[:, :FFN_FC]
        u = gu[:, FFN_FC:]
        hm_ref[:, FFN_FC * c:FFN_FC * (c + 1)] = (g * jax.nn.sigmoid(g) * u).astype(BF16)
    f = jnp.dot(hm_ref[...], wd_ref[...], preferred_element_type=F32)
    o_ref[...] = x + 0.5 * _rms(f, post_ref[...])


def _ffn(h, pre_g, post_g, w_gate, w_up, w_down):
    t = h.shape[0]
    nc = D_FF // FFN_FC
    wgu = jnp.concatenate([w_gate.reshape(D_MODEL, nc, FFN_FC), w_up.reshape(D_MODEL, nc, FFN_FC)],
                          axis=-1).reshape(D_MODEL, 2 * D_FF).astype(BF16)
    wd = w_down.astype(BF16)
    return pl.pallas_call(
        _ffn_kernel,
        name="ffn",
        grid=(t // FFN_TM,),
        in_specs=[
            pl.BlockSpec((FFN_TM, D_MODEL), lambda i: (i, 0)),
            _const_spec((1, D_MODEL)),
            _const_spec((1, D_MODEL)),
            _const_spec((D_MODEL, 2 * D_FF)),
            _const_spec((D_FF, D_MODEL)),
        ],
        out_specs=pl.BlockSpec((FFN_TM, D_MODEL), lambda i: (i, 0)),
        out_shape=jax.ShapeDtypeStruct((t, D_MODEL), F32),
        scratch_shapes=[pltpu.VMEM((FFN_TM, D_FF), BF16)],
        compiler_params=pltpu.CompilerParams(dimension_semantics=("arbitrary",), vmem_limit_bytes=VMEM_LIMIT),
    )(h, pre_g.reshape(1, D_MODEL), post_g.reshape(1, D_MODEL), wgu, wd)


def _rope_tile(x, cos, sin_lo, sin_hi, half):
    return x * cos + pltpu.roll(x, LANES - half, 1) * sin_lo + pltpu.roll(x, half, 1) * sin_hi


def _mixer_in_kernel(h_ref, pos_ref, pre_ref, wa_ref, wd_ref, wg_ref, bg_ref, qg_ref, kvg_ref,
                     wuq_ref, wuk_ref, wuvt_ref, tab_ref,
                     q_ref, k_ref, vt_ref, dq_ref, dk_ref, dv_ref, gate_ref):
    u = _rms(h_ref[0], pre_ref[...]).astype(BF16)
    pos = pos_ref[0]
    tab = tab_ref[...]
    ang_a = pos * tab[0:1, :]
    cos_a = jnp.cos(ang_a)
    sin_a = jnp.sin(ang_a)
    sin_a_lo = sin_a * tab[1:2, :]
    sin_a_hi = sin_a * tab[2:3, :]
    ang_b = pos * tab[3:4, :]
    cos_b = jnp.cos(ang_b)
    sin_b = jnp.sin(ang_b)
    sin_b_lo = sin_b * tab[4:5, :]
    sin_b_hi = sin_b * tab[5:6, :]

    pa = jnp.dot(u, wa_ref[...], preferred_element_type=F32)
    cq = pa[:, :MLA_Q_RANK]
    ckv = pa[:, MLA_Q_RANK:MLA_Q_RANK + MLA_KV_RANK]
    kr = _rope_tile(pa[:, MLA_Q_RANK + MLA_KV_RANK:], cos_a, sin_a_lo, sin_a_hi, MLA_ROPE // 2)
    qn = _rms(cq, qg_ref[...]).astype(BF16)
    ckvn = _rms(ckv, kvg_ref[...]).astype(BF16)
    q = jnp.dot(qn, wuq_ref[...], preferred_element_type=F32)
    kn = jnp.dot(ckvn, wuk_ref[...], preferred_element_type=F32)
    scale = LOG2E * (MLA_NOPE + MLA_ROPE) ** -0.5
    for h in range(MLA_HEADS):
        sl = slice(HEAD_PAD * h, HEAD_PAD * (h + 1))
        qh = _rope_tile(q[:, sl], cos_a, sin_a_lo, sin_a_hi, MLA_ROPE // 2)
        q_ref[0, :, sl] = (qh * scale).astype(BF16)
        k_ref[0, :, sl] = (kn[:, sl] + kr).astype(BF16)
    vt = lax.dot_general(wuvt_ref[...], ckvn, (((1,), (1,)), ((), ())), preferred_element_type=F32)
    row = lax.broadcasted_iota(jnp.int32, vt.shape, 0)
    vt_ref[0, 0] = jnp.where(row % VT_ROWS >= MLA_V, 1.0, vt).astype(BF16)

    pd = jnp.dot(u, wd_ref[...], preferred_element_type=F32)
    dscale = DIL_HEAD_DIM ** -0.5
    for c in range(DIL_WIDTH // LANES):
        sl = slice(LANES * c, LANES * (c + 1))
        ks = slice(DIL_WIDTH + LANES * c, DIL_WIDTH + LANES * (c + 1))
        dq_ref[0, :, sl] = (_rope_tile(pd[:, sl], cos_b, sin_b_lo, sin_b_hi, ROPE_DIM // 2) * dscale).astype(BF16)
        dk_ref[0, :, sl] = _rope_tile(pd[:, ks], cos_b, sin_b_lo, sin_b_hi, ROPE_DIM // 2).astype(BF16)
    dv_ref[0] = pd[:, 2 * DIL_WIDTH:].astype(BF16)

    pg = jnp.dot(u, wg_ref[...], preferred_element_type=F32) + bg_ref[...]
    gate_ref[0] = jax.nn.sigmoid(pg).astype(BF16)


def _rope_tables():
    half_a = MLA_ROPE // 2
    inv_a = 1.0 / (jnp.float32(MLA_THETA) ** (jnp.arange(half_a, dtype=F32) / half_a))
    half_b = ROPE_DIM // 2
    inv_b = 1.0 / (jnp.float32(ROPE_THETA) ** (jnp.arange(half_b, dtype=F32) / half_b))
    z = lambda n: jnp.zeros((n,), F32)
    o = lambda n: jnp.ones((n,), F32)
    row_a = jnp.concatenate([z(MLA_NOPE), inv_a, inv_a, z(HEAD_PAD - MLA_NOPE - MLA_ROPE)])
    lo_a = jnp.concatenate([z(MLA_NOPE), -o(half_a), z(half_a), z(HEAD_PAD - MLA_NOPE - MLA_ROPE)])
    hi_a = jnp.concatenate([z(MLA_NOPE), z(half_a), o(half_a), z(HEAD_PAD - MLA_NOPE - MLA_ROPE)])
    rest = DIL_HEAD_DIM - ROPE_DIM
    head_b = jnp.concatenate([inv_b, inv_b, z(rest)])
    head_lo = jnp.concatenate([-o(half_b), z(half_b), z(rest)])
    head_hi = jnp.concatenate([z(half_b), o(half_b), z(rest)])
    rep = LANES // DIL_HEAD_DIM
    rows = [row_a, lo_a, hi_a, jnp.tile(head_b, rep), jnp.tile(head_lo, rep), jnp.tile(head_hi, rep), z(LANES), z(LANES)]
    return jnp.stack(rows)


def _mixer_in(h1, positions, mix_pre_g, w_in, b_gate, q_norm_g, w_uq, kv_norm_g, w_uk, w_uv):
    b, s, _ = h1.shape
    o0 = 0
    o1 = o0 + MLA_Q_RANK
    o2 = o1 + MLA_KV_RANK
    o3 = o2 + MLA_ROPE
    o4 = o3 + 3 * DIL_WIDTH
    pad_hi = HEAD_PAD - MLA_NOPE - MLA_ROPE
    w_kr = jnp.pad(w_in[:, o2:o3], ((0, 0), (MLA_NOPE, pad_hi)))
    wa = jnp.concatenate([w_in[:, o0:o2], w_kr], axis=1).astype(BF16)
    wd = w_in[:, o3:o4].astype(BF16)
    wg = w_in[:, o4:].astype(BF16)
    wuq = jnp.pad(w_uq.reshape(MLA_Q_RANK, MLA_HEADS, MLA_NOPE + MLA_ROPE),
                  ((0, 0), (0, 0), (0, pad_hi))).reshape(MLA_Q_RANK, MLA_HEADS * HEAD_PAD).astype(BF16)
    wuk = jnp.pad(w_uk.reshape(MLA_KV_RANK, MLA_HEADS, MLA_NOPE),
                  ((0, 0), (0, 0), (0, HEAD_PAD - MLA_NOPE))).reshape(MLA_KV_RANK, MLA_HEADS * HEAD_PAD).astype(BF16)
    wuvt = jnp.pad(w_uv.T.reshape(MLA_HEADS, MLA_V, MLA_KV_RANK), ((0, 0), (0, VT_ROWS - MLA_V), (0, 0))
                   ).reshape(MLA_HEADS * VT_ROWS, MLA_KV_RANK).astype(BF16)
    pos = positions.astype(F32).reshape(b, s, 1)
    tm = MIX_TM
    hp = MLA_HEADS * HEAD_PAD
    tok = lambda c: pl.BlockSpec((1, tm, c), lambda bi, i: (bi, i, 0))
    out_shapes = (
        jax.ShapeDtypeStruct((b, s, hp), BF16),
        jax.ShapeDtypeStruct((b, s, hp), BF16),
        jax.ShapeDtypeStruct((b, s // tm, MLA_HEADS * VT_ROWS, tm), BF16),
        jax.ShapeDtypeStruct((b, s, DIL_WIDTH), BF16),
        jax.ShapeDtypeStruct((b, s, DIL_WIDTH), BF16),
        jax.ShapeDtypeStruct((b, s, DIL_WIDTH), BF16),
        jax.ShapeDtypeStruct((b, s, N_BRANCH * D_MODEL), BF16),
    )
    out_specs = (
        tok(hp), tok(hp),
        pl.BlockSpec((1, 1, MLA_HEADS * VT_ROWS, tm), lambda bi, i: (bi, i, 0, 0)),
        tok(DIL_WIDTH), tok(DIL_WIDTH), tok(DIL_WIDTH), tok(N_BRANCH * D_MODEL),
    )
    return pl.pallas_call(
        _mixer_in_kernel,
        name="mixer_in",
        grid=(b, s // tm),
        in_specs=[
            tok(D_MODEL), tok(1),
            _const_spec((1, D_MODEL)),
            _const_spec(wa.shape), _const_spec(wd.shape), _const_spec(wg.shape),
            _const_spec((1, N_BRANCH * D_MODEL)),
            _const_spec((1, MLA_Q_RANK)), _const_spec((1, MLA_KV_RANK)),
            _const_spec(wuq.shape), _const_spec(wuk.shape), _const_spec(wuvt.shape),
            _const_spec((8, LANES)),
        ],
        out_specs=out_specs,
        out_shape=out_shapes,
        compiler_params=pltpu.CompilerParams(dimension_semantics=("arbitrary", "arbitrary"),
                                             vmem_limit_bytes=VMEM_LIMIT),
    )(h1, pos, mix_pre_g.reshape(1, D_MODEL), wa, wd, wg, b_gate.reshape(1, -1),
      q_norm_g.reshape(1, -1), kv_norm_g.reshape(1, -1), wuq, wuk, wuvt, _rope_tables())


def _flash_kernel(q_ref, k_ref, vt_ref, o_ref, sa_ref, sb_ref):
    nk = k_ref.shape[1] // FLASH_TK
    qs = [q_ref[0, :, HEAD_PAD * hh:HEAD_PAD * (hh + 1)] for hh in range(2)]

    def scores(c, hh, buf):
        cmax = None
        for part in range(FLASH_TK // FLASH_TS):
            r0 = pl.multiple_of(c * FLASH_TK + part * FLASH_TS, FLASH_TS)
            k = k_ref[0, pl.ds(r0, FLASH_TS), HEAD_PAD * hh:HEAD_PAD * (hh + 1)]
            st = lax.dot_general(k, qs[hh], (((1,), (1,)), ((), ())), preferred_element_type=F32)
            buf[hh, FLASH_TS * part:FLASH_TS * (part + 1)] = st
            pm = jnp.max(st, axis=0, keepdims=True)
            cmax = pm if cmax is None else jnp.maximum(cmax, pm)
        return cmax

    def consume(c, hh, buf, m, cmax, acc):
        m_new = jnp.maximum(m, cmax)
        alpha = jnp.exp2(m - m_new)
        p = jnp.exp2((buf[hh] - m_new).astype(BF16))
        vt = vt_ref[0, c, VT_ROWS * hh:VT_ROWS * (hh + 1), :]
        return m_new, alpha * acc + jnp.dot(vt, p, preferred_element_type=F32)

    def step(j, carry):
        c0 = 2 * j
        c2 = jnp.minimum(c0 + 2, nk - 1)
        new = []
        for hh in range(2):
            m, cmax, acc = carry[hh]
            cmax1 = scores(c0 + 1, hh, sb_ref)
            m, acc = consume(c0, hh, sa_ref, m, cmax, acc)
            cmax2 = scores(c2, hh, sa_ref)
            m, acc = consume(c0 + 1, hh, sb_ref, m, cmax1, acc)
            new.append((m, cmax2, acc))
        return tuple(new)

    init = tuple((jnp.full((1, FLASH_TQ), -jnp.inf, F32), scores(0, hh, sa_ref),
                  jnp.zeros((VT_ROWS, FLASH_TQ), F32)) for hh in range(2))
    res = lax.fori_loop(0, nk // 2, step, init)
    outs = [acc[:MLA_V] / acc[MLA_V:MLA_V + 1] for (_, _, acc) in res]
    o_ref[0] = jnp.concatenate(outs, axis=0).T.astype(BF16)


def _mla_flash(q, k, vt):
    b, s, _ = q.shape
    nk = s // FLASH_TK
    return pl.pallas_call(
        _flash_kernel,
        name="mla_flash",
        grid=(b, MLA_HEADS // 2, s // FLASH_TQ),
        in_specs=[
            pl.BlockSpec((1, FLASH_TQ, 2 * HEAD_PAD), lambda bi, hp, qi: (bi, qi, hp)),
            pl.BlockSpec((1, s, 2 * HEAD_PAD), lambda bi, hp, qi: (bi, 0, hp)),
            pl.BlockSpec((1, nk, 2 * VT_ROWS, FLASH_TK), lambda bi, hp, qi: (bi, 0, hp, 0)),
        ],
        out_specs=pl.BlockSpec((1, FLASH_TQ, 2 * MLA_V), lambda bi, hp, qi: (bi, qi, hp)),
        out_shape=jax.ShapeDtypeStruct((b, s, MLA_WIDTH), BF16),
        scratch_shapes=[pltpu.VMEM((2, FLASH_TK, FLASH_TQ), F32)] * 2,
        compiler_params=pltpu.CompilerParams(dimension_semantics=("arbitrary", "arbitrary", "arbitrary"),
                                             vmem_limit_bytes=VMEM_LIMIT),
    )(q, k, vt)


def _dilated_kernel(q_ref, kc_ref, kp_ref, kn_ref, vc_ref, vp_ref, vn_ref, o_ref, lse_ref, kw_ref, vw_ref, *, n_rows):
    lb = q_ref.shape[1]
    l0 = pl.program_id(2) * lb
    kw_ref[0:DIL_HALF] = kp_ref[0]
    kw_ref[DIL_HALF:DIL_HALF + lb] = kc_ref[0]
    kw_ref[DIL_HALF + lb:] = kn_ref[0]
    vw_ref[0:DIL_HALF] = vp_ref[0]
    vw_ref[DIL_HALF:DIL_HALF + lb] = vc_ref[0]
    vw_ref[DIL_HALF + lb:] = vn_ref[0]
    win = DIL_SUB + 2 * DIL_HALF
    a = lax.broadcasted_iota(jnp.int32, (DIL_SUB, win), 0)
    c = lax.broadcasted_iota(jnp.int32, (DIL_SUB, win), 1)
    band = (c >= a) & (c <= a + 2 * DIL_HALF)

    def sub(j, carry):
        r0 = pl.multiple_of(j * DIL_SUB, DIL_SUB)
        kidx = l0 - DIL_HALF + r0 + c
        mask = band & (kidx >= 0) & (kidx < n_rows)
        for h in range(DIL_HEADS):
            hs = slice(DIL_HEAD_DIM * h, DIL_HEAD_DIM * (h + 1))
            qh = q_ref[0, pl.ds(r0, DIL_SUB), hs]
            kh = kw_ref[pl.ds(r0, win), hs]
            vh = vw_ref[pl.ds(r0, win), hs]
            s = lax.dot_general(qh, kh, (((1,), (1,)), ((), ())), preferred_element_type=F32)
            s = jnp.where(mask, s, NEG)
            m = jnp.max(s, axis=-1, keepdims=True)
            p = jnp.exp(s - m)
            den = jnp.sum(p, axis=-1, keepdims=True)
            num = jnp.dot(p.astype(BF16), vh, preferred_element_type=F32)
            o_ref[0, pl.ds(r0, DIL_SUB), hs] = num / den
            lse_ref[0, pl.ds(r0, DIL_SUB), hs] = jnp.broadcast_to(m + jnp.log(den), (DIL_SUB, DIL_HEAD_DIM))
        return carry

    lax.fori_loop(0, lb // DIL_SUB, sub, 0)


def _dilated(dq, dk, dv, dilation):
    b, s, w = dq.shape
    n_rows = s // dilation
    lb = min(DIL_LB, n_rows)
    hb = lb // DIL_HALF
    last = n_rows // DIL_HALF - 1
    view = lambda t: t.reshape(b, n_rows, dilation * w)
    cur = pl.BlockSpec((1, lb, w), lambda bi, r, l: (bi, l, r))
    prev = pl.BlockSpec((1, DIL_HALF, w), lambda bi, r, l: (bi, jnp.maximum(l * hb - 1, 0), r))
    nxt = pl.BlockSpec((1, DIL_HALF, w), lambda bi, r, l: (bi, jnp.minimum((l + 1) * hb, last), r))
    o, lse = pl.pallas_call(
        functools.partial(_dilated_kernel, n_rows=n_rows),
        name=f"dilated_{dilation}",
        grid=(b, dilation, n_rows // lb),
        in_specs=[cur, cur, prev, nxt, cur, prev, nxt],
        out_specs=(cur, cur),
        out_shape=(jax.ShapeDtypeStruct((b, n_rows, dilation * w), F32),) * 2,
        scratch_shapes=[pltpu.VMEM((lb + 2 * DIL_HALF, w), BF16)] * 2,
        compiler_params=pltpu.CompilerParams(dimension_semantics=("arbitrary",) * 3, vmem_limit_bytes=VMEM_LIMIT),
    )(view(dq), view(dk), view(dk), view(dk), view(dv), view(dv), view(dv))
    return o.reshape(b, s, w), lse.reshape(b, s, w)


def _mixer_out_kernel(h_ref, oa_ref, o1_ref, l1_ref, o2_ref, l2_ref, o3_ref, l3_ref, gate_ref,
                      wba_ref, wbb_ref, wo_ref, post_ref, out_ref):
    l1 = l1_ref[0]
    l2 = l2_ref[0]
    l3 = l3_ref[0]
    lm = jnp.maximum(jnp.maximum(l1, l2), l3)
    e1 = jnp.exp(l1 - lm)
    e2 = jnp.exp(l2 - lm)
    e3 = jnp.exp(l3 - lm)
    ob = (e1 * o1_ref[0] + e2 * o2_ref[0] + e3 * o3_ref[0]) / (e1 + e2 + e3)
    ya = jnp.dot(oa_ref[0], wba_ref[...], preferred_element_type=F32)
    yb = jnp.dot(ob.astype(BF16), wbb_ref[...], preferred_element_type=F32)
    gate = gate_ref[0].astype(F32)
    merged = gate[:, :D_MODEL] * ya + gate[:, D_MODEL:] * yb
    mix = jnp.dot(merged.astype(BF16), wo_ref[...], preferred_element_type=F32)
    out_ref[0] = h_ref[0] + _rms(mix, post_ref[...])


def _mixer_out(h1, oa, parts, gates, w_branch_a, w_branch_b, w_out, mix_post_g):
    b, s, _ = h1.shape
    tm = MIX_TM
    tok = lambda c: pl.BlockSpec((1, tm, c), lambda bi, i: (bi, i, 0))
    flat = [t for pair in parts for t in pair]
    return pl.pallas_call(
        _mixer_out_kernel,
        name="mixer_out",
        grid=(b, s // tm),
        in_specs=[tok(D_MODEL), tok(MLA_WIDTH)] + [tok(DIL_WIDTH)] * 6 + [tok(N_BRANCH * D_MODEL),
                  _const_spec((MLA_WIDTH, D_MODEL)), _const_spec((DIL_WIDTH, D_MODEL)),
                  _const_spec((D_MODEL, D_MODEL)), _const_spec((1, D_MODEL))],
        out_specs=tok(D_MODEL),
        out_shape=jax.ShapeDtypeStruct((b, s, D_MODEL), F32),
        compiler_params=pltpu.CompilerParams(dimension_semantics=("arbitrary", "arbitrary"),
                                             vmem_limit_bytes=VMEM_LIMIT),
    )(h1, oa, *flat, gates, w_branch_a.astype(BF16), w_branch_b.astype(BF16), w_out.astype(BF16),
      mix_post_g.reshape(1, D_MODEL))


def kernel(x, positions, ffn1_pre_g, ffn1_post_g, ffn1_w_gate, ffn1_w_up, ffn1_w_down, mix_pre_g, w_in, b_gate,
           q_norm_g, w_uq, kv_norm_g, w_uk, w_uv, w_branch_a, w_branch_b, w_out, mix_post_g,
           ffn2_pre_g, ffn2_post_g, ffn2_w_gate, ffn2_w_up, ffn2_w_down):
    b, s, d = x.shape
    depth = ffn1_pre_g.shape[0]
    h = x
    for l in range(depth):
        h = _ffn(h.reshape(b * s, d), ffn1_pre_g[l], ffn1_post_g[l], ffn1_w_gate[l], ffn1_w_up[l],
                 ffn1_w_down[l]).reshape(b, s, d)
        q, k, vt, dq, dk, dv, gates = _mixer_in(h, positions, mix_pre_g[l], w_in[l], b_gate[l], q_norm_g[l],
                                                w_uq[l], kv_norm_g[l], w_uk[l], w_uv[l])
        oa = _mla_flash(q, k, vt)
        parts = [_dilated(dq, dk, dv, dil) for (_, dil) in DIL_PATTERNS]
        h = _mixer_out(h, oa, parts, gates, w_branch_a[l], w_branch_b[l], w_out[l], mix_post_g[l])
        h = _ffn(h.reshape(b * s, d), ffn2_pre_g[l], ffn2_post_g[l], ffn2_w_gate[l], ffn2_w_up[l],
                 ffn2_w_down[l]).reshape(b, s, d)
    return h
```

```python
import functools

import jax
import jax.numpy as jnp
import numpy as np
from jax import lax
from jax.experimental import pallas as pl
from jax.experimental.pallas import tpu as pltpu

F32 = jnp.float32
BF16 = jnp.bfloat16

D_MODEL = 1024
D_FF = 2816
EPS = 1e-6
MLA_HEADS = 8
MLA_Q_RANK = 384
MLA_KV_RANK = 256
MLA_NOPE = 64
MLA_ROPE = 32
MLA_V = 64
MLA_THETA = 10000.0
MLA_WIDTH = MLA_HEADS * MLA_V
DIL_HEADS = 8
DIL_HEAD_DIM = 64
DIL_PATTERNS = ((128, 1), (512, 4), (2048, 16))
DIL_WIDTH = DIL_HEADS * DIL_HEAD_DIM
ROPE_THETA = 500000.0
ROPE_DIM = DIL_HEAD_DIM // 4
N_BRANCH = 2
NEG = -1e30

LANES = 128
HEAD_PAD = LANES
VMEM_LIMIT = 56 * 1024 * 1024

FFN_TM = 512
FFN_FC = 256
MIX_TM = 512
FLASH_TQ = 256
FLASH_TK = MIX_TM
FLASH_TS = 512
VT_ROWS = MLA_V + 16
LOG2E = 1.4426950408889634
DIL_LB = 512
DIL_SUB = 128
DIL_HALF = 64


def _rms(x, g):
    ms = jnp.mean(x * x, axis=-1, keepdims=True)
    return x * lax.rsqrt(ms + EPS) * g


def _const_spec(shape):
    nd = len(shape)
    return pl.BlockSpec(shape, lambda *_: (0,) * nd, pipeline_mode=pl.Buffered(1))


def _ffn_kernel(x_ref, pre_ref, post_ref, wgu_ref, wd_ref, o_ref, hm_ref):
    x = x_ref[...]
    xn = _rms(x, pre_ref[...]).astype(BF16)
    for c in range(D_FF // FFN_FC):
        gu = jnp.dot(xn, wgu_ref[:, 2 * FFN_FC * c:2 * FFN_FC * (c + 1)], preferred_element_type=F32)
        g = gu[:, :FFN_FC]
        u = gu[:, FFN_FC:]
        hm_ref[:, FFN_FC * c:FFN_FC * (c + 1)] = (g * jax.nn.sigmoid(g) * u).astype(BF16)
    f = jnp.dot(hm_ref[...], wd_ref[...], preferred_element_type=F32)
    o_ref[...] = x + 0.5 * _rms(f, post_ref[...])


def _ffn(h, pre_g, post_g, w_gate, w_up, w_down):
    t = h.shape[0]
    nc = D_FF // FFN_FC
    wgu = jnp.concatenate([w_gate.reshape(D_MODEL, nc, FFN_FC), w_up.reshape(D_MODEL, nc, FFN_FC)],
                          axis=-1).reshape(D_MODEL, 2 * D_FF).astype(BF16)
    wd = w_down.astype(BF16)
    return pl.pallas_call(
        _ffn_kernel,
        name="ffn",
        grid=(t // FFN_TM,),
        in_specs=[
            pl.BlockSpec((FFN_TM, D_MODEL), lambda i: (i, 0)),
            _const_spec((1, D_MODEL)),
            _const_spec((1, D_MODEL)),
            _const_spec((D_MODEL, 2 * D_FF)),
            _const_spec((D_FF, D_MODEL)),
        ],
        out_specs=pl.BlockSpec((FFN_TM, D_MODEL), lambda i: (i, 0)),
        out_shape=jax.ShapeDtypeStruct((t, D_MODEL), F32),
        scratch_shapes=[pltpu.VMEM((FFN_TM, D_FF), BF16)],
        compiler_params=pltpu.CompilerParams(dimension_semantics=("arbitrary",), vmem_limit_bytes=VMEM_LIMIT),
    )(h, pre_g.reshape(1, D_MODEL), post_g.reshape(1, D_MODEL), wgu, wd)


def _rope_tile(x, cos, sin_lo, sin_hi, half):
    return x * cos + pltpu.roll(x, LANES - half, 1) * sin_lo + pltpu.roll(x, half, 1) * sin_hi


def _mixer_in_kernel(h_ref, pos_ref, pre_ref, wa_ref, wd_ref, wg_ref, bg_ref, qg_ref, kvg_ref,
                     wuq_ref, wuk_ref, wuvt_ref, tab_ref,
                     q_ref, k_ref, vt_ref, dq_ref, dk_ref, dv_ref, gate_ref):
    u = _rms(h_ref[0], pre_ref[...]).astype(BF16)
    pos = pos_ref[0]
    tab = tab_ref[...]
    ang_a = pos * tab[0:1, :]
    cos_a = jnp.cos(ang_a)
    sin_a = jnp.sin(ang_a)
    sin_a_lo = sin_a * tab[1:2, :]
    sin_a_hi = sin_a * tab[2:3, :]
    ang_b = pos * tab[3:4, :]
    cos_b = jnp.cos(ang_b)
    sin_b = jnp.sin(ang_b)
    sin_b_lo = sin_b * tab[4:5, :]
    sin_b_hi = sin_b * tab[5:6, :]

    pa = jnp.dot(u, wa_ref[...], preferred_element_type=F32)
    cq = pa[:, :MLA_Q_RANK]
    ckv = pa[:, MLA_Q_RANK:MLA_Q_RANK + MLA_KV_RANK]
    kr = _rope_tile(pa[:, MLA_Q_RANK + MLA_KV_RANK:], cos_a, sin_a_lo, sin_a_hi, MLA_ROPE // 2)
    qn = _rms(cq, qg_ref[...]).astype(BF16)
    ckvn = _rms(ckv, kvg_ref[...]).astype(BF16)
    q = jnp.dot(qn, wuq_ref[...], preferred_element_type=F32)
    kn = jnp.dot(ckvn, wuk_ref[...], preferred_element_type=F32)
    scale = LOG2E * (MLA_NOPE + MLA_ROPE) ** -0.5
    for h in range(MLA_HEADS):
        sl = slice(HEAD_PAD * h, HEAD_PAD * (h + 1))
        qh = _rope_tile(q[:, sl], cos_a, sin_a_lo, sin_a_hi, MLA_ROPE // 2)
        q_ref[0, :, sl] = (qh * scale).astype(BF16)
        k_ref[0, :, sl] = (kn[:, sl] + kr).astype(BF16)
    vt = lax.dot_general(wuvt_ref[...], ckvn, (((1,), (1,)), ((), ())), preferred_element_type=F32)
    row = lax.broadcasted_iota(jnp.int32, vt.shape, 0)
    vt_ref[0, 0] = jnp.where(row % VT_ROWS >= MLA_V, 1.0, vt).astype(BF16)

    pd = jnp.dot(u, wd_ref[...], preferred_element_type=F32)
    dscale = DIL_HEAD_DIM ** -0.5
    for c in range(DIL_WIDTH // LANES):
        sl = slice(LANES * c, LANES * (c + 1))
        ks = slice(DIL_WIDTH + LANES * c, DIL_WIDTH + LANES * (c + 1))
        dq_ref[0, :, sl] = (_rope_tile(pd[:, sl], cos_b, sin_b_lo, sin_b_hi, ROPE_DIM // 2) * dscale).astype(BF16)
        dk_ref[0, :, sl] = _rope_tile(pd[:, ks], cos_b, sin_b_lo, sin_b_hi, ROPE_DIM // 2).astype(BF16)
    dv_ref[0] = pd[:, 2 * DIL_WIDTH:].astype(BF16)

    pg = jnp.dot(u, wg_ref[...], preferred_element_type=F32) + bg_ref[...]
    gate_ref[0] = jax.nn.sigmoid(pg).astype(BF16)


def _rope_tables():
    half_a = MLA_ROPE // 2
    inv_a = 1.0 / (jnp.float32(MLA_THETA) ** (jnp.arange(half_a, dtype=F32) / half_a))
    half_b = ROPE_DIM // 2
    inv_b = 1.0 / (jnp.float32(ROPE_THETA) ** (jnp.arange(half_b, dtype=F32) / half_b))
    z = lambda n: jnp.zeros((n,), F32)
    o = lambda n: jnp.ones((n,), F32)
    row_a = jnp.concatenate([z(MLA_NOPE), inv_a, inv_a, z(HEAD_PAD - MLA_NOPE - MLA_ROPE)])
    lo_a = jnp.concatenate([z(MLA_NOPE), -o(half_a), z(half_a), z(HEAD_PAD - MLA_NOPE - MLA_ROPE)])
    hi_a = jnp.concatenate([z(MLA_NOPE), z(half_a), o(half_a), z(HEAD_PAD - MLA_NOPE - MLA_ROPE)])
    rest = DIL_HEAD_DIM - ROPE_DIM
    head_b = jnp.concatenate([inv_b, inv_b, z(rest)])
    head_lo = jnp.concatenate([-o(half_b), z(half_b), z(rest)])
    head_hi = jnp.concatenate([z(half_b), o(half_b), z(rest)])
    rep = LANES // DIL_HEAD_DIM
    rows = [row_a, lo_a, hi_a, jnp.tile(head_b, rep), jnp.tile(head_lo, rep), jnp.tile(head_hi, rep), z(LANES), z(LANES)]
    return jnp.stack(rows)


def _mixer_in(h1, positions, mix_pre_g, w_in, b_gate, q_norm_g, w_uq, kv_norm_g, w_uk, w_uv):
    b, s, _ = h1.shape
    o0 = 0
    o1 = o0 + MLA_Q_RANK
    o2 = o1 + MLA_KV_RANK
    o3 = o2 + MLA_ROPE
    o4 = o3 + 3 * DIL_WIDTH
    pad_hi = HEAD_PAD - MLA_NOPE - MLA_ROPE
    w_kr = jnp.pad(w_in[:, o2:o3], ((0, 0), (MLA_NOPE, pad_hi)))
    wa = jnp.concatenate([w_in[:, o0:o2], w_kr], axis=1).astype(BF16)
    wd = w_in[:, o3:o4].astype(BF16)
    wg = w_in[:, o4:].astype(BF16)
    wuq = jnp.pad(w_uq.reshape(MLA_Q_RANK, MLA_HEADS, MLA_NOPE + MLA_ROPE),
                  ((0, 0), (0, 0), (0, pad_hi))).reshape(MLA_Q_RANK, MLA_HEADS * HEAD_PAD).astype(BF16)
    wuk = jnp.pad(w_uk.reshape(MLA_KV_RANK, MLA_HEADS, MLA_NOPE),
                  ((0, 0), (0, 0), (0, HEAD_PAD - MLA_NOPE))).reshape(MLA_KV_RANK, MLA_HEADS * HEAD_PAD).astype(BF16)
    wuvt = jnp.pad(w_uv.T.reshape(MLA_HEADS, MLA_V, MLA_KV_RANK), ((0, 0), (0, VT_ROWS - MLA_V), (0, 0))
                   ).reshape(MLA_HEADS * VT_ROWS, MLA_KV_RANK).astype(BF16)
    pos = positions.astype(F32).reshape(b, s, 1)
    tm = MIX_TM
    hp = MLA_HEADS * HEAD_PAD
    tok = lambda c: pl.BlockSpec((1, tm, c), lambda bi, i: (bi, i, 0))
    out_shapes = (
        jax.ShapeDtypeStruct((b, s, hp), BF16),
        jax.ShapeDtypeStruct((b, s, hp), BF16),
        jax.ShapeDtypeStruct((b, s // tm, MLA_HEADS * VT_ROWS, tm), BF16),
        jax.ShapeDtypeStruct((b, s, DIL_WIDTH), BF16),
        jax.ShapeDtypeStruct((b, s, DIL_WIDTH), BF16),
        jax.ShapeDtypeStruct((b, s, DIL_WIDTH), BF16),
        jax.ShapeDtypeStruct((b, s, N_BRANCH * D_MODEL), BF16),
    )
    out_specs = (
        tok(hp), tok(hp),
        pl.BlockSpec((1, 1, MLA_HEADS * VT_ROWS, tm), lambda bi, i: (bi, i, 0, 0)),
        tok(DIL_WIDTH), tok(DIL_WIDTH), tok(DIL_WIDTH), tok(N_BRANCH * D_MODEL),
    )
    return pl.pallas_call(
        _mixer_in_kernel,
        name="mixer_in",
        grid=(b, s // tm),
        in_specs=[
            tok(D_MODEL), tok(1),
            _const_spec((1, D_MODEL)),
            _const_spec(wa.shape), _const_spec(wd.shape), _const_spec(wg.shape),
            _const_spec((1, N_BRANCH * D_MODEL)),
            _const_spec((1, MLA_Q_RANK)), _const_spec((1, MLA_KV_RANK)),
            _const_spec(wuq.shape), _const_spec(wuk.shape), _const_spec(wuvt.shape),
            _const_spec((8, LANES)),
        ],
        out_specs=out_specs,
        out_shape=out_shapes,
        compiler_params=pltpu.CompilerParams(dimension_semantics=("arbitrary", "arbitrary"),
                                             vmem_limit_bytes=VMEM_LIMIT),
    )(h1, pos, mix_pre_g.reshape(1, D_MODEL), wa, wd, wg, b_gate.reshape(1, -1),
      q_norm_g.reshape(1, -1), kv_norm_g.reshape(1, -1), wuq, wuk, wuvt, _rope_tables())


def _flash_kernel(q_ref, k_ref, vt_ref, o_ref, s0_ref, s1_ref, s2_ref, s3_ref):
    nk = k_ref.shape[1] // FLASH_TK
    assert nk % 4 == 0
    bufs = (s0_ref, s1_ref, s2_ref, s3_ref)
    qs = [q_ref[0, :, HEAD_PAD * hh:HEAD_PAD * (hh + 1)] for hh in range(2)]

    def scores(c, hh, buf):
        cmax = None
        for part in range(FLASH_TK // FLASH_TS):
            r0 = c * FLASH_TK + part * FLASH_TS
            if not isinstance(r0, int):
                r0 = pl.multiple_of(r0, FLASH_TS)
            k = k_ref[0, pl.ds(r0, FLASH_TS), HEAD_PAD * hh:HEAD_PAD * (hh + 1)]
            st = lax.dot_general(k, qs[hh], (((1,), (1,)), ((), ())), preferred_element_type=F32)
            buf[hh, FLASH_TS * part:FLASH_TS * (part + 1)] = st
            pm = jnp.max(st, axis=0, keepdims=True)
            cmax = pm if cmax is None else jnp.maximum(cmax, pm)
        return cmax

    def consume(c, hh, buf, m, cmax, acc):
        m_new = jnp.maximum(m, cmax)
        alpha = jnp.exp2(m - m_new)
        p = jnp.exp2((buf[hh] - m_new).astype(BF16))
        vt = vt_ref[0, c, VT_ROWS * hh:VT_ROWS * (hh + 1), :]
        return m_new, alpha * acc + jnp.dot(vt, p, preferred_element_type=F32)

    def half(c, state, pair_in, pair_out, produce):
        new = []
        for hh in range(2):
            m, cm0, cm1, acc = state[hh]
            nxt = [scores(c + 2 + i, hh, bufs[pair_out + i]) for i in range(2)] if produce else [cm0, cm1]
            m, acc = consume(c, hh, bufs[pair_in], m, cm0, acc)
            m, acc = consume(c + 1, hh, bufs[pair_in + 1], m, cm1, acc)
            new.append((m, nxt[0], nxt[1], acc))
        return tuple(new)

    def step(j, state):
        c = 4 * j
        return half(c + 2, half(c, state, 0, 2, True), 2, 0, True)

    state = tuple((jnp.full((1, FLASH_TQ), -jnp.inf, F32), scores(0, hh, bufs[0]), scores(1, hh, bufs[1]),
                   jnp.zeros((VT_ROWS, FLASH_TQ), F32)) for hh in range(2))
    state = lax.fori_loop(0, nk // 4 - 1, step, state)
    state = half(nk - 2, half(nk - 4, state, 0, 2, True), 2, 0, False)
    outs = [acc[:MLA_V] / acc[MLA_V:MLA_V + 1] for (_, _, _, acc) in state]
    o_ref[0] = jnp.concatenate(outs, axis=0).T.astype(BF16)


def _mla_flash(q, k, vt):
    b, s, _ = q.shape
    nk = s // FLASH_TK
    return pl.pallas_call(
        _flash_kernel,
        name="mla_flash",
        grid=(b, MLA_HEADS // 2, s // FLASH_TQ),
        in_specs=[
            pl.BlockSpec((1, FLASH_TQ, 2 * HEAD_PAD), lambda bi, hp, qi: (bi, qi, hp)),
            pl.BlockSpec((1, s, 2 * HEAD_PAD), lambda bi, hp, qi: (bi, 0, hp)),
            pl.BlockSpec((1, nk, 2 * VT_ROWS, FLASH_TK), lambda bi, hp, qi: (bi, 0, hp, 0)),
        ],
        out_specs=pl.BlockSpec((1, FLASH_TQ, 2 * MLA_V), lambda bi, hp, qi: (bi, qi, hp)),
        out_shape=jax.ShapeDtypeStruct((b, s, MLA_WIDTH), BF16),
        scratch_shapes=[pltpu.VMEM((2, FLASH_TK, FLASH_TQ), F32)] * 4,
        compiler_params=pltpu.CompilerParams(dimension_semantics=("arbitrary", "arbitrary", "arbitrary"),
                                             vmem_limit_bytes=VMEM_LIMIT),
    )(q, k, vt)


def _dilated_kernel(q_ref, kc_ref, kp_ref, kn_ref, vc_ref, vp_ref, vn_ref, o_ref, lse_ref, kw_ref, vw_ref, *, n_rows):
    lb = q_ref.shape[1]
    l0 = pl.program_id(2) * lb
    kw_ref[0:DIL_HALF] = kp_ref[0]
    kw_ref[DIL_HALF:DIL_HALF + lb] = kc_ref[0]
    kw_ref[DIL_HALF + lb:] = kn_ref[0]
    vw_ref[0:DIL_HALF] = vp_ref[0]
    vw_ref[DIL_HALF:DIL_HALF + lb] = vc_ref[0]
    vw_ref[DIL_HALF + lb:] = vn_ref[0]
    win = DIL_SUB + 2 * DIL_HALF
    a = lax.broadcasted_iota(jnp.int32, (DIL_SUB, win), 0)
    c = lax.broadcasted_iota(jnp.int32, (DIL_SUB, win), 1)
    band = (c >= a) & (c <= a + 2 * DIL_HALF)

    def sub(j, carry):
        r0 = pl.multiple_of(j * DIL_SUB, DIL_SUB)
        kidx = l0 - DIL_HALF + r0 + c
        mask = band & (kidx >= 0) & (kidx < n_rows)
        for h in range(DIL_HEADS):
            hs = slice(DIL_HEAD_DIM * h, DIL_HEAD_DIM * (h + 1))
            qh = q_ref[0, pl.ds(r0, DIL_SUB), hs]
            kh = kw_ref[pl.ds(r0, win), hs]
            vh = vw_ref[pl.ds(r0, win), hs]
            s = lax.dot_general(qh, kh, (((1,), (1,)), ((), ())), preferred_element_type=F32)
            s = jnp.where(mask, s, NEG)
            m = jnp.max(s, axis=-1, keepdims=True)
            p = jnp.exp(s - m)
            den = jnp.sum(p, axis=-1, keepdims=True)
            num = jnp.dot(p.astype(BF16), vh, preferred_element_type=F32)
            o_ref[0, pl.ds(r0, DIL_SUB), hs] = num / den
            lse_ref[0, pl.ds(r0, DIL_SUB), hs] = jnp.broadcast_to(m + jnp.log(den), (DIL_SUB, DIL_HEAD_DIM))
        return carry

    lax.fori_loop(0, lb // DIL_SUB, sub, 0)


def _dilated(dq, dk, dv, dilation):
    b, s, w = dq.shape
    n_rows = s // dilation
    lb = min(DIL_LB, n_rows)
    hb = lb // DIL_HALF
    last = n_rows // DIL_HALF - 1
    view = lambda t: t.reshape(b, n_rows, dilation * w)
    cur = pl.BlockSpec((1, lb, w), lambda bi, r, l: (bi, l, r))
    prev = pl.BlockSpec((1, DIL_HALF, w), lambda bi, r, l: (bi, jnp.maximum(l * hb - 1, 0), r))
    nxt = pl.BlockSpec((1, DIL_HALF, w), lambda bi, r, l: (bi, jnp.minimum((l + 1) * hb, last), r))
    o, lse = pl.pallas_call(
        functools.partial(_dilated_kernel, n_rows=n_rows),
        name=f"dilated_{dilation}",
        grid=(b, dilation, n_rows // lb),
        in_specs=[cur, cur, prev, nxt, cur, prev, nxt],
        out_specs=(cur, cur),
        out_shape=(jax.ShapeDtypeStruct((b, n_rows, dilation * w), F32),) * 2,
        scratch_shapes=[pltpu.VMEM((lb + 2 * DIL_HALF, w), BF16)] * 2,
        compiler_params=pltpu.CompilerParams(dimension_semantics=("arbitrary",) * 3, vmem_limit_bytes=VMEM_LIMIT),
    )(view(dq), view(dk), view(dk), view(dk), view(dv), view(dv), view(dv))
    return o.reshape(b, s, w), lse.reshape(b, s, w)


def _mixer_out_kernel(h_ref, oa_ref, o1_ref, l1_ref, o2_ref, l2_ref, o3_ref, l3_ref, gate_ref,
                      wba_ref, wbb_ref, wo_ref, post_ref, out_ref):
    l1 = l1_ref[0]
    l2 = l2_ref[0]
    l3 = l3_ref[0]
    lm = jnp.maximum(jnp.maximum(l1, l2), l3)
    e1 = jnp.exp(l1 - lm)
    e2 = jnp.exp(l2 - lm)
    e3 = jnp.exp(l3 - lm)
    ob = (e1 * o1_ref[0] + e2 * o2_ref[0] + e3 * o3_ref[0]) / (e1 + e2 + e3)
    ya = jnp.dot(oa_ref[0], wba_ref[...], preferred_element_type=F32)
    yb = jnp.dot(ob.astype(BF16), wbb_ref[...], preferred_element_type=F32)
    gate = gate_ref[0].astype(F32)
    merged = gate[:, :D_MODEL] * ya + gate[:, D_MODEL:] * yb
    mix = jnp.dot(merged.astype(BF16), wo_ref[...], preferred_element_type=F32)
    out_ref[0] = h_ref[0] + _rms(mix, post_ref[...])


def _mixer_out(h1, oa, parts, gates, w_branch_a, w_branch_b, w_out, mix_post_g):
    b, s, _ = h1.shape
    tm = MIX_TM
    tok = lambda c: pl.BlockSpec((1, tm, c), lambda bi, i: (bi, i, 0))
    flat = [t for pair in parts for t in pair]
    return pl.pallas_call(
        _mixer_out_kernel,
        name="mixer_out",
        grid=(b, s // tm),
        in_specs=[tok(D_MODEL), tok(MLA_WIDTH)] + [tok(DIL_WIDTH)] * 6 + [tok(N_BRANCH * D_MODEL),
                  _const_spec((MLA_WIDTH, D_MODEL)), _const_spec((DIL_WIDTH, D_MODEL)),
                  _const_spec((D_MODEL, D_MODEL)), _const_spec((1, D_MODEL))],
        out_specs=tok(D_MODEL),
        out_shape=jax.ShapeDtypeStruct((b, s, D_MODEL), F32),
        compiler_params=pltpu.CompilerParams(dimension_semantics=("arbitrary", "arbitrary"),
                                             vmem_limit_bytes=VMEM_LIMIT),
    )(h1, oa, *flat, gates, w_branch_a.astype(BF16), w_branch_b.astype(BF16), w_out.astype(BF16),
      mix_post_g.reshape(1, D_MODEL))


def kernel(x, positions, ffn1_pre_g, ffn1_post_g, ffn1_w_gate, ffn1_w_up, ffn1_w_down, mix_pre_g, w_in, b_gate,
           q_norm_g, w_uq, kv_norm_g, w_uk, w_uv, w_branch_a, w_branch_b, w_out, mix_post_g,
           ffn2_pre_g, ffn2_post_g, ffn2_w_gate, ffn2_w_up, ffn2_w_down):
    b, s, d = x.shape
    depth = ffn1_pre_g.shape[0]
    h = x
    for l in range(depth):
        h = _ffn(h.reshape(b * s, d), ffn1_pre_g[l], ffn1_post_g[l], ffn1_w_gate[l], ffn1_w_up[l],
                 ffn1_w_down[l]).reshape(b, s, d)
        q, k, vt, dq, dk, dv, gates = _mixer_in(h, positions, mix_pre_g[l], w_in[l], b_gate[l], q_norm_g[l],
                                                w_uq[l], kv_norm_g[l], w_uk[l], w_uv[l])
        oa = _mla_flash(q, k, vt)
        parts = [_dilated(dq, dk, dv, dil) for (_, dil) in DIL_PATTERNS]
        h = _mixer_out(h, oa, parts, gates, w_branch_a[l], w_branch_b[l], w_out[l], mix_post_g[l])
        h = _ffn(h.reshape(b * s, d), ffn2_pre_g[l], ffn2_post_g[l], ffn2_w_gate[l], ffn2_w_up[l],
                 ffn2_w_down[l]).reshape(b, s, d)
    return h
```

```python
import functools

import jax
import jax.numpy as jnp
import numpy as np
from jax import lax
from jax.experimental import pallas as pl
from jax.experimental.pallas import tpu as pltpu

F32 = jnp.float32
BF16 = jnp.bfloat16

D_MODEL = 1024
D_FF = 2816
EPS = 1e-6
MLA_HEADS = 8
MLA_Q_RANK = 384
MLA_KV_RANK = 256
MLA_NOPE = 64
MLA_ROPE = 32
MLA_V = 64
MLA_THETA = 10000.0
MLA_WIDTH = MLA_HEADS * MLA_V
DIL_HEADS = 8
DIL_HEAD_DIM = 64
DIL_PATTERNS = ((128, 1), (512, 4), (2048, 16))
DIL_WIDTH = DIL_HEADS * DIL_HEAD_DIM
ROPE_THETA = 500000.0
ROPE_DIM = DIL_HEAD_DIM // 4
N_BRANCH = 2
NEG = -1e30

LANES = 128
HEAD_PAD = LANES
VMEM_LIMIT = 56 * 1024 * 1024

FFN_TM = 512
FFN_FC = 256
MIX_TM = 512
FLASH_TQ = 256
FLASH_TK = MIX_TM
FLASH_TS = 512
VT_ROWS = MLA_V + 16
LOG2E = 1.4426950408889634
DIL_LB = 512
DIL_SUB = 256
DIL_HALF = 64


def _rms(x, g):
    ms = jnp.mean(x * x, axis=-1, keepdims=True)
    return x * lax.rsqrt(ms + EPS) * g


def _const_spec(shape):
    nd = len(shape)
    return pl.BlockSpec(shape, lambda *_: (0,) * nd, pipeline_mode=pl.Buffered(1))


def _ffn_kernel(x_ref, pre_ref, post_ref, wgu_ref, wd_ref, o_ref, hm_ref):
    x = x_ref[...]
    xn = _rms(x, pre_ref[...]).astype(BF16)
    for c in range(D_FF // FFN_FC):
        gu = jnp.dot(xn, wgu_ref[:, 2 * FFN_FC * c:2 * FFN_FC * (c + 1)], preferred_element_type=F32)
        g = gu[:, :FFN_FC]
        u = gu[:, FFN_FC:]
        hm_ref[:, FFN_FC * c:FFN_FC * (c + 1)] = (g * jax.nn.sigmoid(g) * u).astype(BF16)
    f = jnp.dot(hm_ref[...], wd_ref[...], preferred_element_type=F32)
    o_ref[...] = x + 0.5 * _rms(f, post_ref[...])


def _ffn(h, pre_g, post_g, w_gate, w_up, w_down):
    t = h.shape[0]
    nc = D_FF // FFN_FC
    wgu = jnp.concatenate([w_gate.reshape(D_MODEL, nc, FFN_FC), w_up.reshape(D_MODEL, nc, FFN_FC)],
                          axis=-1).reshape(D_MODEL, 2 * D_FF).astype(BF16)
    wd = w_down.astype(BF16)
    return pl.pallas_call(
        _ffn_kernel,
        name="ffn",
        grid=(t // FFN_TM,),
        in_specs=[
            pl.BlockSpec((FFN_TM, D_MODEL), lambda i: (i, 0)),
            _const_spec((1, D_MODEL)),
            _const_spec((1, D_MODEL)),
            _const_spec((D_MODEL, 2 * D_FF)),
            _const_spec((D_FF, D_MODEL)),
        ],
        out_specs=pl.BlockSpec((FFN_TM, D_MODEL), lambda i: (i, 0)),
        out_shape=jax.ShapeDtypeStruct((t, D_MODEL), F32),
        scratch_shapes=[pltpu.VMEM((FFN_TM, D_FF), BF16)],
        compiler_params=pltpu.CompilerParams(dimension_semantics=("arbitrary",), vmem_limit_bytes=VMEM_LIMIT),
    )(h, pre_g.reshape(1, D_MODEL), post_g.reshape(1, D_MODEL), wgu, wd)


def _rope_tile(x, cos, sin_lo, sin_hi, half):
    return x * cos + pltpu.roll(x, LANES - half, 1) * sin_lo + pltpu.roll(x, half, 1) * sin_hi


def _mixer_in_kernel(h_ref, pos_ref, pre_ref, wa_ref, wd_ref, wg_ref, bg_ref, qg_ref, kvg_ref,
                     wuq_ref, wuk_ref, wuvt_ref, tab_ref,
                     q_ref, k_ref, vt_ref, gate_ref, *rest):
    dil_refs, dscr_ref = rest[:-1], rest[-1]
    u = _rms(h_ref[0], pre_ref[...]).astype(BF16)
    pos = pos_ref[0]
    tab = tab_ref[...]
    ang_a = pos * tab[0:1, :]
    cos_a = jnp.cos(ang_a)
    sin_a = jnp.sin(ang_a)
    sin_a_lo = sin_a * tab[1:2, :]
    sin_a_hi = sin_a * tab[2:3, :]
    ang_b = pos * tab[3:4, :]
    cos_b = jnp.cos(ang_b)
    sin_b = jnp.sin(ang_b)
    sin_b_lo = sin_b * tab[4:5, :]
    sin_b_hi = sin_b * tab[5:6, :]

    pa = jnp.dot(u, wa_ref[...], preferred_element_type=F32)
    cq = pa[:, :MLA_Q_RANK]
    ckv = pa[:, MLA_Q_RANK:MLA_Q_RANK + MLA_KV_RANK]
    kr = _rope_tile(pa[:, MLA_Q_RANK + MLA_KV_RANK:], cos_a, sin_a_lo, sin_a_hi, MLA_ROPE // 2)
    qn = _rms(cq, qg_ref[...]).astype(BF16)
    ckvn = _rms(ckv, kvg_ref[...]).astype(BF16)
    q = jnp.dot(qn, wuq_ref[...], preferred_element_type=F32)
    kn = jnp.dot(ckvn, wuk_ref[...], preferred_element_type=F32)
    scale = LOG2E * (MLA_NOPE + MLA_ROPE) ** -0.5
    for h in range(MLA_HEADS):
        sl = slice(HEAD_PAD * h, HEAD_PAD * (h + 1))
        qh = _rope_tile(q[:, sl], cos_a, sin_a_lo, sin_a_hi, MLA_ROPE // 2)
        q_ref[0, :, sl] = (qh * scale).astype(BF16)
        k_ref[0, :, sl] = (kn[:, sl] + kr).astype(BF16)
    vt = lax.dot_general(wuvt_ref[...], ckvn, (((1,), (1,)), ((), ())), preferred_element_type=F32)
    row = lax.broadcasted_iota(jnp.int32, vt.shape, 0)
    vt_ref[0, 0] = jnp.where(row % VT_ROWS >= MLA_V, 1.0, vt).astype(BF16)

    pd = jnp.dot(u, wd_ref[...], preferred_element_type=F32)
    dscale = DIL_HEAD_DIM ** -0.5
    nc = DIL_WIDTH // LANES
    for c in range(nc):
        sl = slice(LANES * c, LANES * (c + 1))
        ks = slice(DIL_WIDTH + LANES * c, DIL_WIDTH + LANES * (c + 1))
        dscr_ref[c] = _rope_tile(pd[:, sl], cos_b, sin_b_lo, sin_b_hi, ROPE_DIM // 2) * dscale
        dscr_ref[nc + c] = _rope_tile(pd[:, ks], cos_b, sin_b_lo, sin_b_hi, ROPE_DIM // 2)
        dscr_ref[2 * nc + c] = pd[:, 2 * DIL_WIDTH + LANES * c:2 * DIL_WIDTH + LANES * (c + 1)]
    tm = dscr_ref.shape[1]
    for pi, (_, dil) in enumerate(DIL_PATTERNS):
        for r in range(dil):
            for t in range(3):
                for c in range(nc):
                    rows = dscr_ref[nc * t + c, pl.ds(r, tm // dil, stride=dil), :] if dil > 1 else dscr_ref[nc * t + c]
                    dil_refs[3 * pi + t][0, r, :, LANES * c:LANES * (c + 1)] = rows.astype(BF16)

    pg = jnp.dot(u, wg_ref[...], preferred_element_type=F32) + bg_ref[...]
    gate_ref[0] = jax.nn.sigmoid(pg).astype(BF16)


def _rope_tables():
    half_a = MLA_ROPE // 2
    inv_a = 1.0 / (jnp.float32(MLA_THETA) ** (jnp.arange(half_a, dtype=F32) / half_a))
    half_b = ROPE_DIM // 2
    inv_b = 1.0 / (jnp.float32(ROPE_THETA) ** (jnp.arange(half_b, dtype=F32) / half_b))
    z = lambda n: jnp.zeros((n,), F32)
    o = lambda n: jnp.ones((n,), F32)
    row_a = jnp.concatenate([z(MLA_NOPE), inv_a, inv_a, z(HEAD_PAD - MLA_NOPE - MLA_ROPE)])
    lo_a = jnp.concatenate([z(MLA_NOPE), -o(half_a), z(half_a), z(HEAD_PAD - MLA_NOPE - MLA_ROPE)])
    hi_a = jnp.concatenate([z(MLA_NOPE), z(half_a), o(half_a), z(HEAD_PAD - MLA_NOPE - MLA_ROPE)])
    rest = DIL_HEAD_DIM - ROPE_DIM
    head_b = jnp.concatenate([inv_b, inv_b, z(rest)])
    head_lo = jnp.concatenate([-o(half_b), z(half_b), z(rest)])
    head_hi = jnp.concatenate([z(half_b), o(half_b), z(rest)])
    rep = LANES // DIL_HEAD_DIM
    rows = [row_a, lo_a, hi_a, jnp.tile(head_b, rep), jnp.tile(head_lo, rep), jnp.tile(head_hi, rep), z(LANES), z(LANES)]
    return jnp.stack(rows)


def _mixer_in(h1, positions, mix_pre_g, w_in, b_gate, q_norm_g, w_uq, kv_norm_g, w_uk, w_uv):
    b, s, _ = h1.shape
    o0 = 0
    o1 = o0 + MLA_Q_RANK
    o2 = o1 + MLA_KV_RANK
    o3 = o2 + MLA_ROPE
    o4 = o3 + 3 * DIL_WIDTH
    pad_hi = HEAD_PAD - MLA_NOPE - MLA_ROPE
    w_kr = jnp.pad(w_in[:, o2:o3], ((0, 0), (MLA_NOPE, pad_hi)))
    wa = jnp.concatenate([w_in[:, o0:o2], w_kr], axis=1).astype(BF16)
    wd = w_in[:, o3:o4].astype(BF16)
    wg = w_in[:, o4:].astype(BF16)
    wuq = jnp.pad(w_uq.reshape(MLA_Q_RANK, MLA_HEADS, MLA_NOPE + MLA_ROPE),
                  ((0, 0), (0, 0), (0, pad_hi))).reshape(MLA_Q_RANK, MLA_HEADS * HEAD_PAD).astype(BF16)
    wuk = jnp.pad(w_uk.reshape(MLA_KV_RANK, MLA_HEADS, MLA_NOPE),
                  ((0, 0), (0, 0), (0, HEAD_PAD - MLA_NOPE))).reshape(MLA_KV_RANK, MLA_HEADS * HEAD_PAD).astype(BF16)
    wuvt = jnp.pad(w_uv.T.reshape(MLA_HEADS, MLA_V, MLA_KV_RANK), ((0, 0), (0, VT_ROWS - MLA_V), (0, 0))
                   ).reshape(MLA_HEADS * VT_ROWS, MLA_KV_RANK).astype(BF16)
    pos = positions.astype(F32).reshape(b, s, 1)
    tm = MIX_TM
    hp = MLA_HEADS * HEAD_PAD
    tok = lambda c: pl.BlockSpec((1, tm, c), lambda bi, i: (bi, i, 0))
    out_shapes = (
        jax.ShapeDtypeStruct((b, s, hp), BF16),
        jax.ShapeDtypeStruct((b, s, hp), BF16),
        jax.ShapeDtypeStruct((b, s // tm, MLA_HEADS * VT_ROWS, tm), BF16),
        jax.ShapeDtypeStruct((b, s, N_BRANCH * D_MODEL), BF16),
    )
    out_specs = (
        tok(hp), tok(hp),
        pl.BlockSpec((1, 1, MLA_HEADS * VT_ROWS, tm), lambda bi, i: (bi, i, 0, 0)),
        tok(N_BRANCH * D_MODEL),
    )
    for _, dil in DIL_PATTERNS:
        out_shapes += (jax.ShapeDtypeStruct((b, dil, s // dil, DIL_WIDTH), BF16),) * 3
        out_specs += (pl.BlockSpec((1, dil, tm // dil, DIL_WIDTH), lambda bi, i: (bi, 0, i, 0)),) * 3
    outs = pl.pallas_call(
        _mixer_in_kernel,
        name="mixer_in",
        grid=(b, s // tm),
        in_specs=[
            tok(D_MODEL), tok(1),
            _const_spec((1, D_MODEL)),
            _const_spec(wa.shape), _const_spec(wd.shape), _const_spec(wg.shape),
            _const_spec((1, N_BRANCH * D_MODEL)),
            _const_spec((1, MLA_Q_RANK)), _const_spec((1, MLA_KV_RANK)),
            _const_spec(wuq.shape), _const_spec(wuk.shape), _const_spec(wuvt.shape),
            _const_spec((8, LANES)),
        ],
        out_specs=out_specs,
        out_shape=out_shapes,
        scratch_shapes=[pltpu.VMEM((3 * DIL_WIDTH // LANES, tm, LANES), F32)],
        compiler_params=pltpu.CompilerParams(dimension_semantics=("arbitrary", "arbitrary"),
                                             vmem_limit_bytes=VMEM_LIMIT),
    )(h1, pos, mix_pre_g.reshape(1, D_MODEL), wa, wd, wg, b_gate.reshape(1, -1),
      q_norm_g.reshape(1, -1), kv_norm_g.reshape(1, -1), wuq, wuk, wuvt, _rope_tables())
    q, k, vt, gates = outs[:4]
    dil_qkv = [outs[4 + 3 * i:7 + 3 * i] for i in range(len(DIL_PATTERNS))]
    return q, k, vt, gates, dil_qkv


def _flash_kernel(q_ref, k_ref, vt_ref, o_ref, s0_ref, s1_ref, s2_ref, s3_ref):
    nk = k_ref.shape[1] // FLASH_TK
    assert nk % 4 == 0
    bufs = (s0_ref, s1_ref, s2_ref, s3_ref)
    qs = [q_ref[0, :, HEAD_PAD * hh:HEAD_PAD * (hh + 1)] for hh in range(2)]

    def scores(c, hh, buf):
        cmax = None
        for part in range(FLASH_TK // FLASH_TS):
            r0 = c * FLASH_TK + part * FLASH_TS
            if not isinstance(r0, int):
                r0 = pl.multiple_of(r0, FLASH_TS)
            k = k_ref[0, pl.ds(r0, FLASH_TS), HEAD_PAD * hh:HEAD_PAD * (hh + 1)]
            st = lax.dot_general(k, qs[hh], (((1,), (1,)), ((), ())), preferred_element_type=F32)
            buf[hh, FLASH_TS * part:FLASH_TS * (part + 1)] = st
            pm = jnp.max(st, axis=0, keepdims=True)
            cmax = pm if cmax is None else jnp.maximum(cmax, pm)
        return cmax

    def consume(c, hh, buf, m, cmax, acc):
        m_new = jnp.maximum(m, cmax)
        alpha = jnp.exp2(m - m_new)
        p = jnp.exp2((buf[hh] - m_new).astype(BF16))
        vt = vt_ref[0, c, VT_ROWS * hh:VT_ROWS * (hh + 1), :]
        return m_new, alpha * acc + jnp.dot(vt, p, preferred_element_type=F32)

    def half(c, state, pair_in, pair_out, produce):
        new = []
        for hh in range(2):
            m, cm0, cm1, acc = state[hh]
            nxt = [scores(c + 2 + i, hh, bufs[pair_out + i]) for i in range(2)] if produce else [cm0, cm1]
            m, acc = consume(c, hh, bufs[pair_in], m, cm0, acc)
            m, acc = consume(c + 1, hh, bufs[pair_in + 1], m, cm1, acc)
            new.append((m, nxt[0], nxt[1], acc))
        return tuple(new)

    def step(j, state):
        c = 4 * j
        return half(c + 2, half(c, state, 0, 2, True), 2, 0, True)

    state = tuple((jnp.full((1, FLASH_TQ), -jnp.inf, F32), scores(0, hh, bufs[0]), scores(1, hh, bufs[1]),
                   jnp.zeros((VT_ROWS, FLASH_TQ), F32)) for hh in range(2))
    state = lax.fori_loop(0, nk // 4 - 1, step, state)
    state = half(nk - 2, half(nk - 4, state, 0, 2, True), 2, 0, False)
    outs = [acc[:MLA_V] / acc[MLA_V:MLA_V + 1] for (_, _, _, acc) in state]
    o_ref[0] = jnp.concatenate(outs, axis=0).T.astype(BF16)


def _mla_flash(q, k, vt):
    b, s, _ = q.shape
    nk = s // FLASH_TK
    return pl.pallas_call(
        _flash_kernel,
        name="mla_flash",
        grid=(b, MLA_HEADS // 2, s // FLASH_TQ),
        in_specs=[
            pl.BlockSpec((1, FLASH_TQ, 2 * HEAD_PAD), lambda bi, hp, qi: (bi, qi, hp)),
            pl.BlockSpec((1, s, 2 * HEAD_PAD), lambda bi, hp, qi: (bi, 0, hp)),
            pl.BlockSpec((1, nk, 2 * VT_ROWS, FLASH_TK), lambda bi, hp, qi: (bi, 0, hp, 0)),
        ],
        out_specs=pl.BlockSpec((1, FLASH_TQ, 2 * MLA_V), lambda bi, hp, qi: (bi, qi, hp)),
        out_shape=jax.ShapeDtypeStruct((b, s, MLA_WIDTH), BF16),
        scratch_shapes=[pltpu.VMEM((2, FLASH_TK, FLASH_TQ), F32)] * 4,
        compiler_params=pltpu.CompilerParams(dimension_semantics=("arbitrary", "arbitrary", "arbitrary"),
                                             vmem_limit_bytes=VMEM_LIMIT),
    )(q, k, vt)


def _dilated_kernel(q_ref, kc_ref, kp_ref, kn_ref, vc_ref, vp_ref, vn_ref, o_ref, lse_ref,
                    kw_ref, vt_ref, sa_ref, sb_ref, ot_ref, lt_ref, *, n_rows):
    lb = q_ref.shape[2]
    l0 = pl.program_id(2) * lb
    wrows = lb + 2 * DIL_HALF
    kw_ref[0:DIL_HALF] = kp_ref[0, 0]
    kw_ref[DIL_HALF:DIL_HALF + lb] = kc_ref[0, 0]
    kw_ref[DIL_HALF + lb:] = kn_ref[0, 0]
    vw = jnp.concatenate([vp_ref[0, 0], vc_ref[0, 0], vn_ref[0, 0]], axis=0).astype(F32)
    vt = vw.T.astype(BF16)
    ones = jnp.ones((VT_ROWS - DIL_HEAD_DIM, wrows), BF16)
    for h in range(DIL_HEADS):
        vt_ref[VT_ROWS * h:VT_ROWS * h + DIL_HEAD_DIM] = vt[DIL_HEAD_DIM * h:DIL_HEAD_DIM * (h + 1)]
        vt_ref[VT_ROWS * h + DIL_HEAD_DIM:VT_ROWS * (h + 1)] = ones

    win = DIL_SUB + 2 * DIL_HALF
    krow = lax.broadcasted_iota(jnp.int32, (win, DIL_SUB), 0)
    qcol = lax.broadcasted_iota(jnp.int32, (win, DIL_SUB), 1)
    in_band = (krow - qcol).astype(jnp.uint32) <= 2 * DIL_HALF
    lane = lax.broadcasted_iota(jnp.int32, (DIL_SUB, LANES), 1)
    sbufs = (sa_ref, sb_ref)

    def produce(u):
        j, h = divmod(u, DIL_HEADS)
        ls = slice(LANES * (h // 2), LANES * (h // 2 + 1))
        kpair = kw_ref[j * DIL_SUB:j * DIL_SUB + win, ls]
        qpair = q_ref[0, 0, j * DIL_SUB:(j + 1) * DIL_SUB, ls]
        mine = (lane >= DIL_HEAD_DIM) if h % 2 else (lane < DIL_HEAD_DIM)
        qh = jnp.where(mine, qpair, jnp.zeros_like(qpair))
        st = lax.dot_general(kpair, qh, (((1,), (1,)), ((), ())), preferred_element_type=F32)
        kidx = l0 - DIL_HALF + j * DIL_SUB + krow
        st = jnp.where(in_band & (kidx.astype(jnp.uint32) < n_rows), st, NEG)
        sbufs[u % 2][...] = st
        return jnp.max(st, axis=0, keepdims=True)

    def consume(u, m):
        j, h = divmod(u, DIL_HEADS)
        p = jnp.exp((sbufs[u % 2][...] - m).astype(BF16))
        r = jnp.dot(vt_ref[VT_ROWS * h:VT_ROWS * (h + 1), j * DIL_SUB:j * DIL_SUB + win], p,
                    preferred_element_type=F32)
        den = r[DIL_HEAD_DIM:DIL_HEAD_DIM + 1]
        hs = slice(DIL_HEAD_DIM * h, DIL_HEAD_DIM * (h + 1))
        ot_ref[hs] = r[:DIL_HEAD_DIM] / den
        lt_ref[hs] = jnp.broadcast_to(m + jnp.log(den), (DIL_HEAD_DIM, DIL_SUB))
        if h == DIL_HEADS - 1:
            o_ref[0, 0, j * DIL_SUB:(j + 1) * DIL_SUB, :] = ot_ref[...].T
            lse_ref[0, 0, j * DIL_SUB:(j + 1) * DIL_SUB, :] = lt_ref[...].T

    n_units = (lb // DIL_SUB) * DIL_HEADS
    m_cur = produce(0)
    for u in range(n_units):
        m_next = produce(u + 1) if u + 1 < n_units else None
        consume(u, m_cur)
        m_cur = m_next


def _dilated(dq, dk, dv, dilation):
    b, _, n_rows, w = dq.shape
    lb = min(DIL_LB, n_rows)
    hb = lb // DIL_HALF
    last = n_rows // DIL_HALF - 1
    cur = pl.BlockSpec((1, 1, lb, w), lambda bi, r, l: (bi, r, l, 0))
    prev = pl.BlockSpec((1, 1, DIL_HALF, w), lambda bi, r, l: (bi, r, jnp.maximum(l * hb - 1, 0), 0))
    nxt = pl.BlockSpec((1, 1, DIL_HALF, w), lambda bi, r, l: (bi, r, jnp.minimum((l + 1) * hb, last), 0))
    return pl.pallas_call(
        functools.partial(_dilated_kernel, n_rows=n_rows),
        name=f"dilated_{dilation}",
        grid=(b, dilation, n_rows // lb),
        in_specs=[cur, cur, prev, nxt, cur, prev, nxt],
        out_specs=(cur, cur),
        out_shape=(jax.ShapeDtypeStruct((b, dilation, n_rows, w), F32),) * 2,
        scratch_shapes=[pltpu.VMEM((lb + 2 * DIL_HALF, w), BF16),
                        pltpu.VMEM((DIL_HEADS * VT_ROWS, lb + 2 * DIL_HALF), BF16),
                        pltpu.VMEM((DIL_SUB + 2 * DIL_HALF, DIL_SUB), F32),
                        pltpu.VMEM((DIL_SUB + 2 * DIL_HALF, DIL_SUB), F32),
                        pltpu.VMEM((w, DIL_SUB), F32),
                        pltpu.VMEM((w, DIL_SUB), F32)],
        compiler_params=pltpu.CompilerParams(dimension_semantics=("arbitrary",) * 3, vmem_limit_bytes=VMEM_LIMIT),
    )(dq, dk, dk, dk, dv, dv, dv)


def _mixer_out_kernel(h_ref, oa_ref, o1_ref, l1_ref, o2_ref, l2_ref, o3_ref, l3_ref, gate_ref,
                      wba_ref, wbb_ref, wo_ref, post_ref, out_ref, *scr):
    tm = h_ref.shape[1]

    def token_major(ref, dil, buf):
        if dil == 1:
            return ref[0, 0]
        nc = DIL_WIDTH // LANES
        for r in range(dil):
            for c in range(nc):
                buf[c, pl.ds(r, tm // dil, stride=dil), :] = ref[0, r, :, LANES * c:LANES * (c + 1)]
        return jnp.concatenate([buf[c] for c in range(nc)], axis=1)

    bufs = iter(scr)
    os_, ls_ = [], []
    for (o_r, l_r), (_, dil) in zip(((o1_ref, l1_ref), (o2_ref, l2_ref), (o3_ref, l3_ref)), DIL_PATTERNS):
        os_.append(token_major(o_r, dil, None if dil == 1 else next(bufs)))
        ls_.append(token_major(l_r, dil, None if dil == 1 else next(bufs)))
    lm = jnp.maximum(jnp.maximum(ls_[0], ls_[1]), ls_[2])
    es = [jnp.exp(l - lm) for l in ls_]
    ob = (es[0] * os_[0] + es[1] * os_[1] + es[2] * os_[2]) / (es[0] + es[1] + es[2])
    ya =jnp.dot(oa_ref[0], wba_ref[...], preferred_element_type=F32)
    yb = jnp.dot(ob.astype(BF16), wbb_ref[...], preferred_element_type=F32)
    gate = gate_ref[0].astype(F32)
    merged = gate[:, :D_MODEL] * ya + gate[:, D_MODEL:] * yb
    mix = jnp.dot(merged.astype(BF16), wo_ref[...], preferred_element_type=F32)
    out_ref[0] = h_ref[0] + _rms(mix, post_ref[...])


def _mixer_out(h1, oa, parts, gates, w_branch_a, w_branch_b, w_out, mix_post_g):
    b, s, _ = h1.shape
    tm = MIX_TM
    tok = lambda c: pl.BlockSpec((1, tm, c), lambda bi, i: (bi, i, 0))
    flat, part_specs, n_scr = [], [], 0
    for (_, dil), pair in zip(DIL_PATTERNS, parts):
        flat += list(pair)
        part_specs += [pl.BlockSpec((1, dil, tm // dil, DIL_WIDTH), lambda bi, i: (bi, 0, i, 0))] * 2
        n_scr += 2 * (dil > 1)
    return pl.pallas_call(
        _mixer_out_kernel,
        name="mixer_out",
        grid=(b, s // tm),
        scratch_shapes=[pltpu.VMEM((DIL_WIDTH // LANES, tm, LANES), F32)] * n_scr,
        in_specs=[tok(D_MODEL), tok(MLA_WIDTH)] + part_specs + [tok(N_BRANCH * D_MODEL),
                  _const_spec((MLA_WIDTH, D_MODEL)), _const_spec((DIL_WIDTH, D_MODEL)),
                  _const_spec((D_MODEL, D_MODEL)), _const_spec((1, D_MODEL))],
        out_specs=tok(D_MODEL),
        out_shape=jax.ShapeDtypeStruct((b, s, D_MODEL), F32),
        compiler_params=pltpu.CompilerParams(dimension_semantics=("arbitrary", "arbitrary"),
                                             vmem_limit_bytes=VMEM_LIMIT),
    )(h1, oa, *flat, gates, w_branch_a.astype(BF16), w_branch_b.astype(BF16), w_out.astype(BF16),
      mix_post_g.reshape(1, D_MODEL))


def kernel(x, positions, ffn1_pre_g, ffn1_post_g, ffn1_w_gate, ffn1_w_up, ffn1_w_down, mix_pre_g, w_in, b_gate,
           q_norm_g, w_uq, kv_norm_g, w_uk, w_uv, w_branch_a, w_branch_b, w_out, mix_post_g,
           ffn2_pre_g, ffn2_post_g, ffn2_w_gate, ffn2_w_up, ffn2_w_down):
    b, s, d = x.shape
    depth = ffn1_pre_g.shape[0]
    assert all(win // (2 * dil) == DIL_HALF and s % (dil * DIL_SUB) == 0 for win, dil in DIL_PATTERNS)
    h = x
    for l in range(depth):
        h = _ffn(h.reshape(b * s, d), ffn1_pre_g[l], ffn1_post_g[l], ffn1_w_gate[l], ffn1_w_up[l],
                 ffn1_w_down[l]).reshape(b, s, d)
        q, k, vt, gates, dil_qkv = _mixer_in(h, positions, mix_pre_g[l], w_in[l], b_gate[l], q_norm_g[l],
                                             w_uq[l], kv_norm_g[l], w_uk[l], w_uv[l])
        oa = _mla_flash(q, k, vt)
        parts = [_dilated(*qkv, dil) for qkv, (_, dil) in zip(dil_qkv, DIL_PATTERNS)]
        h = _mixer_out(h, oa, parts, gates, w_branch_a[l], w_branch_b[l], w_out[l], mix_post_g[l])
        h = _ffn(h.reshape(b * s, d), ffn2_pre_g[l], ffn2_post_g[l], ffn2_w_gate[l], ffn2_w_up[l],
                 ffn2_w_down[l]).reshape(b, s, d)
    return h
```

```python
import functools

import jax
import jax.numpy as jnp
import numpy as np
from jax import lax
from jax.experimental import pallas as pl
from jax.experimental.pallas import tpu as pltpu

F32 = jnp.float32
BF16 = jnp.bfloat16

D_MODEL = 1024
D_FF = 2816
EPS = 1e-6
MLA_HEADS = 8
MLA_Q_RANK = 384
MLA_KV_RANK = 256
MLA_NOPE = 64
MLA_ROPE = 32
MLA_V = 64
MLA_THETA = 10000.0
MLA_WIDTH = MLA_HEADS * MLA_V
DIL_HEADS = 8
DIL_HEAD_DIM = 64
DIL_PATTERNS = ((128, 1), (512, 4), (2048, 16))
DIL_WIDTH = DIL_HEADS * DIL_HEAD_DIM
ROPE_THETA = 500000.0
ROPE_DIM = DIL_HEAD_DIM // 4
N_BRANCH = 2
NEG = -1e30

LANES = 128
HEAD_PAD = LANES
VMEM_LIMIT = 56 * 1024 * 1024

FFN_TM = 512
FFN_FC = 256
MIX_TM = 512
FLASH_TQ = 256
FLASH_TK = MIX_TM
FLASH_TS = 512
VT_ROWS = MLA_V + 16
LOG2E = 1.4426950408889634
DIL_LB = 512
DIL_SUB = 256
DIL_HALF = 64


def _rms(x, g):
    ms = jnp.mean(x * x, axis=-1, keepdims=True)
    return x * lax.rsqrt(ms + EPS) * g


def _const_spec(shape):
    nd = len(shape)
    return pl.BlockSpec(shape, lambda *_: (0,) * nd, pipeline_mode=pl.Buffered(1))


def _ffn_kernel(x_ref, pre_ref, post_ref, wgu_ref, wd_ref, o_ref, hm_ref):
    x = x_ref[...]
    xn = _rms(x, pre_ref[...]).astype(BF16)
    for c in range(D_FF // FFN_FC):
        gu = jnp.dot(xn, wgu_ref[:, 2 * FFN_FC * c:2 * FFN_FC * (c + 1)], preferred_element_type=F32)
        g = gu[:, :FFN_FC]
        u = gu[:, FFN_FC:]
        hm_ref[:, FFN_FC * c:FFN_FC * (c + 1)] = (g * jax.nn.sigmoid(g) * u).astype(BF16)
    f = jnp.dot(hm_ref[...], wd_ref[...], preferred_element_type=F32)
    o_ref[...] = x + 0.5 * _rms(f, post_ref[...])


def _ffn(h, pre_g, post_g, w_gate, w_up, w_down):
    t = h.shape[0]
    nc = D_FF // FFN_FC
    wgu = jnp.concatenate([w_gate.reshape(D_MODEL, nc, FFN_FC), w_up.reshape(D_MODEL, nc, FFN_FC)],
                          axis=-1).reshape(D_MODEL, 2 * D_FF).astype(BF16)
    wd = w_down.astype(BF16)
    return pl.pallas_call(
        _ffn_kernel,
        name="ffn",
        grid=(t // FFN_TM,),
        in_specs=[
            pl.BlockSpec((FFN_TM, D_MODEL), lambda i: (i, 0)),
            _const_spec((1, D_MODEL)),
            _const_spec((1, D_MODEL)),
            _const_spec((D_MODEL, 2 * D_FF)),
            _const_spec((D_FF, D_MODEL)),
        ],
        out_specs=pl.BlockSpec((FFN_TM, D_MODEL), lambda i: (i, 0)),
        out_shape=jax.ShapeDtypeStruct((t, D_MODEL), F32),
        scratch_shapes=[pltpu.VMEM((FFN_TM, D_FF), BF16)],
        compiler_params=pltpu.CompilerParams(dimension_semantics=("arbitrary",), vmem_limit_bytes=VMEM_LIMIT),
    )(h, pre_g.reshape(1, D_MODEL), post_g.reshape(1, D_MODEL), wgu, wd)


def _rope_tile(x, cos, sin_lo, sin_hi, half):
    return x * cos + pltpu.roll(x, LANES - half, 1) * sin_lo + pltpu.roll(x, half, 1) * sin_hi


def _mixer_in_kernel(h_ref, pos_ref, pre_ref, wa_ref, wd_ref, wg_ref, bg_ref, qg_ref, kvg_ref,
                     wuq_ref, wuk_ref, wuvt_ref, tab_ref,
                     q_ref, k_ref, vt_ref, gate_ref, *rest):
    dil_refs, dscr_ref = rest[:-1], rest[-1]
    u = _rms(h_ref[0], pre_ref[...]).astype(BF16)
    pos = pos_ref[0]
    tab = tab_ref[...]
    ang_a = pos * tab[0:1, :]
    cos_a = jnp.cos(ang_a)
    sin_a = jnp.sin(ang_a)
    sin_a_lo = sin_a * tab[1:2, :]
    sin_a_hi = sin_a * tab[2:3, :]
    ang_b = pos * tab[3:4, :]
    cos_b = jnp.cos(ang_b)
    sin_b = jnp.sin(ang_b)
    sin_b_lo = sin_b * tab[4:5, :]
    sin_b_hi = sin_b * tab[5:6, :]

    pa = jnp.dot(u, wa_ref[...], preferred_element_type=F32)
    cq = pa[:, :MLA_Q_RANK]
    ckv = pa[:, MLA_Q_RANK:MLA_Q_RANK + MLA_KV_RANK]
    kr = _rope_tile(pa[:, MLA_Q_RANK + MLA_KV_RANK:], cos_a, sin_a_lo, sin_a_hi, MLA_ROPE // 2)
    qn = _rms(cq, qg_ref[...]).astype(BF16)
    ckvn = _rms(ckv, kvg_ref[...]).astype(BF16)
    q = jnp.dot(qn, wuq_ref[...], preferred_element_type=F32)
    kn = jnp.dot(ckvn, wuk_ref[...], preferred_element_type=F32)
    scale = LOG2E * (MLA_NOPE + MLA_ROPE) ** -0.5
    for h in range(MLA_HEADS):
        sl = slice(HEAD_PAD * h, HEAD_PAD * (h + 1))
        qh = _rope_tile(q[:, sl], cos_a, sin_a_lo, sin_a_hi, MLA_ROPE // 2)
        q_ref[0, :, sl] = (qh * scale).astype(BF16)
        k_ref[0, :, sl] = (kn[:, sl] + kr).astype(BF16)
    vt = lax.dot_general(wuvt_ref[...], ckvn, (((1,), (1,)), ((), ())), preferred_element_type=F32)
    row = lax.broadcasted_iota(jnp.int32, vt.shape, 0)
    vt_ref[0, 0] = jnp.where(row % VT_ROWS >= MLA_V, 1.0, vt).astype(BF16)

    pd = jnp.dot(u, wd_ref[...], preferred_element_type=F32)
    dscale = DIL_HEAD_DIM ** -0.5
    nc = DIL_WIDTH // LANES
    for c in range(nc):
        sl = slice(LANES * c, LANES * (c + 1))
        ks = slice(DIL_WIDTH + LANES * c, DIL_WIDTH + LANES * (c + 1))
        dscr_ref[c] = _rope_tile(pd[:, sl], cos_b, sin_b_lo, sin_b_hi, ROPE_DIM // 2) * dscale
        dscr_ref[nc + c] = _rope_tile(pd[:, ks], cos_b, sin_b_lo, sin_b_hi, ROPE_DIM // 2)
        dscr_ref[2 * nc + c] = pd[:, 2 * DIL_WIDTH + LANES * c:2 * DIL_WIDTH + LANES * (c + 1)]
    tm = dscr_ref.shape[1]
    for pi, (_, dil) in enumerate(DIL_PATTERNS):
        for r in range(dil):
            for t in range(3):
                for c in range(nc):
                    rows = dscr_ref[nc * t + c, pl.ds(r, tm // dil, stride=dil), :] if dil > 1 else dscr_ref[nc * t + c]
                    dil_refs[3 * pi + t][0, r, :, LANES * c:LANES * (c + 1)] = rows.astype(BF16)

    pg = jnp.dot(u, wg_ref[...], preferred_element_type=F32) + bg_ref[...]
    gate_ref[0] = jax.nn.sigmoid(pg).astype(BF16)


def _rope_tables():
    half_a = MLA_ROPE // 2
    inv_a = 1.0 / (jnp.float32(MLA_THETA) ** (jnp.arange(half_a, dtype=F32) / half_a))
    half_b = ROPE_DIM // 2
    inv_b = 1.0 / (jnp.float32(ROPE_THETA) ** (jnp.arange(half_b, dtype=F32) / half_b))
    z = lambda n: jnp.zeros((n,), F32)
    o = lambda n: jnp.ones((n,), F32)
    row_a = jnp.concatenate([z(MLA_NOPE), inv_a, inv_a, z(HEAD_PAD - MLA_NOPE - MLA_ROPE)])
    lo_a = jnp.concatenate([z(MLA_NOPE), -o(half_a), z(half_a), z(HEAD_PAD - MLA_NOPE - MLA_ROPE)])
    hi_a = jnp.concatenate([z(MLA_NOPE), z(half_a), o(half_a), z(HEAD_PAD - MLA_NOPE - MLA_ROPE)])
    rest = DIL_HEAD_DIM - ROPE_DIM
    head_b = jnp.concatenate([inv_b, inv_b, z(rest)])
    head_lo = jnp.concatenate([-o(half_b), z(half_b), z(rest)])
    head_hi = jnp.concatenate([z(half_b), o(half_b), z(rest)])
    rep = LANES // DIL_HEAD_DIM
    rows = [row_a, lo_a, hi_a, jnp.tile(head_b, rep), jnp.tile(head_lo, rep), jnp.tile(head_hi, rep), z(LANES), z(LANES)]
    return jnp.stack(rows)


def _mixer_in(h1, positions, mix_pre_g, w_in, b_gate, q_norm_g, w_uq, kv_norm_g, w_uk, w_uv):
    b, s, _ = h1.shape
    o0 = 0
    o1 = o0 + MLA_Q_RANK
    o2 = o1 + MLA_KV_RANK
    o3 = o2 + MLA_ROPE
    o4 = o3 + 3 * DIL_WIDTH
    pad_hi = HEAD_PAD - MLA_NOPE - MLA_ROPE
    w_kr = jnp.pad(w_in[:, o2:o3], ((0, 0), (MLA_NOPE, pad_hi)))
    wa = jnp.concatenate([w_in[:, o0:o2], w_kr], axis=1).astype(BF16)
    wd = w_in[:, o3:o4].astype(BF16)
    wg = w_in[:, o4:].astype(BF16)
    wuq = jnp.pad(w_uq.reshape(MLA_Q_RANK, MLA_HEADS, MLA_NOPE + MLA_ROPE),
                  ((0, 0), (0, 0), (0, pad_hi))).reshape(MLA_Q_RANK, MLA_HEADS * HEAD_PAD).astype(BF16)
    wuk = jnp.pad(w_uk.reshape(MLA_KV_RANK, MLA_HEADS, MLA_NOPE),
                  ((0, 0), (0, 0), (0, HEAD_PAD - MLA_NOPE))).reshape(MLA_KV_RANK, MLA_HEADS * HEAD_PAD).astype(BF16)
    wuvt = jnp.pad(w_uv.T.reshape(MLA_HEADS, MLA_V, MLA_KV_RANK), ((0, 0), (0, VT_ROWS - MLA_V), (0, 0))
                   ).reshape(MLA_HEADS * VT_ROWS, MLA_KV_RANK).astype(BF16)
    pos = positions.astype(F32).reshape(b, s, 1)
    tm = MIX_TM
    hp = MLA_HEADS * HEAD_PAD
    tok = lambda c: pl.BlockSpec((1, tm, c), lambda bi, i: (bi, i, 0))
    out_shapes = (
        jax.ShapeDtypeStruct((b, s, hp), BF16),
        jax.ShapeDtypeStruct((b, s, hp), BF16),
        jax.ShapeDtypeStruct((b, s // tm, MLA_HEADS * VT_ROWS, tm), BF16),
        jax.ShapeDtypeStruct((b, s, N_BRANCH * D_MODEL), BF16),
    )
    out_specs = (
        tok(hp), tok(hp),
        pl.BlockSpec((1, 1, MLA_HEADS * VT_ROWS, tm), lambda bi, i: (bi, i, 0, 0)),
        tok(N_BRANCH * D_MODEL),
    )
    for _, dil in DIL_PATTERNS:
        out_shapes += (jax.ShapeDtypeStruct((b, dil, s // dil, DIL_WIDTH), BF16),) * 3
        out_specs += (pl.BlockSpec((1, dil, tm // dil, DIL_WIDTH), lambda bi, i: (bi, 0, i, 0)),) * 3
    outs = pl.pallas_call(
        _mixer_in_kernel,
        name="mixer_in",
        grid=(b, s // tm),
        in_specs=[
            tok(D_MODEL), tok(1),
            _const_spec((1, D_MODEL)),
            _const_spec(wa.shape), _const_spec(wd.shape), _const_spec(wg.shape),
            _const_spec((1, N_BRANCH * D_MODEL)),
            _const_spec((1, MLA_Q_RANK)), _const_spec((1, MLA_KV_RANK)),
            _const_spec(wuq.shape), _const_spec(wuk.shape), _const_spec(wuvt.shape),
            _const_spec((8, LANES)),
        ],
        out_specs=out_specs,
        out_shape=out_shapes,
        scratch_shapes=[pltpu.VMEM((3 * DIL_WIDTH // LANES, tm, LANES), F32)],
        compiler_params=pltpu.CompilerParams(dimension_semantics=("arbitrary", "arbitrary"),
                                             vmem_limit_bytes=VMEM_LIMIT),
    )(h1, pos, mix_pre_g.reshape(1, D_MODEL), wa, wd, wg, b_gate.reshape(1, -1),
      q_norm_g.reshape(1, -1), kv_norm_g.reshape(1, -1), wuq, wuk, wuvt, _rope_tables())
    q, k, vt, gates = outs[:4]
    dil_qkv = [outs[4 + 3 * i:7 + 3 * i] for i in range(len(DIL_PATTERNS))]
    return q, k, vt, gates, dil_qkv


def _flash_kernel(q_ref, k_ref, vt_ref, o_ref, s0_ref, s1_ref, s2_ref, s3_ref):
    nk = k_ref.shape[1] // FLASH_TK
    nq = q_ref.shape[1] // FLASH_TQ
    assert nk % 4 == 0
    bufs = (s0_ref, s1_ref, s2_ref, s3_ref)

    def scores(t, c, hh, buf):
        q0 = t * FLASH_TQ
        k0 = c * FLASH_TK
        if not isinstance(q0, int):
            q0 = pl.multiple_of(q0, FLASH_TQ)
        if not isinstance(k0, int):
            k0 = pl.multiple_of(k0, FLASH_TK)
        q = q_ref[0, pl.ds(q0, FLASH_TQ), HEAD_PAD * hh:HEAD_PAD * (hh + 1)]
        k = k_ref[0, pl.ds(k0, FLASH_TK), HEAD_PAD * hh:HEAD_PAD * (hh + 1)]
        st = lax.dot_general(k, q, (((1,), (1,)), ((), ())), preferred_element_type=F32)
        buf[hh] = st
        return jnp.max(st, axis=0, keepdims=True)

    def consume(c, hh, buf, m, cmax, acc):
        m_new = jnp.maximum(m, cmax)
        alpha = jnp.exp2(m - m_new)
        p = jnp.exp2((buf[hh] - m_new).astype(BF16))
        vt = vt_ref[0, c, VT_ROWS * hh:VT_ROWS * (hh + 1), :]
        return m_new, alpha * acc + jnp.dot(vt, p, preferred_element_type=F32)

    def half(c, state, pair_in, pair_out, t_next, c_next):
        new = []
        for hh in range(2):
            m, cm0, cm1, acc = state[hh]
            nxt = [scores(t_next, c_next + i, hh, bufs[pair_out + i]) for i in range(2)]
            m, acc = consume(c, hh, bufs[pair_in], m, cm0, acc)
            m, acc = consume(c + 1, hh, bufs[pair_in + 1], m, cm1, acc)
            new.append((m, nxt[0], nxt[1], acc))
        return tuple(new)

    def tile(t, cmaxes):
        state = tuple((jnp.full((1, FLASH_TQ), -jnp.inf, F32), cmaxes[hh][0], cmaxes[hh][1],
                       jnp.zeros((VT_ROWS, FLASH_TQ), F32)) for hh in range(2))

        def group(j, st):
            c = 4 * j
            return half(c + 2, half(c, st, 0, 2, t, c + 2), 2, 0, t, c + 4)

        state = lax.fori_loop(0, nk // 4 - 1, group, state)
        t_next = jnp.minimum(t + 1, nq - 1)
        state = half(nk - 2, half(nk - 4, state, 0, 2, t, nk - 2), 2, 0, t_next, 0)
        outs = [acc[:MLA_V] / acc[MLA_V:MLA_V + 1] for (_, _, _, acc) in state]
        o_ref[0, pl.ds(pl.multiple_of(t * FLASH_TQ, FLASH_TQ), FLASH_TQ), :] = (
            jnp.concatenate(outs, axis=0).T.astype(BF16))
        return tuple((cm0, cm1) for (_, cm0, cm1, _) in state)

    first = tuple((scores(0, 0, hh, bufs[0]), scores(0, 1, hh, bufs[1])) for hh in range(2))
    lax.fori_loop(0, nq, tile, first)


def _mla_flash(q, k, vt):
    b, s, _ = q.shape
    nk = s // FLASH_TK
    return pl.pallas_call(
        _flash_kernel,
        name="mla_flash",
        grid=(b, MLA_HEADS // 2),
        in_specs=[
            pl.BlockSpec((1, s, 2 * HEAD_PAD), lambda bi, hp: (bi, 0, hp)),
            pl.BlockSpec((1, s, 2 * HEAD_PAD), lambda bi, hp: (bi, 0, hp)),
            pl.BlockSpec((1, nk, 2 * VT_ROWS, FLASH_TK), lambda bi, hp: (bi, 0, hp, 0)),
        ],
        out_specs=pl.BlockSpec((1, s, 2 * MLA_V), lambda bi, hp: (bi, 0, hp)),
        out_shape=jax.ShapeDtypeStruct((b, s, MLA_WIDTH), BF16),
        scratch_shapes=[pltpu.VMEM((2, FLASH_TK, FLASH_TQ), F32)] * 4,
        compiler_params=pltpu.CompilerParams(dimension_semantics=("arbitrary", "arbitrary"),
                                             vmem_limit_bytes=VMEM_LIMIT),
    )(q, k, vt)


def _dilated_kernel(q_ref, kc_ref, kp_ref, kn_ref, vc_ref, vp_ref, vn_ref, o_ref, lse_ref,
                    kw_ref, vt_ref, sa_ref, sb_ref, ot_ref, lt_ref, *, n_rows):
    lb = q_ref.shape[2]
    l0 = pl.program_id(2) * lb
    wrows = lb + 2 * DIL_HALF
    kw_ref[0:DIL_HALF] = kp_ref[0, 0]
    kw_ref[DIL_HALF:DIL_HALF + lb] = kc_ref[0, 0]
    kw_ref[DIL_HALF + lb:] = kn_ref[0, 0]
    vw = jnp.concatenate([vp_ref[0, 0], vc_ref[0, 0], vn_ref[0, 0]], axis=0).astype(F32)
    vt = vw.T.astype(BF16)
    ones = jnp.ones((VT_ROWS - DIL_HEAD_DIM, wrows), BF16)
    for h in range(DIL_HEADS):
        vt_ref[VT_ROWS * h:VT_ROWS * h + DIL_HEAD_DIM] = vt[DIL_HEAD_DIM * h:DIL_HEAD_DIM * (h + 1)]
        vt_ref[VT_ROWS * h + DIL_HEAD_DIM:VT_ROWS * (h + 1)] = ones

    win = DIL_SUB + 2 * DIL_HALF
    krow = lax.broadcasted_iota(jnp.int32, (win, DIL_SUB), 0)
    qcol = lax.broadcasted_iota(jnp.int32, (win, DIL_SUB), 1)
    in_band = (krow - qcol).astype(jnp.uint32) <= 2 * DIL_HALF
    lane = lax.broadcasted_iota(jnp.int32, (DIL_SUB, LANES), 1)
    sbufs = (sa_ref, sb_ref)

    def produce(u):
        j, h = divmod(u, DIL_HEADS)
        ls = slice(LANES * (h // 2), LANES * (h // 2 + 1))
        kpair = kw_ref[j * DIL_SUB:j * DIL_SUB + win, ls]
        qpair = q_ref[0, 0, j * DIL_SUB:(j + 1) * DIL_SUB, ls]
        mine = (lane >= DIL_HEAD_DIM) if h % 2 else (lane < DIL_HEAD_DIM)
        qh = jnp.where(mine, qpair, jnp.zeros_like(qpair))
        st = lax.dot_general(kpair, qh, (((1,), (1,)), ((), ())), preferred_element_type=F32)
        kidx = l0 - DIL_HALF + j * DIL_SUB + krow
        st = jnp.where(in_band & (kidx.astype(jnp.uint32) < n_rows), st, NEG)
        sbufs[u % 2][...] = st
        return jnp.max(st, axis=0, keepdims=True)

    def consume(u, m):
        j, h = divmod(u, DIL_HEADS)
        p = jnp.exp((sbufs[u % 2][...] - m).astype(BF16))
        r = jnp.dot(vt_ref[VT_ROWS * h:VT_ROWS * (h + 1), j * DIL_SUB:j * DIL_SUB + win], p,
                    preferred_element_type=F32)
        den = r[DIL_HEAD_DIM:DIL_HEAD_DIM + 1]
        hs = slice(DIL_HEAD_DIM * h, DIL_HEAD_DIM * (h + 1))
        ot_ref[hs] = r[:DIL_HEAD_DIM] / den
        lt_ref[hs] = jnp.broadcast_to(m + jnp.log(den), (DIL_HEAD_DIM, DIL_SUB))
        if h == DIL_HEADS - 1:
            o_ref[0, 0, j * DIL_SUB:(j + 1) * DIL_SUB, :] = ot_ref[...].T
            lse_ref[0, 0, j * DIL_SUB:(j + 1) * DIL_SUB, :] = lt_ref[...].T

    n_units = (lb // DIL_SUB) * DIL_HEADS
    m_cur = produce(0)
    for u in range(n_units):
        m_next = produce(u + 1) if u + 1 < n_units else None
        consume(u, m_cur)
        m_cur = m_next


def _dilated(dq, dk, dv, dilation):
    b, _, n_rows, w = dq.shape
    lb = min(DIL_LB, n_rows)
    hb = lb // DIL_HALF
    last = n_rows // DIL_HALF - 1
    cur = pl.BlockSpec((1, 1, lb, w), lambda bi, r, l: (bi, r, l, 0))
    prev = pl.BlockSpec((1, 1, DIL_HALF, w), lambda bi, r, l: (bi, r, jnp.maximum(l * hb - 1, 0), 0))
    nxt = pl.BlockSpec((1, 1, DIL_HALF, w), lambda bi, r, l: (bi, r, jnp.minimum((l + 1) * hb, last), 0))
    return pl.pallas_call(
        functools.partial(_dilated_kernel, n_rows=n_rows),
        name=f"dilated_{dilation}",
        grid=(b, dilation, n_rows // lb),
        in_specs=[cur, cur, prev, nxt, cur, prev, nxt],
        out_specs=(cur, cur),
        out_shape=(jax.ShapeDtypeStruct((b, dilation, n_rows, w), F32),) * 2,
        scratch_shapes=[pltpu.VMEM((lb + 2 * DIL_HALF, w), BF16),
                        pltpu.VMEM((DIL_HEADS * VT_ROWS, lb + 2 * DIL_HALF), BF16),
                        pltpu.VMEM((DIL_SUB + 2 * DIL_HALF, DIL_SUB), F32),
                        pltpu.VMEM((DIL_SUB + 2 * DIL_HALF, DIL_SUB), F32),
                        pltpu.VMEM((w, DIL_SUB), F32),
                        pltpu.VMEM((w, DIL_SUB), F32)],
        compiler_params=pltpu.CompilerParams(dimension_semantics=("arbitrary",) * 3, vmem_limit_bytes=VMEM_LIMIT),
    )(dq, dk, dk, dk, dv, dv, dv)


def _mixer_out_kernel(h_ref, oa_ref, o1_ref, l1_ref, o2_ref, l2_ref, o3_ref, l3_ref, gate_ref,
                      wba_ref, wbb_ref, wo_ref, post_ref, out_ref, *scr):
    tm = h_ref.shape[1]

    def token_major(ref, dil, buf):
        if dil == 1:
            return ref[0, 0]
        nc = DIL_WIDTH // LANES
        for r in range(dil):
            for c in range(nc):
                buf[c, pl.ds(r, tm // dil, stride=dil), :] = ref[0, r, :, LANES * c:LANES * (c + 1)]
        return jnp.concatenate([buf[c] for c in range(nc)], axis=1)

    bufs = iter(scr)
    os_, ls_ = [], []
    for (o_r, l_r), (_, dil) in zip(((o1_ref, l1_ref), (o2_ref, l2_ref), (o3_ref, l3_ref)), DIL_PATTERNS):
        os_.append(token_major(o_r, dil, None if dil == 1 else next(bufs)))
        ls_.append(token_major(l_r, dil, None if dil == 1 else next(bufs)))
    lm = jnp.maximum(jnp.maximum(ls_[0], ls_[1]), ls_[2])
    es = [jnp.exp(l - lm) for l in ls_]
    ob = (es[0] * os_[0] + es[1] * os_[1] + es[2] * os_[2]) / (es[0] + es[1] + es[2])
    ya =jnp.dot(oa_ref[0], wba_ref[...], preferred_element_type=F32)
    yb = jnp.dot(ob.astype(BF16), wbb_ref[...], preferred_element_type=F32)
    gate = gate_ref[0].astype(F32)
    merged = gate[:, :D_MODEL] * ya + gate[:, D_MODEL:] * yb
    mix = jnp.dot(merged.astype(BF16), wo_ref[...], preferred_element_type=F32)
    out_ref[0] = h_ref[0] + _rms(mix, post_ref[...])


def _mixer_out(h1, oa, parts, gates, w_branch_a, w_branch_b, w_out, mix_post_g):
    b, s, _ = h1.shape
    tm = MIX_TM
    tok = lambda c: pl.BlockSpec((1, tm, c), lambda bi, i: (bi, i, 0))
    flat, part_specs, n_scr = [], [], 0
    for (_, dil), pair in zip(DIL_PATTERNS, parts):
        flat += list(pair)
        part_specs += [pl.BlockSpec((1, dil, tm // dil, DIL_WIDTH), lambda bi, i: (bi, 0, i, 0))] * 2
        n_scr += 2 * (dil > 1)
    return pl.pallas_call(
        _mixer_out_kernel,
        name="mixer_out",
        grid=(b, s // tm),
        scratch_shapes=[pltpu.VMEM((DIL_WIDTH // LANES, tm, LANES), F32)] * n_scr,
        in_specs=[tok(D_MODEL), tok(MLA_WIDTH)] + part_specs + [tok(N_BRANCH * D_MODEL),
                  _const_spec((MLA_WIDTH, D_MODEL)), _const_spec((DIL_WIDTH, D_MODEL)),
                  _const_spec((D_MODEL, D_MODEL)), _const_spec((1, D_MODEL))],
        out_specs=tok(D_MODEL),
        out_shape=jax.ShapeDtypeStruct((b, s, D_MODEL), F32),
        compiler_params=pltpu.CompilerParams(dimension_semantics=("arbitrary", "arbitrary"),
                                             vmem_limit_bytes=VMEM_LIMIT),
    )(h1, oa, *flat, gates, w_branch_a.astype(BF16), w_branch_b.astype(BF16), w_out.astype(BF16),
      mix_post_g.reshape(1, D_MODEL))


def kernel(x, positions, ffn1_pre_g, ffn1_post_g, ffn1_w_gate, ffn1_w_up, ffn1_w_down, mix_pre_g, w_in, b_gate,
           q_norm_g, w_uq, kv_norm_g, w_uk, w_uv, w_branch_a, w_branch_b, w_out, mix_post_g,
           ffn2_pre_g, ffn2_post_g, ffn2_w_gate, ffn2_w_up, ffn2_w_down):
    b, s, d = x.shape
    depth = ffn1_pre_g.shape[0]
    assert all(win // (2 * dil) == DIL_HALF and s % (dil * DIL_SUB) == 0 for win, dil in DIL_PATTERNS)
    h = x
    for l in range(depth):
        h = _ffn(h.reshape(b * s, d), ffn1_pre_g[l], ffn1_post_g[l], ffn1_w_gate[l], ffn1_w_up[l],
                 ffn1_w_down[l]).reshape(b, s, d)
        q, k, vt, gates, dil_qkv = _mixer_in(h, positions, mix_pre_g[l], w_in[l], b_gate[l], q_norm_g[l],
                                             w_uq[l], kv_norm_g[l], w_uk[l], w_uv[l])
        oa = _mla_flash(q, k, vt)
        parts = [_dilated(*qkv, dil) for qkv, (_, dil) in zip(dil_qkv, DIL_PATTERNS)]
        h = _mixer_out(h, oa, parts, gates, w_branch_a[l], w_branch_b[l], w_out[l], mix_post_g[l])
        h = _ffn(h.reshape(b * s, d), ffn2_pre_g[l], ffn2_post_g[l], ffn2_w_gate[l], ffn2_w_up[l],
                 ffn2_w_down[l]).reshape(b, s, d)
    return h
```

```python
import functools

import jax
import jax.numpy as jnp
import numpy as np
from jax import lax
from jax.experimental import pallas as pl
from jax.experimental.pallas import tpu as pltpu

F32 = jnp.float32
BF16 = jnp.bfloat16

D_MODEL = 1024
D_FF = 2816
EPS = 1e-6
MLA_HEADS = 8
MLA_Q_RANK = 384
MLA_KV_RANK = 256
MLA_NOPE = 64
MLA_ROPE = 32
MLA_V = 64
MLA_THETA = 10000.0
MLA_WIDTH = MLA_HEADS * MLA_V
DIL_HEADS = 8
DIL_HEAD_DIM = 64
DIL_PATTERNS = ((128, 1), (512, 4), (2048, 16))
DIL_WIDTH = DIL_HEADS * DIL_HEAD_DIM
ROPE_THETA = 500000.0
ROPE_DIM = DIL_HEAD_DIM // 4
N_BRANCH = 2
NEG = -1e30

LANES = 128
HEAD_PAD = LANES
VMEM_LIMIT = 56 * 1024 * 1024

FFN_TM = 512
FFN_FC = 256
MIX_TM = 512
FLASH_TQ = 256
FLASH_TK = MIX_TM
FLASH_TS = 512
VT_ROWS = MLA_V + 16
LOG2E = 1.4426950408889634
DIL_LB = 512
DIL_SUB = 256
DIL_GROUP = 4
DIL_HALF = 64


def _rms(x, g):
    ms = jnp.mean(x * x, axis=-1, keepdims=True)
    return x * lax.rsqrt(ms + EPS) * g


def _const_spec(shape):
    nd = len(shape)
    return pl.BlockSpec(shape, lambda *_: (0,) * nd, pipeline_mode=pl.Buffered(1))


def _ffn_kernel(x_ref, pre_ref, post_ref, wgu_ref, wd_ref, o_ref, hm_ref):
    x = x_ref[...]
    xn = _rms(x, pre_ref[...]).astype(BF16)
    for c in range(D_FF // FFN_FC):
        gu = jnp.dot(xn, wgu_ref[:, 2 * FFN_FC * c:2 * FFN_FC * (c + 1)], preferred_element_type=F32)
        g = gu[:, :FFN_FC]
        u = gu[:, FFN_FC:]
        hm_ref[:, FFN_FC * c:FFN_FC * (c + 1)] = (g * jax.nn.sigmoid(g) * u).astype(BF16)
    f = jnp.dot(hm_ref[...], wd_ref[...], preferred_element_type=F32)
    o_ref[...] = x + 0.5 * _rms(f, post_ref[...])


def _ffn(h, pre_g, post_g, w_gate, w_up, w_down):
    t = h.shape[0]
    nc = D_FF // FFN_FC
    wgu = jnp.concatenate([w_gate.reshape(D_MODEL, nc, FFN_FC), w_up.reshape(D_MODEL, nc, FFN_FC)],
                          axis=-1).reshape(D_MODEL, 2 * D_FF).astype(BF16)
    wd = w_down.astype(BF16)
    return pl.pallas_call(
        _ffn_kernel,
        name="ffn",
        grid=(t // FFN_TM,),
        in_specs=[
            pl.BlockSpec((FFN_TM, D_MODEL), lambda i: (i, 0)),
            _const_spec((1, D_MODEL)),
            _const_spec((1, D_MODEL)),
            _const_spec((D_MODEL, 2 * D_FF)),
            _const_spec((D_FF, D_MODEL)),
        ],
        out_specs=pl.BlockSpec((FFN_TM, D_MODEL), lambda i: (i, 0)),
        out_shape=jax.ShapeDtypeStruct((t, D_MODEL), F32),
        scratch_shapes=[pltpu.VMEM((FFN_TM, D_FF), BF16)],
        compiler_params=pltpu.CompilerParams(dimension_semantics=("arbitrary",), vmem_limit_bytes=VMEM_LIMIT),
    )(h, pre_g.reshape(1, D_MODEL), post_g.reshape(1, D_MODEL), wgu, wd)


def _rope_tile(x, cos, sin_lo, sin_hi, half):
    return x * cos + pltpu.roll(x, LANES - half, 1) * sin_lo + pltpu.roll(x, half, 1) * sin_hi


def _mixer_in_kernel(h_ref, pos_ref, pre_ref, wa_ref, wd_ref, wg_ref, bg_ref, qg_ref, kvg_ref,
                     wuq_ref, wuk_ref, wuvt_ref, tab_ref,
                     q_ref, k_ref, vt_ref, gate_ref, *rest):
    dil_refs, dscr_ref = rest[:-1], rest[-1]
    u = _rms(h_ref[0], pre_ref[...]).astype(BF16)
    pos = pos_ref[0]
    tab = tab_ref[...]
    ang_a = pos * tab[0:1, :]
    cos_a = jnp.cos(ang_a)
    sin_a = jnp.sin(ang_a)
    sin_a_lo = sin_a * tab[1:2, :]
    sin_a_hi = sin_a * tab[2:3, :]
    ang_b = pos * tab[3:4, :]
    cos_b = jnp.cos(ang_b)
    sin_b = jnp.sin(ang_b)
    sin_b_lo = sin_b * tab[4:5, :]
    sin_b_hi = sin_b * tab[5:6, :]

    pa = jnp.dot(u, wa_ref[...], preferred_element_type=F32)
    cq = pa[:, :MLA_Q_RANK]
    ckv = pa[:, MLA_Q_RANK:MLA_Q_RANK + MLA_KV_RANK]
    kr = _rope_tile(pa[:, MLA_Q_RANK + MLA_KV_RANK:], cos_a, sin_a_lo, sin_a_hi, MLA_ROPE // 2)
    qn = _rms(cq, qg_ref[...]).astype(BF16)
    ckvn = _rms(ckv, kvg_ref[...]).astype(BF16)
    q = jnp.dot(qn, wuq_ref[...], preferred_element_type=F32)
    kn = jnp.dot(ckvn, wuk_ref[...], preferred_element_type=F32)
    scale = LOG2E * (MLA_NOPE + MLA_ROPE) ** -0.5
    for h in range(MLA_HEADS):
        sl = slice(HEAD_PAD * h, HEAD_PAD * (h + 1))
        qh = _rope_tile(q[:, sl], cos_a, sin_a_lo, sin_a_hi, MLA_ROPE // 2)
        q_ref[0, :, sl] = (qh * scale).astype(BF16)
        k_ref[0, :, sl] = (kn[:, sl] + kr).astype(BF16)
    vt = lax.dot_general(wuvt_ref[...], ckvn, (((1,), (1,)), ((), ())), preferred_element_type=F32)
    row = lax.broadcasted_iota(jnp.int32, vt.shape, 0)
    vt_ref[0, 0] = jnp.where(row % VT_ROWS >= MLA_V, 1.0, vt).astype(BF16)

    pd = jnp.dot(u, wd_ref[...], preferred_element_type=F32)
    dscale = LOG2E * DIL_HEAD_DIM ** -0.5
    nc = DIL_WIDTH // LANES
    for c in range(nc):
        sl = slice(LANES * c, LANES * (c + 1))
        ks = slice(DIL_WIDTH + LANES * c, DIL_WIDTH + LANES * (c + 1))
        dscr_ref[c] = _rope_tile(pd[:, sl], cos_b, sin_b_lo, sin_b_hi, ROPE_DIM // 2) * dscale
        dscr_ref[nc + c] = _rope_tile(pd[:, ks], cos_b, sin_b_lo, sin_b_hi, ROPE_DIM // 2)
        dscr_ref[2 * nc + c] = pd[:, 2 * DIL_WIDTH + LANES * c:2 * DIL_WIDTH + LANES * (c + 1)]
    tm = dscr_ref.shape[1]
    for pi, (_, dil) in enumerate(DIL_PATTERNS):
        for r in range(dil):
            for t in range(3):
                for c in range(nc):
                    rows = dscr_ref[nc * t + c, pl.ds(r, tm // dil, stride=dil), :] if dil > 1 else dscr_ref[nc * t + c]
                    dil_refs[3 * pi + t][0, r, :, LANES * c:LANES * (c + 1)] = rows.astype(BF16)

    pg = jnp.dot(u, wg_ref[...], preferred_element_type=F32) + bg_ref[...]
    gate_ref[0] = jax.nn.sigmoid(pg).astype(BF16)


def _rope_tables():
    half_a = MLA_ROPE // 2
    inv_a = 1.0 / (jnp.float32(MLA_THETA) ** (jnp.arange(half_a, dtype=F32) / half_a))
    half_b = ROPE_DIM // 2
    inv_b = 1.0 / (jnp.float32(ROPE_THETA) ** (jnp.arange(half_b, dtype=F32) / half_b))
    z = lambda n: jnp.zeros((n,), F32)
    o = lambda n: jnp.ones((n,), F32)
    row_a = jnp.concatenate([z(MLA_NOPE), inv_a, inv_a, z(HEAD_PAD - MLA_NOPE - MLA_ROPE)])
    lo_a = jnp.concatenate([z(MLA_NOPE), -o(half_a), z(half_a), z(HEAD_PAD - MLA_NOPE - MLA_ROPE)])
    hi_a = jnp.concatenate([z(MLA_NOPE), z(half_a), o(half_a), z(HEAD_PAD - MLA_NOPE - MLA_ROPE)])
    rest = DIL_HEAD_DIM - ROPE_DIM
    head_b = jnp.concatenate([inv_b, inv_b, z(rest)])
    head_lo = jnp.concatenate([-o(half_b), z(half_b), z(rest)])
    head_hi = jnp.concatenate([z(half_b), o(half_b), z(rest)])
    rep = LANES // DIL_HEAD_DIM
    rows = [row_a, lo_a, hi_a, jnp.tile(head_b, rep), jnp.tile(head_lo, rep), jnp.tile(head_hi, rep), z(LANES), z(LANES)]
    return jnp.stack(rows)


def _mixer_in(h1, positions, mix_pre_g, w_in, b_gate, q_norm_g, w_uq, kv_norm_g, w_uk, w_uv):
    b, s, _ = h1.shape
    o0 = 0
    o1 = o0 + MLA_Q_RANK
    o2 = o1 + MLA_KV_RANK
    o3 = o2 + MLA_ROPE
    o4 = o3 + 3 * DIL_WIDTH
    pad_hi = HEAD_PAD - MLA_NOPE - MLA_ROPE
    w_kr = jnp.pad(w_in[:, o2:o3], ((0, 0), (MLA_NOPE, pad_hi)))
    wa = jnp.concatenate([w_in[:, o0:o2], w_kr], axis=1).astype(BF16)
    wd = w_in[:, o3:o4].astype(BF16)
    wg = w_in[:, o4:].astype(BF16)
    wuq = jnp.pad(w_uq.reshape(MLA_Q_RANK, MLA_HEADS, MLA_NOPE + MLA_ROPE),
                  ((0, 0), (0, 0), (0, pad_hi))).reshape(MLA_Q_RANK, MLA_HEADS * HEAD_PAD).astype(BF16)
    wuk = jnp.pad(w_uk.reshape(MLA_KV_RANK, MLA_HEADS, MLA_NOPE),
                  ((0, 0), (0, 0), (0, HEAD_PAD - MLA_NOPE))).reshape(MLA_KV_RANK, MLA_HEADS * HEAD_PAD).astype(BF16)
    wuvt = jnp.pad(w_uv.T.reshape(MLA_HEADS, MLA_V, MLA_KV_RANK), ((0, 0), (0, VT_ROWS - MLA_V), (0, 0))
                   ).reshape(MLA_HEADS * VT_ROWS, MLA_KV_RANK).astype(BF16)
    pos = positions.astype(F32).reshape(b, s, 1)
    tm = MIX_TM
    hp = MLA_HEADS * HEAD_PAD
    tok = lambda c: pl.BlockSpec((1, tm, c), lambda bi, i: (bi, i, 0))
    out_shapes = (
        jax.ShapeDtypeStruct((b, s, hp), BF16),
        jax.ShapeDtypeStruct((b, s, hp), BF16),
        jax.ShapeDtypeStruct((b, s // FLASH_TK, MLA_HEADS * VT_ROWS, FLASH_TK), BF16),
        jax.ShapeDtypeStruct((b, s, N_BRANCH * D_MODEL), BF16),
    )
    per_chunk = FLASH_TK // tm
    out_specs = (
        tok(hp), tok(hp),
        pl.BlockSpec((1, 1, MLA_HEADS * VT_ROWS, tm), lambda bi, i: (bi, i // per_chunk, 0, i % per_chunk)),
        tok(N_BRANCH * D_MODEL),
    )
    for _, dil in DIL_PATTERNS:
        out_shapes += (jax.ShapeDtypeStruct((b, dil, s // dil, DIL_WIDTH), BF16),) * 3
        out_specs += (pl.BlockSpec((1, dil, tm // dil, DIL_WIDTH), lambda bi, i: (bi, 0, i, 0)),) * 3
    outs = pl.pallas_call(
        _mixer_in_kernel,
        name="mixer_in",
        grid=(b, s // tm),
        in_specs=[
            tok(D_MODEL), tok(1),
            _const_spec((1, D_MODEL)),
            _const_spec(wa.shape), _const_spec(wd.shape), _const_spec(wg.shape),
            _const_spec((1, N_BRANCH * D_MODEL)),
            _const_spec((1, MLA_Q_RANK)), _const_spec((1, MLA_KV_RANK)),
            _const_spec(wuq.shape), _const_spec(wuk.shape), _const_spec(wuvt.shape),
            _const_spec((8, LANES)),
        ],
        out_specs=out_specs,
        out_shape=out_shapes,
        scratch_shapes=[pltpu.VMEM((3 * DIL_WIDTH // LANES, tm, LANES), F32)],
        compiler_params=pltpu.CompilerParams(dimension_semantics=("arbitrary", "arbitrary"),
                                             vmem_limit_bytes=VMEM_LIMIT),
    )(h1, pos, mix_pre_g.reshape(1, D_MODEL), wa, wd, wg, b_gate.reshape(1, -1),
      q_norm_g.reshape(1, -1), kv_norm_g.reshape(1, -1), wuq, wuk, wuvt, _rope_tables())
    q, k, vt, gates = outs[:4]
    dil_qkv = [outs[4 + 3 * i:7 + 3 * i] for i in range(len(DIL_PATTERNS))]
    return q, k, vt, gates, dil_qkv


def _flash_kernel(q_ref, k_ref, vt_ref, o_ref, s0_ref, s1_ref, s2_ref, s3_ref, qt_ref):
    nk = k_ref.shape[1] // FLASH_TK
    nq = q_ref.shape[1] // FLASH_TQ
    assert nk % 4 == 0
    bufs = (s0_ref, s1_ref, s2_ref, s3_ref)

    def scores(t, c, hh, buf):
        cmax = None
        for part in range(FLASH_TK // FLASH_TS):
            k0 = c * FLASH_TK + part * FLASH_TS
            if not isinstance(k0, int):
                k0 = pl.multiple_of(k0, FLASH_TS)
            k = k_ref[0, pl.ds(k0, FLASH_TS), HEAD_PAD * hh:HEAD_PAD * (hh + 1)]
            st = jnp.dot(k, qt_ref[hh, t], preferred_element_type=F32)
            buf[hh, FLASH_TS * part:FLASH_TS * (part + 1)] = st
            pm = jnp.max(st, axis=0, keepdims=True)
            cmax = pm if cmax is None else jnp.maximum(cmax, pm)
        return cmax

    def consume(c, hh, buf, m, cmax, acc):
        m_new = jnp.maximum(m, cmax)
        alpha = jnp.exp2(m - m_new)
        p = jnp.exp2((buf[hh] - m_new).astype(BF16))
        vt = vt_ref[0, c, VT_ROWS * hh:VT_ROWS * (hh + 1), :]
        return m_new, alpha * acc + jnp.dot(vt, p, preferred_element_type=F32)

    def half(c, state, pair_in, pair_out, t_next, c_next):
        new = []
        for hh in range(2):
            m, cm0, cm1, acc = state[hh]
            nxt = [scores(t_next, c_next + i, hh, bufs[pair_out + i]) for i in range(2)]
            m, acc = consume(c, hh, bufs[pair_in], m, cm0, acc)
            m, acc = consume(c + 1, hh, bufs[pair_in + 1], m, cm1, acc)
            new.append((m, nxt[0], nxt[1], acc))
        return tuple(new)

    def tile(t, cmaxes):
        state = tuple((jnp.full((1, FLASH_TQ), -jnp.inf, F32), cmaxes[hh][0], cmaxes[hh][1],
                       jnp.zeros((VT_ROWS, FLASH_TQ), F32)) for hh in range(2))

        def group(j, st):
            c = 4 * j
            return half(c + 2, half(c, st, 0, 2, t, c + 2), 2, 0, t, c + 4)

        state = lax.fori_loop(0, nk // 4 - 1, group, state)
        t_next = jnp.minimum(t + 1, nq - 1)
        state = half(nk - 2, half(nk - 4, state, 0, 2, t, nk - 2), 2, 0, t_next, 0)
        outs = [acc[:MLA_V] / acc[MLA_V:MLA_V + 1] for (_, _, _, acc) in state]
        o_ref[0, pl.ds(pl.multiple_of(t * FLASH_TQ, FLASH_TQ), FLASH_TQ), :] = (
            jnp.concatenate(outs, axis=0).T.astype(BF16))
        return tuple((cm0, cm1) for (_, cm0, cm1, _) in state)

    def transpose_q(t, carry):
        for hh in range(2):
            q = q_ref[0, pl.ds(pl.multiple_of(t * FLASH_TQ, FLASH_TQ), FLASH_TQ), HEAD_PAD * hh:HEAD_PAD * (hh + 1)]
            qt_ref[hh, t] = q.astype(F32).T.astype(BF16)
        return carry

    lax.fori_loop(0, nq, transpose_q, 0)
    first = tuple((scores(0, 0, hh, bufs[0]), scores(0, 1, hh, bufs[1])) for hh in range(2))
    lax.fori_loop(0, nq, tile, first)


def _mla_flash(q, k, vt):
    b, s, _ = q.shape
    nk = s // FLASH_TK
    return pl.pallas_call(
        _flash_kernel,
        name="mla_flash",
        grid=(b, MLA_HEADS // 2),
        in_specs=[
            pl.BlockSpec((1, s, 2 * HEAD_PAD), lambda bi, hp: (bi, 0, hp)),
            pl.BlockSpec((1, s, 2 * HEAD_PAD), lambda bi, hp: (bi, 0, hp)),
            pl.BlockSpec((1, nk, 2 * VT_ROWS, FLASH_TK), lambda bi, hp: (bi, 0, hp, 0)),
        ],
        out_specs=pl.BlockSpec((1, s, 2 * MLA_V), lambda bi, hp: (bi, 0, hp)),
        out_shape=jax.ShapeDtypeStruct((b, s, MLA_WIDTH), BF16),
        scratch_shapes=[pltpu.VMEM((2, FLASH_TK, FLASH_TQ), F32)] * 4
                       + [pltpu.VMEM((2, s // FLASH_TQ, HEAD_PAD, FLASH_TQ), BF16)],
        compiler_params=pltpu.CompilerParams(dimension_semantics=("arbitrary", "arbitrary"),
                                             vmem_limit_bytes=VMEM_LIMIT),
    )(q, k, vt)


def _dilated_kernel(q_ref, kc_ref, kp_ref, kn_ref, vc_ref, vp_ref, vn_ref, o_ref, lse_ref,
                    kw_ref, vt_ref, ot_ref, lt_ref, *sbufs, n_rows):
    lb = q_ref.shape[2]
    l0 = pl.program_id(2) * lb
    wrows = lb + 2 * DIL_HALF
    kw_ref[0:DIL_HALF] = kp_ref[0, 0]
    kw_ref[DIL_HALF:DIL_HALF + lb] = kc_ref[0, 0]
    kw_ref[DIL_HALF + lb:] = kn_ref[0, 0]
    vw = jnp.concatenate([vp_ref[0, 0], vc_ref[0, 0], vn_ref[0, 0]], axis=0).astype(F32)
    vt = vw.T.astype(BF16)
    ones = jnp.ones((VT_ROWS - DIL_HEAD_DIM, wrows), BF16)
    for h in range(DIL_HEADS):
        vt_ref[VT_ROWS * h:VT_ROWS * h + DIL_HEAD_DIM] = vt[DIL_HEAD_DIM * h:DIL_HEAD_DIM * (h + 1)]
        vt_ref[VT_ROWS * h + DIL_HEAD_DIM:VT_ROWS * (h + 1)] = ones

    win = DIL_SUB + 2 * DIL_HALF
    krow = lax.broadcasted_iota(jnp.int32, (win, DIL_SUB), 0)
    qcol = lax.broadcasted_iota(jnp.int32, (win, DIL_SUB), 1)
    in_band = (krow - qcol).astype(jnp.uint32) <= 2 * DIL_HALF
    lane = lax.broadcasted_iota(jnp.int32, (DIL_SUB, LANES), 1)
    nbuf = len(sbufs)

    def produce(u):
        j, h = divmod(u, DIL_HEADS)
        ls = slice(LANES * (h // 2), LANES * (h // 2 + 1))
        kpair = kw_ref[j * DIL_SUB:j * DIL_SUB + win, ls]
        qpair = q_ref[0, 0, j * DIL_SUB:(j + 1) * DIL_SUB, ls]
        mine = (lane >= DIL_HEAD_DIM) if h % 2 else (lane < DIL_HEAD_DIM)
        qh = jnp.where(mine, qpair, jnp.zeros_like(qpair))
        st = lax.dot_general(kpair, qh, (((1,), (1,)), ((), ())), preferred_element_type=F32)
        kidx = l0 - DIL_HALF + j * DIL_SUB + krow
        st = jnp.where(in_band & (kidx.astype(jnp.uint32) < n_rows), st, NEG)
        sbufs[u % nbuf][...] = st
        return jnp.max(st, axis=0, keepdims=True)

    def consume(u, m):
        j, h = divmod(u, DIL_HEADS)
        p = jnp.exp2((sbufs[u % nbuf][...] - m).astype(BF16))
        r = jnp.dot(vt_ref[VT_ROWS * h:VT_ROWS * (h + 1), j * DIL_SUB:j * DIL_SUB + win], p,
                    preferred_element_type=F32)
        den = r[DIL_HEAD_DIM:DIL_HEAD_DIM + 1]
        hs = slice(DIL_HEAD_DIM * h, DIL_HEAD_DIM * (h + 1))
        ot_ref[hs] = r[:DIL_HEAD_DIM] / den
        lt_ref[hs] = jnp.broadcast_to(m + jnp.log2(den), (DIL_HEAD_DIM, DIL_SUB))
        if h == DIL_HEADS - 1:
            o_ref[0, 0, j * DIL_SUB:(j + 1) * DIL_SUB, :] = ot_ref[...].T
            lse_ref[0, 0, j * DIL_SUB:(j + 1) * DIL_SUB, :] = lt_ref[...].T

    n_units = (lb // DIL_SUB) * DIL_HEADS
    grp = nbuf // 2
    m_cur = [produce(i) for i in range(grp)]
    for u in range(0, n_units, grp):
        m_next = [produce(u + grp + i) for i in range(grp)] if u + grp < n_units else None
        for i in range(grp):
            consume(u + i, m_cur[i])
        m_cur = m_next


def _dilated(dq, dk, dv, dilation):
    b, _, n_rows, w = dq.shape
    lb = min(DIL_LB, n_rows)
    hb = lb // DIL_HALF
    last = n_rows // DIL_HALF - 1
    cur = pl.BlockSpec((1, 1, lb, w), lambda bi, r, l: (bi, r, l, 0))
    prev = pl.BlockSpec((1, 1, DIL_HALF, w), lambda bi, r, l: (bi, r, jnp.maximum(l * hb - 1, 0), 0))
    nxt = pl.BlockSpec((1, 1, DIL_HALF, w), lambda bi, r, l: (bi, r, jnp.minimum((l + 1) * hb, last), 0))
    return pl.pallas_call(
        functools.partial(_dilated_kernel, n_rows=n_rows),
        name=f"dilated_{dilation}",
        grid=(b, dilation, n_rows // lb),
        in_specs=[cur, cur, prev, nxt, cur, prev, nxt],
        out_specs=(cur, cur),
        out_shape=(jax.ShapeDtypeStruct((b, dilation, n_rows, w), F32),) * 2,
        scratch_shapes=[pltpu.VMEM((lb + 2 * DIL_HALF, w), BF16),
                        pltpu.VMEM((DIL_HEADS * VT_ROWS, lb + 2 * DIL_HALF), BF16),
                        pltpu.VMEM((w, DIL_SUB), F32),
                        pltpu.VMEM((w, DIL_SUB), F32)]
                       + [pltpu.VMEM((DIL_SUB + 2 * DIL_HALF, DIL_SUB), F32)] * (2 * DIL_GROUP),
        compiler_params=pltpu.CompilerParams(dimension_semantics=("arbitrary",) * 3, vmem_limit_bytes=VMEM_LIMIT),
    )(dq, dk, dk, dk, dv, dv, dv)


def _mixer_out_kernel(h_ref, oa_ref, o1_ref, l1_ref, o2_ref, l2_ref, o3_ref, l3_ref, gate_ref,
                      wba_ref, wbb_ref, wo_ref, post_ref, out_ref, *scr):
    tm = h_ref.shape[1]

    def token_major(ref, dil, buf):
        if dil == 1:
            return ref[0, 0]
        nc = DIL_WIDTH // LANES
        for r in range(dil):
            for c in range(nc):
                buf[c, pl.ds(r, tm // dil, stride=dil), :] = ref[0, r, :, LANES * c:LANES * (c + 1)]
        return jnp.concatenate([buf[c] for c in range(nc)], axis=1)

    bufs = iter(scr)
    os_, ls_ = [], []
    for (o_r, l_r), (_, dil) in zip(((o1_ref, l1_ref), (o2_ref, l2_ref), (o3_ref, l3_ref)), DIL_PATTERNS):
        os_.append(token_major(o_r, dil, None if dil == 1 else next(bufs)))
        ls_.append(token_major(l_r, dil, None if dil == 1 else next(bufs)))
    lm = jnp.maximum(jnp.maximum(ls_[0], ls_[1]), ls_[2])
    es = [jnp.exp2(l - lm) for l in ls_]
    ob = (es[0] * os_[0] + es[1] * os_[1] + es[2] * os_[2]) / (es[0] + es[1] + es[2])
    ya =jnp.dot(oa_ref[0], wba_ref[...], preferred_element_type=F32)
    yb = jnp.dot(ob.astype(BF16), wbb_ref[...], preferred_element_type=F32)
    gate = gate_ref[0].astype(F32)
    merged = gate[:, :D_MODEL] * ya + gate[:, D_MODEL:] * yb
    mix = jnp.dot(merged.astype(BF16), wo_ref[...], preferred_element_type=F32)
    out_ref[0] = h_ref[0] + _rms(mix, post_ref[...])


def _mixer_out(h1, oa, parts, gates, w_branch_a, w_branch_b, w_out, mix_post_g):
    b, s, _ = h1.shape
    tm = MIX_TM
    tok = lambda c: pl.BlockSpec((1, tm, c), lambda bi, i: (bi, i, 0))
    flat, part_specs, n_scr = [], [], 0
    for (_, dil), pair in zip(DIL_PATTERNS, parts):
        flat += list(pair)
        part_specs += [pl.BlockSpec((1, dil, tm // dil, DIL_WIDTH), lambda bi, i: (bi, 0, i, 0))] * 2
        n_scr += 2 * (dil > 1)
    return pl.pallas_call(
        _mixer_out_kernel,
        name="mixer_out",
        grid=(b, s // tm),
        scratch_shapes=[pltpu.VMEM((DIL_WIDTH // LANES, tm, LANES), F32)] * n_scr,
        in_specs=[tok(D_MODEL), tok(MLA_WIDTH)] + part_specs + [tok(N_BRANCH * D_MODEL),
                  _const_spec((MLA_WIDTH, D_MODEL)), _const_spec((DIL_WIDTH, D_MODEL)),
                  _const_spec((D_MODEL, D_MODEL)), _const_spec((1, D_MODEL))],
        out_specs=tok(D_MODEL),
        out_shape=jax.ShapeDtypeStruct((b, s, D_MODEL), F32),
        compiler_params=pltpu.CompilerParams(dimension_semantics=("arbitrary", "arbitrary"),
                                             vmem_limit_bytes=VMEM_LIMIT),
    )(h1, oa, *flat, gates, w_branch_a.astype(BF16), w_branch_b.astype(BF16), w_out.astype(BF16),
      mix_post_g.reshape(1, D_MODEL))


def kernel(x, positions, ffn1_pre_g, ffn1_post_g, ffn1_w_gate, ffn1_w_up, ffn1_w_down, mix_pre_g, w_in, b_gate,
           q_norm_g, w_uq, kv_norm_g, w_uk, w_uv, w_branch_a, w_branch_b, w_out, mix_post_g,
           ffn2_pre_g, ffn2_post_g, ffn2_w_gate, ffn2_w_up, ffn2_w_down):
    b, s, d = x.shape
    depth = ffn1_pre_g.shape[0]
    assert all(win // (2 * dil) == DIL_HALF and s % (dil * DIL_SUB) == 0 for win, dil in DIL_PATTERNS)
    h = x
    for l in range(depth):
        h = _ffn(h.reshape(b * s, d), ffn1_pre_g[l], ffn1_post_g[l], ffn1_w_gate[l], ffn1_w_up[l],
                 ffn1_w_down[l]).reshape(b, s, d)
        q, k, vt, gates, dil_qkv = _mixer_in(h, positions, mix_pre_g[l], w_in[l], b_gate[l], q_norm_g[l],
                                             w_uq[l], kv_norm_g[l], w_uk[l], w_uv[l])
        oa = _mla_flash(q, k, vt)
        parts = [_dilated(*qkv, dil) for qkv, (_, dil) in zip(dil_qkv, DIL_PATTERNS)]
        h = _mixer_out(h, oa, parts, gates, w_branch_a[l], w_branch_b[l], w_out[l], mix_post_g[l])
        h = _ffn(h.reshape(b * s, d), ffn2_pre_g[l], ffn2_post_g[l], ffn2_w_gate[l], ffn2_w_up[l],
                 ffn2_w_down[l]).reshape(b, s, d)
    return h
```

```python
import functools

import jax
import jax.numpy as jnp
import numpy as np
from jax import lax
from jax.experimental import pallas as pl
from jax.experimental.pallas import tpu as pltpu

F32 = jnp.float32
BF16 = jnp.bfloat16

D_MODEL = 1024
D_FF = 2816
EPS = 1e-6
MLA_HEADS = 8
MLA_Q_RANK = 384
MLA_KV_RANK = 256
MLA_NOPE = 64
MLA_ROPE = 32
MLA_V = 64
MLA_THETA = 10000.0
MLA_WIDTH = MLA_HEADS * MLA_V
DIL_HEADS = 8
DIL_HEAD_DIM = 64
DIL_PATTERNS = ((128, 1), (512, 4), (2048, 16))
DIL_WIDTH = DIL_HEADS * DIL_HEAD_DIM
ROPE_THETA = 500000.0
ROPE_DIM = DIL_HEAD_DIM // 4
N_BRANCH = 2
NEG = -1e30

LANES = 128
HEAD_PAD = LANES
VMEM_LIMIT = 56 * 1024 * 1024

FFN_TM = 512
FFN_FC = 256
MIX_TM = 512
FLASH_TQ = 256
FLASH_TK = MIX_TM
FLASH_TS = 512
VT_ROWS = MLA_V + 16
LOG2E = 1.4426950408889634
DIL_LB = 512
DIL_SUB = 256
DIL_GROUP = 4
DIL_HALF = 64


def _rms(x, g):
    ms = jnp.mean(x * x, axis=-1, keepdims=True)
    return x * lax.rsqrt(ms + EPS) * g


def _const_spec(shape):
    nd = len(shape)
    return pl.BlockSpec(shape, lambda *_: (0,) * nd, pipeline_mode=pl.Buffered(1))


def _ffn_kernel(x_ref, pre_ref, post_ref, wgu_ref, wd_ref, o_ref, hm_ref):
    x = x_ref[...]
    xn = _rms(x, pre_ref[...]).astype(BF16)
    for c in range(D_FF // FFN_FC):
        gu = jnp.dot(xn, wgu_ref[:, 2 * FFN_FC * c:2 * FFN_FC * (c + 1)], preferred_element_type=F32)
        g = gu[:, :FFN_FC]
        u = gu[:, FFN_FC:]
        hm_ref[:, FFN_FC * c:FFN_FC * (c + 1)] = (g * jax.nn.sigmoid(g) * u).astype(BF16)
    f = jnp.dot(hm_ref[...], wd_ref[...], preferred_element_type=F32)
    o_ref[...] = x + 0.5 * _rms(f, post_ref[...])


def _ffn(h, pre_g, post_g, w_gate, w_up, w_down):
    t = h.shape[0]
    nc = D_FF // FFN_FC
    wgu = jnp.concatenate([w_gate.reshape(D_MODEL, nc, FFN_FC), w_up.reshape(D_MODEL, nc, FFN_FC)],
                          axis=-1).reshape(D_MODEL, 2 * D_FF).astype(BF16)
    wd = w_down.astype(BF16)
    return pl.pallas_call(
        _ffn_kernel,
        name="ffn",
        grid=(t // FFN_TM,),
        in_specs=[
            pl.BlockSpec((FFN_TM, D_MODEL), lambda i: (i, 0)),
            _const_spec((1, D_MODEL)),
            _const_spec((1, D_MODEL)),
            _const_spec((D_MODEL, 2 * D_FF)),
            _const_spec((D_FF, D_MODEL)),
        ],
        out_specs=pl.BlockSpec((FFN_TM, D_MODEL), lambda i: (i, 0)),
        out_shape=jax.ShapeDtypeStruct((t, D_MODEL), F32),
        scratch_shapes=[pltpu.VMEM((FFN_TM, D_FF), BF16)],
        compiler_params=pltpu.CompilerParams(dimension_semantics=("arbitrary",), vmem_limit_bytes=VMEM_LIMIT),
    )(h, pre_g.reshape(1, D_MODEL), post_g.reshape(1, D_MODEL), wgu, wd)


def _rope_tile(x, cos, sin):
    return x * cos + pltpu.roll(x, LANES // 2, 1) * sin


def _mixer_in_kernel(h_ref, pos_ref, pre_ref, wa_ref, wd_ref, wg_ref, bg_ref, qg_ref, kvg_ref,
                     wuq_ref, wuk_ref, wuvt_ref, tab_ref,
                     q_ref, k_ref, vt_ref, gate_ref, *rest):
    dil_refs, dscr_ref = rest[:-1], rest[-1]
    u = _rms(h_ref[0], pre_ref[...]).astype(BF16)
    pos = pos_ref[0]
    ang = pos * tab_ref[0:1, :]
    cc = jnp.cos(ang)
    ss = jnp.sin(ang)
    rcc = pltpu.roll(cc, LANES // 2, 1)
    rss = pltpu.roll(ss, LANES // 2, 1)
    low = lax.broadcasted_iota(jnp.int32, ang.shape, 1) < LANES // 2
    cos_a = jnp.where(low, cc, rcc)
    sin_a = jnp.where(low, -ss, rss)
    cos_b = jnp.where(low, rcc, cc)
    sin_b = jnp.where(low, -rss, ss)

    pa = jnp.dot(u, wa_ref[...], preferred_element_type=F32)
    cq = pa[:, :MLA_Q_RANK]
    ckv = pa[:, MLA_Q_RANK:MLA_Q_RANK + MLA_KV_RANK]
    kr = _rope_tile(pa[:, MLA_Q_RANK + MLA_KV_RANK:], cos_a, sin_a)
    qn = _rms(cq, qg_ref[...]).astype(BF16)
    ckvn = _rms(ckv, kvg_ref[...]).astype(BF16)
    q = jnp.dot(qn, wuq_ref[...], preferred_element_type=F32)
    kn = jnp.dot(ckvn, wuk_ref[...], preferred_element_type=F32)
    scale = LOG2E * (MLA_NOPE + MLA_ROPE) ** -0.5
    for h in range(MLA_HEADS):
        sl = slice(HEAD_PAD * h, HEAD_PAD * (h + 1))
        qh = _rope_tile(q[:, sl], cos_a, sin_a)
        q_ref[0, :, sl] = (qh * scale).astype(BF16)
        k_ref[0, :, sl] = (kn[:, sl] + kr).astype(BF16)
    vt = lax.dot_general(wuvt_ref[...], ckvn, (((1,), (1,)), ((), ())), preferred_element_type=F32)
    row = lax.broadcasted_iota(jnp.int32, vt.shape, 0)
    vt_ref[0, 0] = jnp.where(row % VT_ROWS >= MLA_V, 1.0, vt).astype(BF16)

    pd = jnp.dot(u, wd_ref[...], preferred_element_type=F32)
    dscale = LOG2E * DIL_HEAD_DIM ** -0.5
    nc = DIL_WIDTH // LANES
    for c in range(nc):
        sl = slice(LANES * c, LANES * (c + 1))
        ks = slice(DIL_WIDTH + LANES * c, DIL_WIDTH + LANES * (c + 1))
        dscr_ref[c] = _rope_tile(pd[:, sl], cos_b, sin_b) * dscale
        dscr_ref[nc + c] = _rope_tile(pd[:, ks], cos_b, sin_b)
        dscr_ref[2 * nc + c] = pd[:, 2 * DIL_WIDTH + LANES * c:2 * DIL_WIDTH + LANES * (c + 1)]
    tm = dscr_ref.shape[1]
    for pi, (_, dil) in enumerate(DIL_PATTERNS):
        for t in range(3):
            for c in range(nc):
                if dil == 1:
                    dil_refs[3 * pi + t][0, 0, :, LANES * c:LANES * (c + 1)] = dscr_ref[nc * t + c].astype(BF16)
                    continue
                for r in range(dil):
                    rows = dscr_ref[nc * t + c, pl.ds(r, tm // dil, stride=dil), :]
                    dil_refs[3 * pi + t][0, r, :, LANES * c:LANES * (c + 1)] = rows.astype(BF16)

    pg = jnp.dot(u, wg_ref[...], preferred_element_type=F32) + bg_ref[...]
    gate_ref[0] = jax.nn.sigmoid(pg).astype(BF16)


HALF_A = MLA_ROPE // 2
HALF_B = ROPE_DIM // 2
NOPE_LO = LANES // 2 - HALF_A


def _rope_table():
    inv_a = 1.0 / (jnp.float32(MLA_THETA) ** (jnp.arange(HALF_A, dtype=F32) / HALF_A))
    inv_b = 1.0 / (jnp.float32(ROPE_THETA) ** (jnp.arange(HALF_B, dtype=F32) / HALF_B))
    z = lambda n: jnp.zeros((n,), F32)
    row = jnp.concatenate([inv_a, z(LANES // 2 - HALF_A), inv_b, inv_b, z(LANES // 2 - 2 * HALF_B)])
    return jnp.zeros((8, LANES), F32).at[0].set(row)


def _mla_tile_cols(nope, rope):
    lead = (nope if nope is not None else rope).shape[:-1]
    zeros = lambda n: jnp.zeros(lead + (n,), F32)
    x1, x2 = (rope[..., :HALF_A], rope[..., HALF_A:]) if rope is not None else (zeros(HALF_A), zeros(HALF_A))
    n_lo, n_hi = (nope[..., :NOPE_LO], nope[..., NOPE_LO:]) if nope is not None else (zeros(NOPE_LO), zeros(MLA_NOPE - NOPE_LO))
    return jnp.concatenate([x1, n_lo, x2, n_hi, zeros(HEAD_PAD - MLA_NOPE - MLA_ROPE)], axis=-1)


def _dil_tile_cols(w):
    wh = w.reshape(w.shape[0], DIL_HEADS // 2, 2, DIL_HEAD_DIM)
    a, b = wh[:, :, 0], wh[:, :, 1]
    tile = jnp.concatenate([a[..., :HALF_B], b[..., :HALF_B], a[..., ROPE_DIM:], a[..., HALF_B:ROPE_DIM],
                            b[..., HALF_B:ROPE_DIM], b[..., ROPE_DIM:]], axis=-1)
    return tile.reshape(w.shape[0], DIL_WIDTH)


def _mixer_in(h1, positions, mix_pre_g, w_in, b_gate, q_norm_g, w_uq, kv_norm_g, w_uk, w_uv):
    b, s, _ = h1.shape
    o0 = 0
    o1 = o0 + MLA_Q_RANK
    o2 = o1 + MLA_KV_RANK
    o3 = o2 + MLA_ROPE
    o4 = o3 + 3 * DIL_WIDTH
    w_kr = _mla_tile_cols(None, w_in[:, o2:o3])
    wa = jnp.concatenate([w_in[:, o0:o2], w_kr], axis=1).astype(BF16)
    wd = jnp.concatenate([_dil_tile_cols(w_in[:, o3:o3 + DIL_WIDTH]),
                          _dil_tile_cols(w_in[:, o3 + DIL_WIDTH:o3 + 2 * DIL_WIDTH]),
                          w_in[:, o3 + 2 * DIL_WIDTH:o4]], axis=1).astype(BF16)
    wg = w_in[:, o4:].astype(BF16)
    wuq_h = w_uq.reshape(MLA_Q_RANK, MLA_HEADS, MLA_NOPE + MLA_ROPE)
    wuq = _mla_tile_cols(wuq_h[..., :MLA_NOPE], wuq_h[..., MLA_NOPE:]).reshape(MLA_Q_RANK, -1).astype(BF16)
    wuk = _mla_tile_cols(w_uk.reshape(MLA_KV_RANK, MLA_HEADS, MLA_NOPE), None).reshape(MLA_KV_RANK, -1).astype(BF16)
    wuvt = jnp.pad(w_uv.T.reshape(MLA_HEADS, MLA_V, MLA_KV_RANK), ((0, 0), (0, VT_ROWS - MLA_V), (0, 0))
                   ).reshape(MLA_HEADS * VT_ROWS, MLA_KV_RANK).astype(BF16)
    pos = positions.astype(F32).reshape(b, s, 1)
    tm = MIX_TM
    hp = MLA_HEADS * HEAD_PAD
    tok = lambda c: pl.BlockSpec((1, tm, c), lambda bi, i: (bi, i, 0))
    out_shapes = (
        jax.ShapeDtypeStruct((b, s, hp), BF16),
        jax.ShapeDtypeStruct((b, s, hp), BF16),
        jax.ShapeDtypeStruct((b, s // FLASH_TK, MLA_HEADS * VT_ROWS, FLASH_TK), BF16),
        jax.ShapeDtypeStruct((b, s, N_BRANCH * D_MODEL), BF16),
    )
    per_chunk = FLASH_TK // tm
    out_specs = (
        tok(hp), tok(hp),
        pl.BlockSpec((1, 1, MLA_HEADS * VT_ROWS, tm), lambda bi, i: (bi, i // per_chunk, 0, i % per_chunk)),
        tok(N_BRANCH * D_MODEL),
    )
    for _, dil in DIL_PATTERNS:
        out_shapes += (jax.ShapeDtypeStruct((b, dil, s // dil, DIL_WIDTH), BF16),) * 3
        out_specs += (pl.BlockSpec((1, dil, tm // dil, DIL_WIDTH), lambda bi, i: (bi, 0, i, 0)),) * 3
    outs = pl.pallas_call(
        _mixer_in_kernel,
        name="mixer_in",
        grid=(b, s // tm),
        in_specs=[
            tok(D_MODEL), tok(1),
            _const_spec((1, D_MODEL)),
            _const_spec(wa.shape), _const_spec(wd.shape), _const_spec(wg.shape),
            _const_spec((1, N_BRANCH * D_MODEL)),
            _const_spec((1, MLA_Q_RANK)), _const_spec((1, MLA_KV_RANK)),
            _const_spec(wuq.shape), _const_spec(wuk.shape), _const_spec(wuvt.shape),
            _const_spec((8, LANES)),
        ],
        out_specs=out_specs,
        out_shape=out_shapes,
        scratch_shapes=[pltpu.VMEM((3 * DIL_WIDTH // LANES, tm, LANES), F32)],
        compiler_params=pltpu.CompilerParams(dimension_semantics=("arbitrary", "arbitrary"),
                                             vmem_limit_bytes=VMEM_LIMIT),
    )(h1, pos, mix_pre_g.reshape(1, D_MODEL), wa, wd, wg, b_gate.reshape(1, -1),
      q_norm_g.reshape(1, -1), kv_norm_g.reshape(1, -1), wuq, wuk, wuvt, _rope_table())
    q, k, vt, gates = outs[:4]
    dil_qkv = [outs[4 + 3 * i:7 + 3 * i] for i in range(len(DIL_PATTERNS))]
    return q, k, vt, gates, dil_qkv


def _flash_kernel(q_ref, k_ref, vt_ref, o_ref, s0_ref, s1_ref, s2_ref, s3_ref, qt_ref):
    nk = k_ref.shape[1] // FLASH_TK
    nq = q_ref.shape[1] // FLASH_TQ
    assert nk % 4 == 0
    bufs = (s0_ref, s1_ref, s2_ref, s3_ref)

    def scores(t, c, hh, buf):
        cmax = None
        for part in range(FLASH_TK // FLASH_TS):
            k0 = c * FLASH_TK + part * FLASH_TS
            if not isinstance(k0, int):
                k0 = pl.multiple_of(k0, FLASH_TS)
            k = k_ref[0, pl.ds(k0, FLASH_TS), HEAD_PAD * hh:HEAD_PAD * (hh + 1)]
            st = jnp.dot(k, qt_ref[hh, t], preferred_element_type=F32)
            buf[hh, FLASH_TS * part:FLASH_TS * (part + 1)] = st
            pm = jnp.max(st, axis=0, keepdims=True)
            cmax = pm if cmax is None else jnp.maximum(cmax, pm)
        return cmax

    def consume(c, hh, buf, m, cmax, acc):
        m_new = jnp.maximum(m, cmax)
        alpha = jnp.exp2(m - m_new)
        p = jnp.exp2((buf[hh] - m_new).astype(BF16))
        vt = vt_ref[0, c, VT_ROWS * hh:VT_ROWS * (hh + 1), :]
        return m_new, alpha * acc + jnp.dot(vt, p, preferred_element_type=F32)

    def half(c, state, pair_in, pair_out, t_next, c_next):
        new = []
        for hh in range(2):
            m, cm0, cm1, acc = state[hh]
            nxt = [scores(t_next, c_next + i, hh, bufs[pair_out + i]) for i in range(2)]
            m, acc = consume(c, hh, bufs[pair_in], m, cm0, acc)
            m, acc = consume(c + 1, hh, bufs[pair_in + 1], m, cm1, acc)
            new.append((m, nxt[0], nxt[1], acc))
        return tuple(new)

    def tile(t, cmaxes):
        state = tuple((jnp.full((1, FLASH_TQ), -jnp.inf, F32), cmaxes[hh][0], cmaxes[hh][1],
                       jnp.zeros((VT_ROWS, FLASH_TQ), F32)) for hh in range(2))

        def group(j, st):
            c = 4 * j
            return half(c + 2, half(c, st, 0, 2, t, c + 2), 2, 0, t, c + 4)

        state = lax.fori_loop(0, nk // 4 - 1, group, state)
        t_next = jnp.minimum(t + 1, nq - 1)
        state = half(nk - 2, half(nk - 4, state, 0, 2, t, nk - 2), 2, 0, t_next, 0)
        outs = [acc[:MLA_V] / acc[MLA_V:MLA_V + 1] for (_, _, _, acc) in state]
        o_ref[0, pl.ds(pl.multiple_of(t * FLASH_TQ, FLASH_TQ), FLASH_TQ), :] = (
            jnp.concatenate(outs, axis=0).T.astype(BF16))
        return tuple((cm0, cm1) for (_, cm0, cm1, _) in state)

    def transpose_q(t, carry):
        for hh in range(2):
            q = q_ref[0, pl.ds(pl.multiple_of(t * FLASH_TQ, FLASH_TQ), FLASH_TQ), HEAD_PAD * hh:HEAD_PAD * (hh + 1)]
            qt_ref[hh, t] = q.astype(F32).T.astype(BF16)
        return carry

    lax.fori_loop(0, nq, transpose_q, 0)
    first = tuple((scores(0, 0, hh, bufs[0]), scores(0, 1, hh, bufs[1])) for hh in range(2))
    lax.fori_loop(0, nq, tile, first)


def _mla_flash(q, k, vt):
    b, s, _ = q.shape
    nk = s // FLASH_TK
    return pl.pallas_call(
        _flash_kernel,
        name="mla_flash",
        grid=(b, MLA_HEADS // 2),
        in_specs=[
            pl.BlockSpec((1, s, 2 * HEAD_PAD), lambda bi, hp: (bi, 0, hp)),
            pl.BlockSpec((1, s, 2 * HEAD_PAD), lambda bi, hp: (bi, 0, hp)),
            pl.BlockSpec((1, nk, 2 * VT_ROWS, FLASH_TK), lambda bi, hp: (bi, 0, hp, 0)),
        ],
        out_specs=pl.BlockSpec((1, s, 2 * MLA_V), lambda bi, hp: (bi, 0, hp)),
        out_shape=jax.ShapeDtypeStruct((b, s, MLA_WIDTH), BF16),
        scratch_shapes=[pltpu.VMEM((2, FLASH_TK, FLASH_TQ), F32)] * 4
                       + [pltpu.VMEM((2, s // FLASH_TQ, HEAD_PAD, FLASH_TQ), BF16)],
        compiler_params=pltpu.CompilerParams(dimension_semantics=("arbitrary", "arbitrary"),
                                             vmem_limit_bytes=VMEM_LIMIT),
    )(q, k, vt)


def _dilated_kernel(q_ref, kc_ref, kp_ref, kn_ref, vc_ref, vp_ref, vn_ref, o_ref, lse_ref,
                    kw_ref, vt_ref, ot_ref, lt_ref, *sbufs, n_rows):
    lb = q_ref.shape[2]
    l0 = pl.program_id(2) * lb
    wrows = lb + 2 * DIL_HALF
    kw_ref[0:DIL_HALF] = kp_ref[0, 0]
    kw_ref[DIL_HALF:DIL_HALF + lb] = kc_ref[0, 0]
    kw_ref[DIL_HALF + lb:] = kn_ref[0, 0]
    vw = jnp.concatenate([vp_ref[0, 0], vc_ref[0, 0], vn_ref[0, 0]], axis=0).astype(F32)
    vt = vw.T.astype(BF16)
    ones = jnp.ones((VT_ROWS - DIL_HEAD_DIM, wrows), BF16)
    for h in range(DIL_HEADS):
        vt_ref[VT_ROWS * h:VT_ROWS * h + DIL_HEAD_DIM] = vt[DIL_HEAD_DIM * h:DIL_HEAD_DIM * (h + 1)]
        vt_ref[VT_ROWS * h + DIL_HEAD_DIM:VT_ROWS * (h + 1)] = ones

    win = DIL_SUB + 2 * DIL_HALF
    krow = lax.broadcasted_iota(jnp.int32, (win, DIL_SUB), 0)
    qcol = lax.broadcasted_iota(jnp.int32, (win, DIL_SUB), 1)
    in_band = (krow - qcol).astype(jnp.uint32) <= 2 * DIL_HALF
    lane = lax.broadcasted_iota(jnp.int32, (DIL_SUB, LANES), 1)
    odd_lanes = ((lane >= HALF_B) & (lane < ROPE_DIM)) | (lane >= LANES // 2 + HALF_B)
    nbuf = len(sbufs)

    def produce(u):
        j, h = divmod(u, DIL_HEADS)
        ls = slice(LANES * (h // 2), LANES * (h // 2 + 1))
        kpair = kw_ref[j * DIL_SUB:j * DIL_SUB + win, ls]
        qpair = q_ref[0, 0, j * DIL_SUB:(j + 1) * DIL_SUB, ls]
        qh = jnp.where(odd_lanes if h % 2 else ~odd_lanes, qpair, jnp.zeros_like(qpair))
        st = lax.dot_general(kpair, qh, (((1,), (1,)), ((), ())), preferred_element_type=F32)
        kidx = l0 - DIL_HALF + j * DIL_SUB + krow
        st = jnp.where(in_band & (kidx.astype(jnp.uint32) < n_rows), st, NEG)
        sbufs[u % nbuf][...] = st
        return jnp.max(st, axis=0, keepdims=True)

    def consume(u, m):
        j, h = divmod(u, DIL_HEADS)
        p = jnp.exp2((sbufs[u % nbuf][...] - m).astype(BF16))
        r = jnp.dot(vt_ref[VT_ROWS * h:VT_ROWS * (h + 1), j * DIL_SUB:j * DIL_SUB + win], p,
                    preferred_element_type=F32)
        den = r[DIL_HEAD_DIM:DIL_HEAD_DIM + 1]
        hs = slice(DIL_HEAD_DIM * h, DIL_HEAD_DIM * (h + 1))
        ot_ref[hs] = r[:DIL_HEAD_DIM] / den
        lt_ref[hs] = jnp.broadcast_to(m + jnp.log2(den), (DIL_HEAD_DIM, DIL_SUB))
        if h == DIL_HEADS - 1:
            o_ref[0, 0, j * DIL_SUB:(j + 1) * DIL_SUB, :] = ot_ref[...].T.astype(BF16)
            lse_ref[0, 0, j * DIL_SUB:(j + 1) * DIL_SUB, :] = lt_ref[...].T

    n_units = (lb // DIL_SUB) * DIL_HEADS
    grp = nbuf // 2
    m_cur = [produce(i) for i in range(grp)]
    for u in range(0, n_units, grp):
        m_next = [produce(u + grp + i) for i in range(grp)] if u + grp < n_units else None
        for i in range(grp):
            consume(u + i, m_cur[i])
        m_cur = m_next


def _dilated(dq, dk, dv, dilation):
    b, _, n_rows, w = dq.shape
    lb = min(DIL_LB, n_rows)
    hb = lb // DIL_HALF
    last = n_rows // DIL_HALF - 1
    cur = pl.BlockSpec((1, 1, lb, w), lambda bi, r, l: (bi, r, l, 0))
    prev = pl.BlockSpec((1, 1, DIL_HALF, w), lambda bi, r, l: (bi, r, jnp.maximum(l * hb - 1, 0), 0))
    nxt = pl.BlockSpec((1, 1, DIL_HALF, w), lambda bi, r, l: (bi, r, jnp.minimum((l + 1) * hb, last), 0))
    return pl.pallas_call(
        functools.partial(_dilated_kernel, n_rows=n_rows),
        name=f"dilated_{dilation}",
        grid=(b, dilation, n_rows // lb),
        in_specs=[cur, cur, prev, nxt, cur, prev, nxt],
        out_specs=(cur, cur),
        out_shape=(jax.ShapeDtypeStruct((b, dilation, n_rows, w), BF16),
                   jax.ShapeDtypeStruct((b, dilation, n_rows, w), F32)),
        scratch_shapes=[pltpu.VMEM((lb + 2 * DIL_HALF, w), BF16),
                        pltpu.VMEM((DIL_HEADS * VT_ROWS, lb + 2 * DIL_HALF), BF16),
                        pltpu.VMEM((w, DIL_SUB), F32),
                        pltpu.VMEM((w, DIL_SUB), F32)]
                       + [pltpu.VMEM((DIL_SUB + 2 * DIL_HALF, DIL_SUB), F32)] * (2 * DIL_GROUP),
        compiler_params=pltpu.CompilerParams(dimension_semantics=("arbitrary",) * 3, vmem_limit_bytes=VMEM_LIMIT),
    )(dq, dk, dk, dk, dv, dv, dv)


def _mixer_out_kernel(h_ref, oa_ref, o1_ref, l1_ref, o2_ref, l2_ref, o3_ref, l3_ref, gate_ref,
                      wba_ref, wbb_ref, wo_ref, post_ref, out_ref, *scr):
    tm = h_ref.shape[1]

    def token_major(ref, dil, buf):
        if dil == 1:
            return ref[0, 0].astype(F32)
        nc = DIL_WIDTH // LANES
        for r in range(dil):
            for c in range(nc):
                buf[c, pl.ds(r, tm // dil, stride=dil), :] = ref[0, r, :, LANES * c:LANES * (c + 1)].astype(F32)
        return jnp.concatenate([buf[c] for c in range(nc)], axis=1)

    bufs = iter(scr)
    os_, ls_ = [], []
    for (o_r, l_r), (_, dil) in zip(((o1_ref, l1_ref), (o2_ref, l2_ref), (o3_ref, l3_ref)), DIL_PATTERNS):
        os_.append(token_major(o_r, dil, None if dil == 1 else next(bufs)))
        ls_.append(token_major(l_r, dil, None if dil == 1 else next(bufs)))
    lm = jnp.maximum(jnp.maximum(ls_[0], ls_[1]), ls_[2])
    es = [jnp.exp2(l - lm) for l in ls_]
    ob = (es[0] * os_[0] + es[1] * os_[1] + es[2] * os_[2]) / (es[0] + es[1] + es[2])
    ya =jnp.dot(oa_ref[0], wba_ref[...], preferred_element_type=F32)
    yb = jnp.dot(ob.astype(BF16), wbb_ref[...], preferred_element_type=F32)
    gate = gate_ref[0].astype(F32)
    merged = gate[:, :D_MODEL] * ya + gate[:, D_MODEL:] * yb
    mix = jnp.dot(merged.astype(BF16), wo_ref[...], preferred_element_type=F32)
    out_ref[0] = h_ref[0] + _rms(mix, post_ref[...])


def _mixer_out(h1, oa, parts, gates, w_branch_a, w_branch_b, w_out, mix_post_g):
    b, s, _ = h1.shape
    tm = MIX_TM
    tok = lambda c: pl.BlockSpec((1, tm, c), lambda bi, i: (bi, i, 0))
    flat, part_specs, n_scr = [], [], 0
    for (_, dil), pair in zip(DIL_PATTERNS, parts):
        flat += list(pair)
        part_specs += [pl.BlockSpec((1, dil, tm // dil, DIL_WIDTH), lambda bi, i: (bi, 0, i, 0))] * 2
        n_scr += 2 * (dil > 1)
    return pl.pallas_call(
        _mixer_out_kernel,
        name="mixer_out",
        grid=(b, s // tm),
        scratch_shapes=[pltpu.VMEM((DIL_WIDTH // LANES, tm, LANES), F32)] * n_scr,
        in_specs=[tok(D_MODEL), tok(MLA_WIDTH)] + part_specs + [tok(N_BRANCH * D_MODEL),
                  _const_spec((MLA_WIDTH, D_MODEL)), _const_spec((DIL_WIDTH, D_MODEL)),
                  _const_spec((D_MODEL, D_MODEL)), _const_spec((1, D_MODEL))],
        out_specs=tok(D_MODEL),
        out_shape=jax.ShapeDtypeStruct((b, s, D_MODEL), F32),
        compiler_params=pltpu.CompilerParams(dimension_semantics=("arbitrary", "arbitrary"),
                                             vmem_limit_bytes=VMEM_LIMIT),
    )(h1, oa, *flat, gates, w_branch_a.astype(BF16), w_branch_b.astype(BF16), w_out.astype(BF16),
      mix_post_g.reshape(1, D_MODEL))


def kernel(x, positions, ffn1_pre_g, ffn1_post_g, ffn1_w_gate, ffn1_w_up, ffn1_w_down, mix_pre_g, w_in, b_gate,
           q_norm_g, w_uq, kv_norm_g, w_uk, w_uv, w_branch_a, w_branch_b, w_out, mix_post_g,
           ffn2_pre_g, ffn2_post_g, ffn2_w_gate, ffn2_w_up, ffn2_w_down):
    b, s, d = x.shape
    depth = ffn1_pre_g.shape[0]
    assert all(win // (2 * dil) == DIL_HALF and s % (dil * DIL_SUB) == 0 for win, dil in DIL_PATTERNS)
    h = x
    for l in range(depth):
        h = _ffn(h.reshape(b * s, d), ffn1_pre_g[l], ffn1_post_g[l], ffn1_w_gate[l], ffn1_w_up[l],
                 ffn1_w_down[l]).reshape(b, s, d)
        q, k, vt, gates, dil_qkv = _mixer_in(h, positions, mix_pre_g[l], w_in[l], b_gate[l], q_norm_g[l],
                                             w_uq[l], kv_norm_g[l], w_uk[l], w_uv[l])
        oa = _mla_flash(q, k, vt)
        parts = [_dilated(*qkv, dil) for qkv, (_, dil) in zip(dil_qkv, DIL_PATTERNS)]
        h = _mixer_out(h, oa, parts, gates, w_branch_a[l], w_branch_b[l], w_out[l], mix_post_g[l])
        h = _ffn(h.reshape(b * s, d), ffn2_pre_g[l], ffn2_post_g[l], ffn2_w_gate[l], ffn2_w_up[l],
                 ffn2_w_down[l]).reshape(b, s, d)
    return h
```

```python
import functools

import jax
import jax.numpy as jnp
import numpy as np
from jax import lax
from jax.experimental import pallas as pl
from jax.experimental.pallas import tpu as pltpu

F32 = jnp.float32
BF16 = jnp.bfloat16

D_MODEL = 1024
D_FF = 2816
EPS = 1e-6
MLA_HEADS = 8
MLA_Q_RANK = 384
MLA_KV_RANK = 256
MLA_NOPE = 64
MLA_ROPE = 32
MLA_V = 64
MLA_THETA = 10000.0
MLA_WIDTH = MLA_HEADS * MLA_V
DIL_HEADS = 8
DIL_HEAD_DIM = 64
DIL_PATTERNS = ((128, 1), (512, 4), (2048, 16))
DIL_WIDTH = DIL_HEADS * DIL_HEAD_DIM
ROPE_THETA = 500000.0
ROPE_DIM = DIL_HEAD_DIM // 4
N_BRANCH = 2
NEG = -1e30

LANES = 128
HEAD_PAD = LANES
VMEM_LIMIT = 56 * 1024 * 1024

FFN_TM = 1024
FFN_FC = 256
MIX_TM = 512
FLASH_TQ = 256
FLASH_TK = MIX_TM
FLASH_TS = 512
VT_ROWS = MLA_V + 16
LOG2E = 1.4426950408889634
DIL_LB = 2048
DIL_SUB = 256
DIL_GROUP = 4
DIL_HALF = 64


def _rms(x, g):
    ms = jnp.mean(x * x, axis=-1, keepdims=True)
    return x * lax.rsqrt(ms + EPS) * g


def _const_spec(shape):
    nd = len(shape)
    return pl.BlockSpec(shape, lambda *_: (0,) * nd, pipeline_mode=pl.Buffered(1))


def _ffn_kernel(x_ref, pre_ref, post_ref, wg_ref, wu_ref, wd_ref, o_ref, hm_ref):
    x = x_ref[...]
    xn = _rms(x, pre_ref[...]).astype(BF16)
    for c0 in range(0, D_FF, FFN_FC):
        sl = slice(c0, min(c0 + FFN_FC, D_FF))
        g = jnp.dot(xn, wg_ref[:, sl], preferred_element_type=F32)
        u = jnp.dot(xn, wu_ref[:, sl], preferred_element_type=F32)
        hm_ref[:, sl] = (g * jax.nn.sigmoid(g) * u).astype(BF16)
    f = jnp.dot(hm_ref[...], wd_ref[...], preferred_element_type=F32)
    o_ref[...] = x + 0.5 * _rms(f, post_ref[...])


def _ffn(h, pre_g, post_g, w_gate, w_up, w_down):
    t = h.shape[0]
    return pl.pallas_call(
        _ffn_kernel,
        name="ffn",
        grid=(t // FFN_TM,),
        in_specs=[
            pl.BlockSpec((FFN_TM, D_MODEL), lambda i: (i, 0)),
            _const_spec((1, D_MODEL)),
            _const_spec((1, D_MODEL)),
            _const_spec((D_MODEL, D_FF)),
            _const_spec((D_MODEL, D_FF)),
            _const_spec((D_FF, D_MODEL)),
        ],
        out_specs=pl.BlockSpec((FFN_TM, D_MODEL), lambda i: (i, 0)),
        out_shape=jax.ShapeDtypeStruct((t, D_MODEL), F32),
        scratch_shapes=[pltpu.VMEM((FFN_TM, D_FF), BF16)],
        compiler_params=pltpu.CompilerParams(dimension_semantics=("arbitrary",), vmem_limit_bytes=VMEM_LIMIT),
    )(h, pre_g.reshape(1, D_MODEL), post_g.reshape(1, D_MODEL),
      w_gate.astype(BF16), w_up.astype(BF16), w_down.astype(BF16))


def _rope_tile(x, cos, sin):
    return x * cos + pltpu.roll(x, LANES // 2, 1) * sin


def _mixer_in_kernel(h_ref, pos_ref, pre_ref, wa_ref, wd_ref, wg_ref, bg_ref, qg_ref, kvg_ref,
                     wuq_ref, wuk_ref, wuvt_ref, tab_ref,
                     q_ref, k_ref, vt_ref, gate_ref, *rest):
    dil_refs, dscr_ref = rest[:-1], rest[-1]
    u = _rms(h_ref[0], pre_ref[...]).astype(BF16)
    pos = pos_ref[0]
    ang = pos * tab_ref[0:1, :]
    cc = jnp.cos(ang)
    ss = jnp.sin(ang)
    rcc = pltpu.roll(cc, LANES // 2, 1)
    rss = pltpu.roll(ss, LANES // 2, 1)
    low = lax.broadcasted_iota(jnp.int32, ang.shape, 1) < LANES // 2
    cos_a = jnp.where(low, cc, rcc)
    sin_a = jnp.where(low, -ss, rss)
    cos_b = jnp.where(low, rcc, cc)
    sin_b = jnp.where(low, -rss, ss)

    pa = jnp.dot(u, wa_ref[...], preferred_element_type=F32)
    cq = pa[:, :MLA_Q_RANK]
    ckv = pa[:, MLA_Q_RANK:MLA_Q_RANK + MLA_KV_RANK]
    kr = _rope_tile(pa[:, MLA_Q_RANK + MLA_KV_RANK:], cos_a, sin_a)
    qn = _rms(cq, qg_ref[...]).astype(BF16)
    ckvn = _rms(ckv, kvg_ref[...]).astype(BF16)
    q = jnp.dot(qn, wuq_ref[...], preferred_element_type=F32)
    kn = jnp.dot(ckvn, wuk_ref[...], preferred_element_type=F32)
    scale = LOG2E * (MLA_NOPE + MLA_ROPE) ** -0.5
    for h in range(MLA_HEADS):
        sl = slice(HEAD_PAD * h, HEAD_PAD * (h + 1))
        qh = _rope_tile(q[:, sl], cos_a, sin_a)
        q_ref[0, :, sl] = (qh * scale).astype(BF16)
        k_ref[0, :, sl] = (kn[:, sl] + kr).astype(BF16)
    vt = lax.dot_general(wuvt_ref[...], ckvn, (((1,), (1,)), ((), ())), preferred_element_type=F32)
    row = lax.broadcasted_iota(jnp.int32, vt.shape, 0)
    vt_ref[0, 0] = jnp.where(row % VT_ROWS >= MLA_V, 1.0, vt).astype(BF16)

    pd = jnp.dot(u, wd_ref[...], preferred_element_type=F32)
    dscale = LOG2E * DIL_HEAD_DIM ** -0.5
    nc = DIL_WIDTH // LANES
    for c in range(nc):
        sl = slice(LANES * c, LANES * (c + 1))
        ks = slice(DIL_WIDTH + LANES * c, DIL_WIDTH + LANES * (c + 1))
        dscr_ref[c] = _rope_tile(pd[:, sl], cos_b, sin_b) * dscale
        dscr_ref[nc + c] = _rope_tile(pd[:, ks], cos_b, sin_b)
        dscr_ref[2 * nc + c] = pd[:, 2 * DIL_WIDTH + LANES * c:2 * DIL_WIDTH + LANES * (c + 1)]
    tm = dscr_ref.shape[1]
    for pi, (_, dil) in enumerate(DIL_PATTERNS):
        for t in range(3):
            for c in range(nc):
                if dil == 1:
                    dil_refs[3 * pi + t][0, 0, :, LANES * c:LANES * (c + 1)] = dscr_ref[nc * t + c].astype(BF16)
                    continue
                for r in range(dil):
                    rows = dscr_ref[nc * t + c, pl.ds(r, tm // dil, stride=dil), :]
                    dil_refs[3 * pi + t][0, r, :, LANES * c:LANES * (c + 1)] = rows.astype(BF16)

    pg = jnp.dot(u, wg_ref[...], preferred_element_type=F32) + bg_ref[...]
    gate_ref[0] = jax.nn.sigmoid(pg).astype(BF16)


HALF_A = MLA_ROPE // 2
HALF_B = ROPE_DIM // 2
NOPE_LO = LANES // 2 - HALF_A


def _rope_table():
    inv_a = 1.0 / (jnp.float32(MLA_THETA) ** (jnp.arange(HALF_A, dtype=F32) / HALF_A))
    inv_b = 1.0 / (jnp.float32(ROPE_THETA) ** (jnp.arange(HALF_B, dtype=F32) / HALF_B))
    z = lambda n: jnp.zeros((n,), F32)
    row = jnp.concatenate([inv_a, z(LANES // 2 - HALF_A), inv_b, inv_b, z(LANES // 2 - 2 * HALF_B)])
    return jnp.zeros((8, LANES), F32).at[0].set(row)


def _mla_tile_cols(nope, rope):
    lead = (nope if nope is not None else rope).shape[:-1]
    zeros = lambda n: jnp.zeros(lead + (n,), F32)
    x1, x2 = (rope[..., :HALF_A], rope[..., HALF_A:]) if rope is not None else (zeros(HALF_A), zeros(HALF_A))
    n_lo, n_hi = (nope[..., :NOPE_LO], nope[..., NOPE_LO:]) if nope is not None else (zeros(NOPE_LO), zeros(MLA_NOPE - NOPE_LO))
    return jnp.concatenate([x1, n_lo, x2, n_hi, zeros(HEAD_PAD - MLA_NOPE - MLA_ROPE)], axis=-1)


def _dil_tile_cols(w):
    wh = w.reshape(w.shape[0], DIL_HEADS // 2, 2, DIL_HEAD_DIM)
    a, b = wh[:, :, 0], wh[:, :, 1]
    tile = jnp.concatenate([a[..., :HALF_B], b[..., :HALF_B], a[..., ROPE_DIM:], a[..., HALF_B:ROPE_DIM],
                            b[..., HALF_B:ROPE_DIM], b[..., ROPE_DIM:]], axis=-1)
    return tile.reshape(w.shape[0], DIL_WIDTH)


def _mixer_in(h1, positions, mix_pre_g, w_in, b_gate, q_norm_g, w_uq, kv_norm_g, w_uk, w_uv):
    b, s, _ = h1.shape
    o0 = 0
    o1 = o0 + MLA_Q_RANK
    o2 = o1 + MLA_KV_RANK
    o3 = o2 + MLA_ROPE
    o4 = o3 + 3 * DIL_WIDTH
    w_kr = _mla_tile_cols(None, w_in[:, o2:o3])
    wa = jnp.concatenate([w_in[:, o0:o2], w_kr], axis=1).astype(BF16)
    wd = jnp.concatenate([_dil_tile_cols(w_in[:, o3:o3 + DIL_WIDTH]),
                          _dil_tile_cols(w_in[:, o3 + DIL_WIDTH:o3 + 2 * DIL_WIDTH]),
                          w_in[:, o3 + 2 * DIL_WIDTH:o4]], axis=1).astype(BF16)
    wg = w_in[:, o4:].astype(BF16)
    wuq_h = w_uq.reshape(MLA_Q_RANK, MLA_HEADS, MLA_NOPE + MLA_ROPE)
    wuq = _mla_tile_cols(wuq_h[..., :MLA_NOPE], wuq_h[..., MLA_NOPE:]).reshape(MLA_Q_RANK, -1).astype(BF16)
    wuk = _mla_tile_cols(w_uk.reshape(MLA_KV_RANK, MLA_HEADS, MLA_NOPE), None).reshape(MLA_KV_RANK, -1).astype(BF16)
    wuvt = jnp.pad(w_uv.T.reshape(MLA_HEADS, MLA_V, MLA_KV_RANK), ((0, 0), (0, VT_ROWS - MLA_V), (0, 0))
                   ).reshape(MLA_HEADS * VT_ROWS, MLA_KV_RANK).astype(BF16)
    pos = positions.astype(F32).reshape(b, s, 1)
    tm = MIX_TM
    hp = MLA_HEADS * HEAD_PAD
    tok = lambda c: pl.BlockSpec((1, tm, c), lambda bi, i: (bi, i, 0))
    out_shapes = (
        jax.ShapeDtypeStruct((b, s, hp), BF16),
        jax.ShapeDtypeStruct((b, s, hp), BF16),
        jax.ShapeDtypeStruct((b, s // FLASH_TK, MLA_HEADS * VT_ROWS, FLASH_TK), BF16),
        jax.ShapeDtypeStruct((b, s, N_BRANCH * D_MODEL), BF16),
    )
    per_chunk = FLASH_TK // tm
    out_specs = (
        tok(hp), tok(hp),
        pl.BlockSpec((1, 1, MLA_HEADS * VT_ROWS, tm), lambda bi, i: (bi, i // per_chunk, 0, i % per_chunk)),
        tok(N_BRANCH * D_MODEL),
    )
    for _, dil in DIL_PATTERNS:
        out_shapes += (jax.ShapeDtypeStruct((b, dil, s // dil, DIL_WIDTH), BF16),) * 3
        out_specs += (pl.BlockSpec((1, dil, tm // dil, DIL_WIDTH), lambda bi, i: (bi, 0, i, 0)),) * 3
    outs = pl.pallas_call(
        _mixer_in_kernel,
        name="mixer_in",
        grid=(b, s // tm),
        in_specs=[
            tok(D_MODEL), tok(1),
            _const_spec((1, D_MODEL)),
            _const_spec(wa.shape), _const_spec(wd.shape), _const_spec(wg.shape),
            _const_spec((1, N_BRANCH * D_MODEL)),
            _const_spec((1, MLA_Q_RANK)), _const_spec((1, MLA_KV_RANK)),
            _const_spec(wuq.shape), _const_spec(wuk.shape), _const_spec(wuvt.shape),
            _const_spec((8, LANES)),
        ],
        out_specs=out_specs,
        out_shape=out_shapes,
        scratch_shapes=[pltpu.VMEM((3 * DIL_WIDTH // LANES, tm, LANES), F32)],
        compiler_params=pltpu.CompilerParams(dimension_semantics=("arbitrary", "arbitrary"),
                                             vmem_limit_bytes=VMEM_LIMIT),
    )(h1, pos, mix_pre_g.reshape(1, D_MODEL), wa, wd, wg, b_gate.reshape(1, -1),
      q_norm_g.reshape(1, -1), kv_norm_g.reshape(1, -1), wuq, wuk, wuvt, _rope_table())
    q, k, vt, gates = outs[:4]
    dil_qkv = [outs[4 + 3 * i:7 + 3 * i] for i in range(len(DIL_PATTERNS))]
    return q, k, vt, gates, dil_qkv


def _flash_kernel(q_ref, k_ref, vt_ref, o_ref, s0_ref, s1_ref, s2_ref, s3_ref, qt_ref):
    nk = k_ref.shape[1] // FLASH_TK
    nq = q_ref.shape[1] // FLASH_TQ
    assert nk % 4 == 0
    bufs = (s0_ref, s1_ref, s2_ref, s3_ref)

    def scores(t, c, hh, buf):
        cmax = None
        for part in range(FLASH_TK // FLASH_TS):
            k0 = c * FLASH_TK + part * FLASH_TS
            if not isinstance(k0, int):
                k0 = pl.multiple_of(k0, FLASH_TS)
            k = k_ref[0, pl.ds(k0, FLASH_TS), HEAD_PAD * hh:HEAD_PAD * (hh + 1)]
            st = jnp.dot(k, qt_ref[hh, t], preferred_element_type=F32)
            buf[hh, FLASH_TS * part:FLASH_TS * (part + 1)] = st
            pm = jnp.max(st, axis=0, keepdims=True)
            cmax = pm if cmax is None else jnp.maximum(cmax, pm)
        return cmax

    def consume(c, hh, buf, m, cmax, acc):
        m_new = jnp.maximum(m, cmax)
        alpha = jnp.exp2(m - m_new)
        p = jnp.exp2((buf[hh] - m_new).astype(BF16))
        vt = vt_ref[0, c, VT_ROWS * hh:VT_ROWS * (hh + 1), :]
        return m_new, alpha * acc + jnp.dot(vt, p, preferred_element_type=F32)

    def half(c, state, pair_in, pair_out, t_next, c_next):
        new = []
        for hh in range(2):
            m, cm0, cm1, acc = state[hh]
            nxt = [scores(t_next, c_next + i, hh, bufs[pair_out + i]) for i in range(2)]
            m, acc = consume(c, hh, bufs[pair_in], m, cm0, acc)
            m, acc = consume(c + 1, hh, bufs[pair_in + 1], m, cm1, acc)
            new.append((m, nxt[0], nxt[1], acc))
        return tuple(new)

    def tile(t, cmaxes):
        state = tuple((jnp.full((1, FLASH_TQ), -jnp.inf, F32), cmaxes[hh][0], cmaxes[hh][1],
                       jnp.zeros((VT_ROWS, FLASH_TQ), F32)) for hh in range(2))

        def group(j, st):
            c = 4 * j
            return half(c + 2, half(c, st, 0, 2, t, c + 2), 2, 0, t, c + 4)

        state = lax.fori_loop(0, nk // 4 - 1, group, state)
        t_next =jnp.minimum(t + 1, nq - 1)
        state = half(nk - 2, half(nk - 4, state, 0, 2, t, nk - 2), 2, 0, t_next, 0)
        outs = [acc[:MLA_V] / acc[MLA_V:MLA_V + 1] for (_, _, _, acc) in state]
        o_ref[0, pl.ds(pl.multiple_of(t * FLASH_TQ, FLASH_TQ), FLASH_TQ), :] = (
            jnp.concatenate(outs, axis=0).T.astype(BF16))
        return tuple((cm0, cm1) for (_, cm0, cm1, _) in state)

    def transpose_q(t, carry):
        for hh in range(2):
            q = q_ref[0, pl.ds(pl.multiple_of(t * FLASH_TQ, FLASH_TQ), FLASH_TQ), HEAD_PAD * hh:HEAD_PAD * (hh + 1)]
            qt_ref[hh, t] = q.astype(F32).T.astype(BF16)
        return carry

    lax.fori_loop(0, nq, transpose_q, 0)
    first = tuple((scores(0, 0, hh, bufs[0]), scores(0, 1, hh, bufs[1])) for hh in range(2))
    lax.fori_loop(0, nq, tile, first)


def _mla_flash(q, k, vt):
    b, s, _ = q.shape
    nk = s // FLASH_TK
    return pl.pallas_call(
        _flash_kernel,
        name="mla_flash",
        grid=(b, MLA_HEADS // 2),
        in_specs=[
            pl.BlockSpec((1, s, 2 * HEAD_PAD), lambda bi, hp: (bi, 0, hp)),
            pl.BlockSpec((1, s, 2 * HEAD_PAD), lambda bi, hp: (bi, 0, hp)),
            pl.BlockSpec((1, nk, 2 * VT_ROWS, FLASH_TK), lambda bi, hp: (bi, 0, hp, 0)),
        ],
        out_specs=pl.BlockSpec((1, s, 2 * MLA_V), lambda bi, hp: (bi, 0, hp)),
        out_shape=jax.ShapeDtypeStruct((b, s, MLA_WIDTH), BF16),
        scratch_shapes=[pltpu.VMEM((2, FLASH_TK, FLASH_TQ), F32)] * 4
                       + [pltpu.VMEM((2, s // FLASH_TQ, HEAD_PAD, FLASH_TQ), BF16)],
        compiler_params=pltpu.CompilerParams(dimension_semantics=("arbitrary", "arbitrary"),
                                             vmem_limit_bytes=VMEM_LIMIT),
    )(q, k, vt)


def _dilated_kernel(q_ref, kc_ref, kp_ref, kn_ref, vc_ref, vp_ref, vn_ref, o_ref, lse_ref,
                    kw_ref, vt_ref, ot_ref, lt_ref, *sbufs, n_rows):
    lb = q_ref.shape[2]
    l0 = pl.program_id(2) * lb
    wrows = lb + 2 * DIL_HALF
    kw_ref[0:DIL_HALF] = kp_ref[0, 0]
    kw_ref[DIL_HALF:DIL_HALF + lb] = kc_ref[0, 0]
    kw_ref[DIL_HALF + lb:] = kn_ref[0, 0]
    vw = jnp.concatenate([vp_ref[0, 0], vc_ref[0, 0], vn_ref[0, 0]], axis=0).astype(F32)
    vt = vw.T.astype(BF16)
    ones = jnp.ones((VT_ROWS - DIL_HEAD_DIM, wrows), BF16)
    for h in range(DIL_HEADS):
        vt_ref[VT_ROWS * h:VT_ROWS * h + DIL_HEAD_DIM] = vt[DIL_HEAD_DIM * h:DIL_HEAD_DIM * (h + 1)]
        vt_ref[VT_ROWS * h + DIL_HEAD_DIM:VT_ROWS * (h + 1)] = ones

    win = DIL_SUB + 2 * DIL_HALF
    krow = lax.broadcasted_iota(jnp.int32, (win, DIL_SUB), 0)
    qcol = lax.broadcasted_iota(jnp.int32, (win, DIL_SUB), 1)
    in_band = (krow - qcol).astype(jnp.uint32) <= 2 * DIL_HALF
    lane = lax.broadcasted_iota(jnp.int32, (DIL_SUB, LANES), 1)
    odd_lanes = ((lane >= HALF_B) & (lane < ROPE_DIM)) | (lane >= LANES // 2 + HALF_B)
    nbuf = len(sbufs)

    def produce(u):
        j, h = divmod(u, DIL_HEADS)
        ls = slice(LANES * (h // 2), LANES * (h // 2 + 1))
        kpair = kw_ref[j * DIL_SUB:j * DIL_SUB + win, ls]
        qpair = q_ref[0, 0, j * DIL_SUB:(j + 1) * DIL_SUB, ls]
        qh = jnp.where(odd_lanes if h % 2 else ~odd_lanes, qpair, jnp.zeros_like(qpair))
        st = lax.dot_general(kpair, qh, (((1,), (1,)), ((), ())), preferred_element_type=F32)
        kidx = l0 - DIL_HALF + j * DIL_SUB + krow
        st = jnp.where(in_band & (kidx.astype(jnp.uint32) < n_rows), st, NEG)
        sbufs[u % nbuf][...] = st
        return jnp.max(st, axis=0, keepdims=True)

    def consume(u, m):
        j, h = divmod(u, DIL_HEADS)
        p = jnp.exp2((sbufs[u % nbuf][...] - m).astype(BF16))
        r = jnp.dot(vt_ref[VT_ROWS * h:VT_ROWS * (h + 1), j * DIL_SUB:j * DIL_SUB + win], p,
                    preferred_element_type=F32)
        den = r[DIL_HEAD_DIM:DIL_HEAD_DIM + 1]
        hs = slice(DIL_HEAD_DIM * h, DIL_HEAD_DIM * (h + 1))
        ot_ref[hs] = r[:DIL_HEAD_DIM] / den
        lt_ref[hs] = jnp.broadcast_to(m + jnp.log2(den), (DIL_HEAD_DIM, DIL_SUB))
        if h == DIL_HEADS - 1:
            o_ref[0, 0, j * DIL_SUB:(j + 1) * DIL_SUB, :] = ot_ref[...].T.astype(BF16)
            lse_ref[0, 0, j * DIL_SUB:(j + 1) * DIL_SUB, :] = lt_ref[...].T

    n_units = (lb // DIL_SUB) * DIL_HEADS
    grp = nbuf // 2
    m_cur = [produce(i) for i in range(grp)]
    for u in range(0, n_units, grp):
        m_next = [produce(u + grp + i) for i in range(grp)] if u + grp < n_units else None
        for i in range(grp):
            consume(u + i, m_cur[i])
        m_cur = m_next


def _dilated(dq, dk, dv, dilation):
    b, _, n_rows, w = dq.shape
    lb = min(DIL_LB, n_rows)
    hb = lb // DIL_HALF
    last = n_rows // DIL_HALF - 1
    cur = pl.BlockSpec((1, 1, lb, w), lambda bi, r, l: (bi, r, l, 0))
    prev = pl.BlockSpec((1, 1, DIL_HALF, w), lambda bi, r, l: (bi, r, jnp.maximum(l * hb - 1, 0), 0))
    nxt = pl.BlockSpec((1, 1, DIL_HALF, w), lambda bi, r, l: (bi, r, jnp.minimum((l + 1) * hb, last), 0))
    return pl.pallas_call(
        functools.partial(_dilated_kernel, n_rows=n_rows),
        name=f"dilated_{dilation}",
        grid=(b, dilation, n_rows // lb),
        in_specs=[cur, cur, prev, nxt, cur, prev, nxt],
        out_specs=(cur, cur),
        out_shape=(jax.ShapeDtypeStruct((b, dilation, n_rows, w), BF16),
                   jax.ShapeDtypeStruct((b, dilation, n_rows, w), F32)),
        scratch_shapes=[pltpu.VMEM((lb + 2 * DIL_HALF, w), BF16),
                        pltpu.VMEM((DIL_HEADS * VT_ROWS, lb + 2 * DIL_HALF), BF16),
                        pltpu.VMEM((w, DIL_SUB), F32),
                        pltpu.VMEM((w, DIL_SUB), F32)]
                       + [pltpu.VMEM((DIL_SUB + 2 * DIL_HALF, DIL_SUB), F32)] * (2 * DIL_GROUP),
        compiler_params=pltpu.CompilerParams(dimension_semantics=("arbitrary",) * 3, vmem_limit_bytes=VMEM_LIMIT),
    )(dq, dk, dk, dk, dv, dv, dv)


def _mixer_out_kernel(h_ref, oa_ref, o1_ref, l1_ref, o2_ref, l2_ref, o3_ref, l3_ref, gate_ref,
                      wba_ref, wbb_ref, wo_ref, post_ref, out_ref, *scr):
    tm = h_ref.shape[1]

    def token_major(ref, dil, buf):
        if dil == 1:
            return ref[0, 0].astype(F32)
        nc = DIL_WIDTH // LANES
        for r in range(dil):
            for c in range(nc):
                buf[c, pl.ds(r, tm // dil, stride=dil), :] = ref[0, r, :, LANES * c:LANES * (c + 1)].astype(F32)
        return jnp.concatenate([buf[c] for c in range(nc)], axis=1)

    bufs = iter(scr)
    os_, ls_ = [], []
    for (o_r, l_r), (_, dil) in zip(((o1_ref, l1_ref), (o2_ref, l2_ref), (o3_ref, l3_ref)), DIL_PATTERNS):
        os_.append(token_major(o_r, dil, None if dil == 1 else next(bufs)))
        ls_.append(token_major(l_r, dil, None if dil == 1 else next(bufs)))
    lm = jnp.maximum(jnp.maximum(ls_[0], ls_[1]), ls_[2])
    es = [jnp.exp2(l - lm) for l in ls_]
    ob = (es[0] * os_[0] + es[1] * os_[1] + es[2] * os_[2]) / (es[0] + es[1] + es[2])
    ya =jnp.dot(oa_ref[0], wba_ref[...], preferred_element_type=F32)
    yb = jnp.dot(ob.astype(BF16), wbb_ref[...], preferred_element_type=F32)
    gate = gate_ref[0].astype(F32)
    merged = gate[:, :D_MODEL] * ya + gate[:, D_MODEL:] * yb
    mix = jnp.dot(merged.astype(BF16), wo_ref[...], preferred_element_type=F32)
    out_ref[0] = h_ref[0] + _rms(mix, post_ref[...])


def _mixer_out(h1, oa, parts, gates, w_branch_a, w_branch_b, w_out, mix_post_g):
    b, s, _ = h1.shape
    tm = MIX_TM
    tok = lambda c: pl.BlockSpec((1, tm, c), lambda bi, i: (bi, i, 0))
    flat, part_specs, n_scr = [], [], 0
    for (_, dil), pair in zip(DIL_PATTERNS, parts):
        flat += list(pair)
        part_specs += [pl.BlockSpec((1, dil, tm // dil, DIL_WIDTH), lambda bi, i: (bi, 0, i, 0))] * 2
        n_scr += 2 * (dil > 1)
    return pl.pallas_call(
        _mixer_out_kernel,
        name="mixer_out",
        grid=(b, s // tm),
        scratch_shapes=[pltpu.VMEM((DIL_WIDTH // LANES, tm, LANES), F32)] * n_scr,
        in_specs=[tok(D_MODEL), tok(MLA_WIDTH)] + part_specs + [tok(N_BRANCH * D_MODEL),
                  _const_spec((MLA_WIDTH, D_MODEL)), _const_spec((DIL_WIDTH, D_MODEL)),
                  _const_spec((D_MODEL, D_MODEL)), _const_spec((1, D_MODEL))],
        out_specs=tok(D_MODEL),
        out_shape=jax.ShapeDtypeStruct((b, s, D_MODEL), F32),
        compiler_params=pltpu.CompilerParams(dimension_semantics=("arbitrary", "arbitrary"),
                                             vmem_limit_bytes=VMEM_LIMIT),
    )(h1, oa, *flat, gates, w_branch_a.astype(BF16), w_branch_b.astype(BF16), w_out.astype(BF16),
      mix_post_g.reshape(1, D_MODEL))


def kernel(x, positions, ffn1_pre_g, ffn1_post_g, ffn1_w_gate, ffn1_w_up, ffn1_w_down, mix_pre_g, w_in, b_gate,
           q_norm_g, w_uq, kv_norm_g, w_uk, w_uv, w_branch_a, w_branch_b, w_out, mix_post_g,
           ffn2_pre_g, ffn2_post_g, ffn2_w_gate, ffn2_w_up, ffn2_w_down):
    b, s, d = x.shape
    depth = ffn1_pre_g.shape[0]
    assert all(win // (2 * dil) == DIL_HALF and s % (dil * DIL_SUB) == 0 for win, dil in DIL_PATTERNS)
    h = x
    for l in range(depth):
        h = _ffn(h.reshape(b * s, d), ffn1_pre_g[l], ffn1_post_g[l], ffn1_w_gate[l], ffn1_w_up[l],
                 ffn1_w_down[l]).reshape(b, s, d)
        q, k, vt, gates, dil_qkv = _mixer_in(h, positions, mix_pre_g[l], w_in[l], b_gate[l], q_norm_g[l],
                                             w_uq[l], kv_norm_g[l], w_uk[l], w_uv[l])
        oa = _mla_flash(q, k, vt)
        parts = [_dilated(*qkv, dil) for qkv, (_, dil) in zip(dil_qkv, DIL_PATTERNS)]
        h = _mixer_out(h, oa, parts, gates, w_branch_a[l], w_branch_b[l], w_out[l], mix_post_g[l])
        h = _ffn(h.reshape(b * s, d), ffn2_pre_g[l], ffn2_post_g[l], ffn2_w_gate[l], ffn2_w_up[l],
                 ffn2_w_down[l]).reshape(b, s, d)
    return h
```

```python
import functools

import jax
import jax.numpy as jnp
import numpy as np
from jax import lax
from jax.experimental import pallas as pl
from jax.experimental.pallas import tpu as pltpu

F32 = jnp.float32
BF16 = jnp.bfloat16

D_MODEL = 1024
D_FF = 2816
EPS = 1e-6
MLA_HEADS = 8
MLA_Q_RANK = 384
MLA_KV_RANK = 256
MLA_NOPE = 64
MLA_ROPE = 32
MLA_V = 64
MLA_THETA = 10000.0
MLA_WIDTH = MLA_HEADS * MLA_V
DIL_HEADS = 8
DIL_HEAD_DIM = 64
DIL_PATTERNS = ((128, 1), (512, 4), (2048, 16))
DIL_WIDTH = DIL_HEADS * DIL_HEAD_DIM
ROPE_THETA = 500000.0
ROPE_DIM = DIL_HEAD_DIM // 4
N_BRANCH = 2
NEG = -1e30

LANES = 128
HEAD_PAD = LANES
VMEM_LIMIT = 56 * 1024 * 1024

FFN_TM = 1024
FFN_FC = 256
MIX_TM = 512
FLASH_TQ = 256
FLASH_TK = MIX_TM
FLASH_TS = 512
VT_ROWS = MLA_V + 16
LOG2E = 1.4426950408889634
DIL_LB = 2048
DIL_SUB = 256
DIL_GROUP = 4
DIL_HALF = 64


def _rms(x, g):
    ms = jnp.mean(x * x, axis=-1, keepdims=True)
    return x * lax.rsqrt(ms + EPS) * g


def _const_spec(shape):
    nd = len(shape)
    return pl.BlockSpec(shape, lambda *_: (0,) * nd, pipeline_mode=pl.Buffered(1))


def _ffn_kernel(x_ref, pre_ref, post_ref, wg_ref, wu_ref, wd_ref, o_ref, hm_ref):
    x = x_ref[...]
    xn = _rms(x, pre_ref[...]).astype(BF16)
    for c0 in range(0, D_FF, FFN_FC):
        sl = slice(c0, min(c0 + FFN_FC, D_FF))
        g = jnp.dot(xn, wg_ref[:, sl], preferred_element_type=F32)
        u = jnp.dot(xn, wu_ref[:, sl], preferred_element_type=F32)
        hm_ref[:, sl] = (g * jax.nn.sigmoid(g) * u).astype(BF16)
    f = jnp.dot(hm_ref[...], wd_ref[...], preferred_element_type=F32)
    o_ref[...] = x + 0.5 * _rms(f, post_ref[...])


def _ffn(h, pre_g, post_g, w_gate, w_up, w_down):
    t = h.shape[0]
    return pl.pallas_call(
        _ffn_kernel,
        name="ffn",
        grid=(t // FFN_TM,),
        in_specs=[
            pl.BlockSpec((FFN_TM, D_MODEL), lambda i: (i, 0)),
            _const_spec((1, D_MODEL)),
            _const_spec((1, D_MODEL)),
            _const_spec((D_MODEL, D_FF)),
            _const_spec((D_MODEL, D_FF)),
            _const_spec((D_FF, D_MODEL)),
        ],
        out_specs=pl.BlockSpec((FFN_TM, D_MODEL), lambda i: (i, 0)),
        out_shape=jax.ShapeDtypeStruct((t, D_MODEL), F32),
        scratch_shapes=[pltpu.VMEM((FFN_TM, D_FF), BF16)],
        compiler_params=pltpu.CompilerParams(dimension_semantics=("arbitrary",), vmem_limit_bytes=VMEM_LIMIT),
    )(h, pre_g.reshape(1, D_MODEL), post_g.reshape(1, D_MODEL),
      w_gate.astype(BF16), w_up.astype(BF16), w_down.astype(BF16))


def _rope_tile(x, cos, sin):
    return x * cos + pltpu.roll(x, LANES // 2, 1) * sin


def _mixer_in_kernel(h_ref, pos_ref, pre_ref, wa_ref, wd_ref, wg_ref, bg_ref, qg_ref, kvg_ref,
                     wuq_ref, wuk_ref, wuvt_ref, tab_ref,
                     q_ref, k_ref, vt_ref, gate_ref, *rest):
    dil_refs, dscr_ref = rest[:-1], rest[-1]
    u = _rms(h_ref[0], pre_ref[...]).astype(BF16)
    pos = pos_ref[0]
    ang = pos * tab_ref[0:1, :]
    cc = jnp.cos(ang)
    ss = jnp.sin(ang)
    rcc = pltpu.roll(cc, LANES // 2, 1)
    rss = pltpu.roll(ss, LANES // 2, 1)
    low = lax.broadcasted_iota(jnp.int32, ang.shape, 1) < LANES // 2
    cos_a = jnp.where(low, cc, rcc)
    sin_a = jnp.where(low, -ss, rss)
    cos_b = jnp.where(low, rcc, cc)
    sin_b = jnp.where(low, -rss, ss)

    pa = jnp.dot(u, wa_ref[...], preferred_element_type=F32)
    cq = pa[:, :MLA_Q_RANK]
    ckv = pa[:, MLA_Q_RANK:MLA_Q_RANK + MLA_KV_RANK]
    kr = _rope_tile(pa[:, MLA_Q_RANK + MLA_KV_RANK:], cos_a, sin_a)
    qn = _rms(cq, qg_ref[...]).astype(BF16)
    ckvn = _rms(ckv, kvg_ref[...]).astype(BF16)
    q = jnp.dot(qn, wuq_ref[...], preferred_element_type=F32)
    kn = jnp.dot(ckvn, wuk_ref[...], preferred_element_type=F32)
    scale = LOG2E * (MLA_NOPE + MLA_ROPE) ** -0.5
    for h in range(MLA_HEADS):
        sl = slice(HEAD_PAD * h, HEAD_PAD * (h + 1))
        qh = _rope_tile(q[:, sl], cos_a, sin_a)
        q_ref[0, :, sl] = (qh * scale).astype(BF16)
        k_ref[0, :, sl] = (kn[:, sl] + kr).astype(BF16)
    vt = lax.dot_general(wuvt_ref[...], ckvn, (((1,), (1,)), ((), ())), preferred_element_type=F32)
    row = lax.broadcasted_iota(jnp.int32, vt.shape, 0)
    vt_ref[0, 0] = jnp.where(row % VT_ROWS >= MLA_V, 1.0, vt).astype(BF16)

    pd = jnp.dot(u, wd_ref[...], preferred_element_type=F32)
    dscale = LOG2E * DIL_HEAD_DIM ** -0.5
    nc = DIL_WIDTH // LANES
    for c in range(nc):
        sl = slice(LANES * c, LANES * (c + 1))
        ks = slice(DIL_WIDTH + LANES * c, DIL_WIDTH + LANES * (c + 1))
        dscr_ref[c] = _rope_tile(pd[:, sl], cos_b, sin_b) * dscale
        dscr_ref[nc + c] = _rope_tile(pd[:, ks], cos_b, sin_b)
        dscr_ref[2 * nc + c] = pd[:, 2 * DIL_WIDTH + LANES * c:2 * DIL_WIDTH + LANES * (c + 1)]
    tm = dscr_ref.shape[1]
    for pi, (_, dil) in enumerate(DIL_PATTERNS):
        for t in range(3):
            for c in range(nc):
                if dil == 1:
                    dil_refs[3 * pi + t][0, 0, :, LANES * c:LANES * (c + 1)] = dscr_ref[nc * t + c].astype(BF16)
                    continue
                for r in range(dil):
                    rows = dscr_ref[nc * t + c, pl.ds(r, tm // dil, stride=dil), :]
                    dil_refs[3 * pi + t][0, r, :, LANES * c:LANES * (c + 1)] = rows.astype(BF16)

    pg = jnp.dot(u, wg_ref[...], preferred_element_type=F32) + bg_ref[...]
    gate_ref[0] = jax.nn.sigmoid(pg).astype(BF16)


HALF_A = MLA_ROPE // 2
HALF_B = ROPE_DIM // 2
NOPE_LO = LANES // 2 - HALF_A


def _rope_table():
    inv_a = 1.0 / (jnp.float32(MLA_THETA) ** (jnp.arange(HALF_A, dtype=F32) / HALF_A))
    inv_b = 1.0 / (jnp.float32(ROPE_THETA) ** (jnp.arange(HALF_B, dtype=F32) / HALF_B))
    z = lambda n: jnp.zeros((n,), F32)
    row = jnp.concatenate([inv_a, z(LANES // 2 - HALF_A), inv_b, inv_b, z(LANES // 2 - 2 * HALF_B)])
    return jnp.zeros((8, LANES), F32).at[0].set(row)


def _mla_tile_cols(nope, rope):
    lead = (nope if nope is not None else rope).shape[:-1]
    zeros = lambda n: jnp.zeros(lead + (n,), F32)
    x1, x2 = (rope[..., :HALF_A], rope[..., HALF_A:]) if rope is not None else (zeros(HALF_A), zeros(HALF_A))
    n_lo, n_hi = (nope[..., :NOPE_LO], nope[..., NOPE_LO:]) if nope is not None else (zeros(NOPE_LO), zeros(MLA_NOPE - NOPE_LO))
    return jnp.concatenate([x1, n_lo, x2, n_hi, zeros(HEAD_PAD - MLA_NOPE - MLA_ROPE)], axis=-1)


def _dil_tile_cols(w):
    wh = w.reshape(w.shape[0], DIL_HEADS // 2, 2, DIL_HEAD_DIM)
    a, b = wh[:, :, 0], wh[:, :, 1]
    tile = jnp.concatenate([a[..., :HALF_B], b[..., :HALF_B], a[..., ROPE_DIM:], a[..., HALF_B:ROPE_DIM],
                            b[..., HALF_B:ROPE_DIM], b[..., ROPE_DIM:]], axis=-1)
    return tile.reshape(w.shape[0], DIL_WIDTH)


def _mixer_in(h1, positions, mix_pre_g, w_in, b_gate, q_norm_g, w_uq, kv_norm_g, w_uk, w_uv):
    b, s, _ = h1.shape
    o0 = 0
    o1 = o0 + MLA_Q_RANK
    o2 = o1 + MLA_KV_RANK
    o3 = o2 + MLA_ROPE
    o4 = o3 + 3 * DIL_WIDTH
    w_kr = _mla_tile_cols(None, w_in[:, o2:o3])
    wa = jnp.concatenate([w_in[:, o0:o2], w_kr], axis=1).astype(BF16)
    wd = jnp.concatenate([_dil_tile_cols(w_in[:, o3:o3 + DIL_WIDTH]),
                          _dil_tile_cols(w_in[:, o3 + DIL_WIDTH:o3 + 2 * DIL_WIDTH]),
                          w_in[:, o3 + 2 * DIL_WIDTH:o4]], axis=1).astype(BF16)
    wg = w_in[:, o4:].astype(BF16)
    wuq_h = w_uq.reshape(MLA_Q_RANK, MLA_HEADS, MLA_NOPE + MLA_ROPE)
    wuq = _mla_tile_cols(wuq_h[..., :MLA_NOPE], wuq_h[..., MLA_NOPE:]).reshape(MLA_Q_RANK, -1).astype(BF16)
    wuk = _mla_tile_cols(w_uk.reshape(MLA_KV_RANK, MLA_HEADS, MLA_NOPE), None).reshape(MLA_KV_RANK, -1).astype(BF16)
    wuvt = jnp.pad(w_uv.T.reshape(MLA_HEADS, MLA_V, MLA_KV_RANK), ((0, 0), (0, VT_ROWS - MLA_V), (0, 0))
                   ).reshape(MLA_HEADS * VT_ROWS, MLA_KV_RANK).astype(BF16)
    pos = positions.astype(F32).reshape(b, s, 1)
    tm = MIX_TM
    hp = MLA_HEADS * HEAD_PAD
    tok = lambda c: pl.BlockSpec((1, tm, c), lambda bi, i: (bi, i, 0))
    out_shapes = (
        jax.ShapeDtypeStruct((b, s, hp), BF16),
        jax.ShapeDtypeStruct((b, s, hp), BF16),
        jax.ShapeDtypeStruct((b, s // FLASH_TK, MLA_HEADS * VT_ROWS, FLASH_TK), BF16),
        jax.ShapeDtypeStruct((b, s, N_BRANCH * D_MODEL), BF16),
    )
    per_chunk = FLASH_TK // tm
    out_specs = (
        tok(hp), tok(hp),
        pl.BlockSpec((1, 1, MLA_HEADS * VT_ROWS, tm), lambda bi, i: (bi, i // per_chunk, 0, i % per_chunk)),
        tok(N_BRANCH * D_MODEL),
    )
    for _, dil in DIL_PATTERNS:
        out_shapes += (jax.ShapeDtypeStruct((b, dil, s // dil, DIL_WIDTH), BF16),) * 3
        out_specs += (pl.BlockSpec((1, dil, tm // dil, DIL_WIDTH), lambda bi, i: (bi, 0, i, 0)),) * 3
    outs = pl.pallas_call(
        _mixer_in_kernel,
        name="mixer_in",
        grid=(b, s // tm),
        in_specs=[
            tok(D_MODEL), tok(1),
            _const_spec((1, D_MODEL)),
            _const_spec(wa.shape), _const_spec(wd.shape), _const_spec(wg.shape),
            _const_spec((1, N_BRANCH * D_MODEL)),
            _const_spec((1, MLA_Q_RANK)), _const_spec((1, MLA_KV_RANK)),
            _const_spec(wuq.shape), _const_spec(wuk.shape), _const_spec(wuvt.shape),
            _const_spec((8, LANES)),
        ],
        out_specs=out_specs,
        out_shape=out_shapes,
        scratch_shapes=[pltpu.VMEM((3 * DIL_WIDTH // LANES, tm, LANES), F32)],
        compiler_params=pltpu.CompilerParams(dimension_semantics=("arbitrary", "arbitrary"),
                                             vmem_limit_bytes=VMEM_LIMIT),
    )(h1, pos, mix_pre_g.reshape(1, D_MODEL), wa, wd, wg, b_gate.reshape(1, -1),
      q_norm_g.reshape(1, -1), kv_norm_g.reshape(1, -1), wuq, wuk, wuvt, _rope_table())
    q, k, vt, gates = outs[:4]
    dil_qkv = [outs[4 + 3 * i:7 + 3 * i] for i in range(len(DIL_PATTERNS))]
    return q, k, vt, gates, dil_qkv


def _flash_kernel(q_ref, k_ref, vt_ref, o_ref, s0_ref, s1_ref, s2_ref, s3_ref, qt_ref):
    nk = k_ref.shape[1] // FLASH_TK
    nq = q_ref.shape[1] // FLASH_TQ
    assert nk % 4 == 0
    bufs = (s0_ref, s1_ref, s2_ref, s3_ref)

    def scores(t, c, hh, buf):
        cmax = None
        for part in range(FLASH_TK // FLASH_TS):
            k0 = c * FLASH_TK + part * FLASH_TS
            if not isinstance(k0, int):
                k0 = pl.multiple_of(k0, FLASH_TS)
            k = k_ref[0, pl.ds(k0, FLASH_TS), HEAD_PAD * hh:HEAD_PAD * (hh + 1)]
            st = jnp.dot(k, qt_ref[hh, t], preferred_element_type=F32)
            buf[hh, FLASH_TS * part:FLASH_TS * (part + 1)] = st
            pm = jnp.max(st, axis=0, keepdims=True)
            cmax = pm if cmax is None else jnp.maximum(cmax, pm)
        return cmax

    def consume(c, hh, buf, m, cmax, acc):
        m_new = jnp.maximum(m, cmax)
        alpha = jnp.exp2(m - m_new)
        p = jnp.exp2((buf[hh] - m_new).astype(BF16))
        vt = vt_ref[0, c, VT_ROWS * hh:VT_ROWS * (hh + 1), :]
        return m_new, alpha * acc + jnp.dot(vt, p, preferred_element_type=F32)

    def half(c, state, pair_in, pair_out, t_next, c_next):
        new = []
        for hh in range(2):
            m, cm0, cm1, acc = state[hh]
            nxt = [scores(t_next, c_next + i, hh, bufs[pair_out + i]) for i in range(2)]
            m, acc = consume(c, hh, bufs[pair_in], m, cm0, acc)
            m, acc = consume(c + 1, hh, bufs[pair_in + 1], m, cm1, acc)
            new.append((m, nxt[0], nxt[1], acc))
        return tuple(new)

    def tile(t, cmaxes):
        state = tuple((jnp.full((1, FLASH_TQ), -jnp.inf, F32), cmaxes[hh][0], cmaxes[hh][1],
                       jnp.zeros((VT_ROWS, FLASH_TQ), F32)) for hh in range(2))

        def group(j, st):
            c = 4 * j
            return half(c + 2, half(c, st, 0, 2, t, c + 2), 2, 0, t, c + 4)

        state = lax.fori_loop(0, nk // 4 - 1, group, state)
        t_next =jnp.minimum(t + 1, nq - 1)
        state = half(nk - 2, half(nk - 4, state, 0, 2, t, nk - 2), 2, 0, t_next, 0)
        outs = [acc[:MLA_V] / acc[MLA_V:MLA_V + 1] for (_, _, _, acc) in state]
        o_ref[0, pl.ds(pl.multiple_of(t * FLASH_TQ, FLASH_TQ), FLASH_TQ), :] = (
            jnp.concatenate(outs, axis=0).T.astype(BF16))
        return tuple((cm0, cm1) for (_, cm0, cm1, _) in state)

    def transpose_q(t, carry):
        for hh in range(2):
            q = q_ref[0, pl.ds(pl.multiple_of(t * FLASH_TQ, FLASH_TQ), FLASH_TQ), HEAD_PAD * hh:HEAD_PAD * (hh + 1)]
            qt_ref[hh, t] = q.astype(F32).T.astype(BF16)
        return carry

    lax.fori_loop(0, nq, transpose_q, 0)
    first = tuple((scores(0, 0, hh, bufs[0]), scores(0, 1, hh, bufs[1])) for hh in range(2))
    lax.fori_loop(0, nq, tile, first)


def _mla_flash(q, k, vt):
    b, s, _ = q.shape
    nk = s // FLASH_TK
    return pl.pallas_call(
        _flash_kernel,
        name="mla_flash",
        grid=(b, MLA_HEADS // 2),
        in_specs=[
            pl.BlockSpec((1, s, 2 * HEAD_PAD), lambda bi, hp: (bi, 0, hp)),
            pl.BlockSpec((1, s, 2 * HEAD_PAD), lambda bi, hp: (bi, 0, hp)),
            pl.BlockSpec((1, nk, 2 * VT_ROWS, FLASH_TK), lambda bi, hp: (bi, 0, hp, 0)),
        ],
        out_specs=pl.BlockSpec((1, s, 2 * MLA_V), lambda bi, hp: (bi, 0, hp)),
        out_shape=jax.ShapeDtypeStruct((b, s, MLA_WIDTH), BF16),
        scratch_shapes=[pltpu.VMEM((2, FLASH_TK, FLASH_TQ), F32)] * 4
                       + [pltpu.VMEM((2, s // FLASH_TQ, HEAD_PAD, FLASH_TQ), BF16)],
        compiler_params=pltpu.CompilerParams(dimension_semantics=("arbitrary", "arbitrary"),
                                             vmem_limit_bytes=VMEM_LIMIT),
    )(q, k, vt)


def _dilated_kernel(q_ref, kc_ref, kp_ref, kn_ref, vc_ref, vp_ref, vn_ref, o_ref, lse_ref,
                    kw_ref, vt_ref, ot_ref, lt_ref, *sbufs, n_rows):
    lb = q_ref.shape[2]
    l0 = pl.program_id(2) * lb
    wrows = lb + 2 * DIL_HALF
    kw_ref[0:DIL_HALF] = kp_ref[0, 0]
    kw_ref[DIL_HALF:DIL_HALF + lb] = kc_ref[0, 0]
    kw_ref[DIL_HALF + lb:] = kn_ref[0, 0]
    vw = jnp.concatenate([vp_ref[0, 0], vc_ref[0, 0], vn_ref[0, 0]], axis=0).astype(F32)
    vt = vw.T.astype(BF16)
    ones = jnp.ones((VT_ROWS - DIL_HEAD_DIM, wrows), BF16)
    for h in range(DIL_HEADS):
        vt_ref[VT_ROWS * h:VT_ROWS * h + DIL_HEAD_DIM] = vt[DIL_HEAD_DIM * h:DIL_HEAD_DIM * (h + 1)]
        vt_ref[VT_ROWS * h + DIL_HEAD_DIM:VT_ROWS * (h + 1)] = ones

    win = DIL_SUB + 2 * DIL_HALF
    krow = lax.broadcasted_iota(jnp.int32, (win, DIL_SUB), 0)
    qcol = lax.broadcasted_iota(jnp.int32, (win, DIL_SUB), 1)
    in_band = (krow - qcol).astype(jnp.uint32) <= 2 * DIL_HALF
    lane = lax.broadcasted_iota(jnp.int32, (DIL_SUB, LANES), 1)
    odd_lanes = ((lane >= HALF_B) & (lane < ROPE_DIM)) | (lane >= LANES // 2 + HALF_B)
    nbuf = len(sbufs)

    def produce(u):
        j, h = divmod(u, DIL_HEADS)
        ls = slice(LANES * (h // 2), LANES * (h // 2 + 1))
        kpair = kw_ref[j * DIL_SUB:j * DIL_SUB + win, ls]
        qpair = q_ref[0, 0, j * DIL_SUB:(j + 1) * DIL_SUB, ls]
        qh = jnp.where(odd_lanes if h % 2 else ~odd_lanes, qpair, jnp.zeros_like(qpair))
        st = lax.dot_general(kpair, qh, (((1,), (1,)), ((), ())), preferred_element_type=F32)
        kidx = l0 - DIL_HALF + j * DIL_SUB + krow
        st = jnp.where(in_band & (kidx.astype(jnp.uint32) < n_rows), st, NEG)
        sbufs[u % nbuf][...] = st
        return jnp.max(st, axis=0, keepdims=True)

    def consume(u, m):
        j, h = divmod(u, DIL_HEADS)
        p = jnp.exp2((sbufs[u % nbuf][...] - m).astype(BF16))
        r = jnp.dot(vt_ref[VT_ROWS * h:VT_ROWS * (h + 1), j * DIL_SUB:j * DIL_SUB + win], p,
                    preferred_element_type=F32)
        den = r[DIL_HEAD_DIM:DIL_HEAD_DIM + 1]
        hs = slice(DIL_HEAD_DIM * h, DIL_HEAD_DIM * (h + 1))
        ot_ref[hs] = r[:DIL_HEAD_DIM] / den
        lt_ref[h:h + 1] = m + jnp.log2(den)
        if h == DIL_HEADS - 1:
            o_ref[0, 0, j * DIL_SUB:(j + 1) * DIL_SUB, :] = ot_ref[...].T.astype(BF16)
            lse_ref[0, 0, j * DIL_SUB:(j + 1) * DIL_SUB, :] = lt_ref[...].T

    lt_ref[...] = jnp.zeros_like(lt_ref)
    n_units = (lb // DIL_SUB) * DIL_HEADS
    grp = nbuf // 2
    m_cur = [produce(i) for i in range(grp)]
    for u in range(0, n_units, grp):
        m_next = [produce(u + grp + i) for i in range(grp)] if u + grp < n_units else None
        for i in range(grp):
            consume(u + i, m_cur[i])
        m_cur = m_next


def _dilated(dq, dk, dv, dilation):
    b, _, n_rows, w = dq.shape
    lb = min(DIL_LB, n_rows)
    hb = lb // DIL_HALF
    last = n_rows // DIL_HALF - 1
    cur = pl.BlockSpec((1, 1, lb, w), lambda bi, r, l: (bi, r, l, 0))
    prev = pl.BlockSpec((1, 1, DIL_HALF, w), lambda bi, r, l: (bi, r, jnp.maximum(l * hb - 1, 0), 0))
    nxt = pl.BlockSpec((1, 1, DIL_HALF, w), lambda bi, r, l: (bi, r, jnp.minimum((l + 1) * hb, last), 0))
    return pl.pallas_call(
        functools.partial(_dilated_kernel, n_rows=n_rows),
        name=f"dilated_{dilation}",
        grid=(b, dilation, n_rows // lb),
        in_specs=[cur, cur, prev, nxt, cur, prev, nxt],
        out_specs=(cur, pl.BlockSpec((1, 1, lb, LANES), lambda bi, r, l: (bi, r, l, 0))),
        out_shape=(jax.ShapeDtypeStruct((b, dilation, n_rows, w), BF16),
                   jax.ShapeDtypeStruct((b, dilation, n_rows, LANES), F32)),
        scratch_shapes=[pltpu.VMEM((lb + 2 * DIL_HALF, w), BF16),
                        pltpu.VMEM((DIL_HEADS * VT_ROWS, lb + 2 * DIL_HALF), BF16),
                        pltpu.VMEM((w, DIL_SUB), F32),
                        pltpu.VMEM((LANES, DIL_SUB), F32)]
                       + [pltpu.VMEM((DIL_SUB + 2 * DIL_HALF, DIL_SUB), F32)] * (2 * DIL_GROUP),
        compiler_params=pltpu.CompilerParams(dimension_semantics=("arbitrary",) * 3, vmem_limit_bytes=VMEM_LIMIT),
    )(dq, dk, dk, dk, dv, dv, dv)


def _mixer_out_kernel(h_ref, oa_ref, o1_ref, l1_ref, o2_ref, l2_ref, o3_ref, l3_ref, gate_ref,
                      wba_ref, wbb_ref, wo_ref, post_ref, expand_ref, out_ref, *scr):
    tm = h_ref.shape[1]

    def token_major(ref, dil, buf):
        if dil == 1:
            return ref[0, 0].astype(F32)
        nc = ref.shape[-1] // LANES
        for r in range(dil):
            for c in range(nc):
                buf[c, pl.ds(r, tm // dil, stride=dil), :] = ref[0, r, :, LANES * c:LANES * (c + 1)].astype(F32)
        return jnp.concatenate([buf[c] for c in range(nc)], axis=1)

    bufs = iter(scr)
    os_, ls_ = [], []
    for (o_r, l_r), (_, dil) in zip(((o1_ref, l1_ref), (o2_ref, l2_ref), (o3_ref, l3_ref)), DIL_PATTERNS):
        os_.append(token_major(o_r, dil, None if dil == 1 else next(bufs)))
        ls_.append(token_major(l_r, dil, None if dil == 1 else next(bufs)))
    lm = jnp.maximum(jnp.maximum(ls_[0], ls_[1]), ls_[2])
    es = [jnp.exp2(l - lm) for l in ls_]
    inv = 1.0 / (es[0] + es[1] + es[2])
    ob = None
    for e, o in zip(es, os_):
        w = e * inv
        hi = w.astype(BF16)
        lo = (w - hi.astype(F32)).astype(BF16)
        wide = jnp.dot(jnp.concatenate([hi, lo], axis=1), expand_ref[...], preferred_element_type=F32)
        ob = wide * o if ob is None else ob + wide * o
    ya = jnp.dot(oa_ref[0], wba_ref[...], preferred_element_type=F32)
    yb = jnp.dot(ob.astype(BF16), wbb_ref[...], preferred_element_type=F32)
    gate = gate_ref[0].astype(F32)
    merged = gate[:, :D_MODEL] * ya + gate[:, D_MODEL:] * yb
    mix = jnp.dot(merged.astype(BF16), wo_ref[...], preferred_element_type=F32)
    out_ref[0] = h_ref[0] + _rms(mix, post_ref[...])


def _mixer_out(h1, oa, parts, gates, w_branch_a, w_branch_b, w_out, mix_post_g):
    b, s, _ = h1.shape
    tm = MIX_TM
    tok = lambda c: pl.BlockSpec((1, tm, c), lambda bi, i: (bi, i, 0))
    flat, part_specs, scratch = [], [], []
    for (_, dil), pair in zip(DIL_PATTERNS, parts):
        flat += list(pair)
        for width in (DIL_WIDTH, LANES):
            part_specs.append(pl.BlockSpec((1, dil, tm // dil, width), lambda bi, i: (bi, 0, i, 0)))
            if dil > 1:
                scratch.append(pltpu.VMEM((width // LANES, tm, LANES), F32))
    head_of_lane = jnp.arange(DIL_WIDTH) // DIL_HEAD_DIM
    expand = (jnp.arange(2 * LANES)[:, None] % LANES == head_of_lane[None, :]).astype(BF16)
    return pl.pallas_call(
        _mixer_out_kernel,
        name="mixer_out",
        grid=(b, s // tm),
        scratch_shapes=scratch,
        in_specs=[tok(D_MODEL), tok(MLA_WIDTH)] + part_specs + [tok(N_BRANCH * D_MODEL),
                  _const_spec((MLA_WIDTH, D_MODEL)), _const_spec((DIL_WIDTH, D_MODEL)),
                  _const_spec((D_MODEL, D_MODEL)), _const_spec((1, D_MODEL)), _const_spec((2 * LANES, DIL_WIDTH))],
        out_specs=tok(D_MODEL),
        out_shape=jax.ShapeDtypeStruct((b, s, D_MODEL), F32),
        compiler_params=pltpu.CompilerParams(dimension_semantics=("arbitrary", "arbitrary"),
                                             vmem_limit_bytes=VMEM_LIMIT),
    )(h1, oa, *flat, gates, w_branch_a.astype(BF16), w_branch_b.astype(BF16), w_out.astype(BF16),
      mix_post_g.reshape(1, D_MODEL), expand)


def kernel(x, positions, ffn1_pre_g, ffn1_post_g, ffn1_w_gate, ffn1_w_up, ffn1_w_down, mix_pre_g, w_in, b_gate,
           q_norm_g, w_uq, kv_norm_g, w_uk, w_uv, w_branch_a, w_branch_b, w_out, mix_post_g,
           ffn2_pre_g, ffn2_post_g, ffn2_w_gate, ffn2_w_up, ffn2_w_down):
    b, s, d = x.shape
    depth = ffn1_pre_g.shape[0]
    assert all(win // (2 * dil) == DIL_HALF and s % (dil * DIL_SUB) == 0 for win, dil in DIL_PATTERNS)
    h = x
    for l in range(depth):
        h = _ffn(h.reshape(b * s, d), ffn1_pre_g[l], ffn1_post_g[l], ffn1_w_gate[l], ffn1_w_up[l],
                 ffn1_w_down[l]).reshape(b, s, d)
        q, k, vt, gates, dil_qkv = _mixer_in(h, positions, mix_pre_g[l], w_in[l], b_gate[l], q_norm_g[l],
                                             w_uq[l], kv_norm_g[l], w_uk[l], w_uv[l])
        oa = _mla_flash(q, k, vt)
        parts = [_dilated(*qkv, dil) for qkv, (_, dil) in zip(dil_qkv, DIL_PATTERNS)]
        h = _mixer_out(h, oa, parts, gates, w_branch_a[l], w_branch_b[l], w_out[l], mix_post_g[l])
        h = _ffn(h.reshape(b * s, d), ffn2_pre_g[l], ffn2_post_g[l], ffn2_w_gate[l], ffn2_w_up[l],
                 ffn2_w_down[l]).reshape(b, s, d)
    return h
```

```python
import functools

import jax
import jax.numpy as jnp
import numpy as np
from jax import lax
from jax.experimental import pallas as pl
from jax.experimental.pallas import tpu as pltpu

F32 = jnp.float32
BF16 = jnp.bfloat16

D_MODEL = 1024
D_FF = 2816
EPS = 1e-6
MLA_HEADS = 8
MLA_Q_RANK = 384
MLA_KV_RANK = 256
MLA_NOPE = 64
MLA_ROPE = 32
MLA_V = 64
MLA_THETA = 10000.0
MLA_WIDTH = MLA_HEADS * MLA_V
DIL_HEADS = 8
DIL_HEAD_DIM = 64
DIL_PATTERNS = ((128, 1), (512, 4), (2048, 16))
DIL_WIDTH = DIL_HEADS * DIL_HEAD_DIM
ROPE_THETA = 500000.0
ROPE_DIM = DIL_HEAD_DIM // 4
N_BRANCH = 2
NEG = -1e30

LANES = 128
HEAD_PAD = LANES
VMEM_LIMIT = 56 * 1024 * 1024

FFN_TM = 1024
FFN_SPLIT = 2
FFN_FC = 256
MIX_TM = 512
FLASH_TQ = 256
FLASH_TK = MIX_TM
FLASH_TS = 512
VT_ROWS = MLA_V + 16
LOG2E = 1.4426950408889634
DIL_LB = 2048
DIL_SUB = 256
DIL_GROUP = 4
DIL_HALF = 64


def _rms(x, g):
    ms = jnp.mean(x * x, axis=-1, keepdims=True)
    return x * lax.rsqrt(ms + EPS) * g


def _const_spec(shape):
    nd = len(shape)
    return pl.BlockSpec(shape, lambda *_: (0,) * nd, pipeline_mode=pl.Buffered(1))


def _ffn_kernel(x_ref, pre_ref, post_ref, wg_ref, wu_ref, wd_ref, *rest, with_rotary):
    if with_rotary:
        pos_ref, freq_ref, o_ref, cos_ref, sin_ref, hm_ref = rest
        ang = freq_ref[...] * pos_ref[...]
        cos_ref[...] = jnp.cos(ang)
        sin_ref[...] = jnp.sin(ang)
    else:
        o_ref, hm_ref = rest
    rows = x_ref.shape[0] // FFN_SPLIT
    for r in range(FFN_SPLIT):
        rs = slice(rows * r, rows * (r + 1))
        x = x_ref[rs]
        xn = _rms(x, pre_ref[...]).astype(BF16)
        for c0 in range(0, D_FF, FFN_FC):
            sl = slice(c0, min(c0 + FFN_FC, D_FF))
            g = jnp.dot(xn, wg_ref[:, sl], preferred_element_type=F32)
            u = jnp.dot(xn, wu_ref[:, sl], preferred_element_type=F32)
            hm_ref[rs, sl] = (g * jax.nn.sigmoid(g) * u).astype(BF16)
        f = jnp.dot(hm_ref[rs], wd_ref[...], preferred_element_type=F32)
        o_ref[rs] = x + 0.5 * _rms(f, post_ref[...])


def _ffn(h, pre_g, post_g, w_gate, w_up, w_down, positions=None):
    t = h.shape[0]
    tok = lambda c: pl.BlockSpec((FFN_TM, c), lambda i: (i, 0))
    in_specs = [tok(D_MODEL), _const_spec((1, D_MODEL)), _const_spec((1, D_MODEL)),
                _const_spec((D_MODEL, D_FF)), _const_spec((D_MODEL, D_FF)), _const_spec((D_FF, D_MODEL))]
    args = [h, pre_g.reshape(1, D_MODEL), post_g.reshape(1, D_MODEL),
            w_gate.astype(BF16), w_up.astype(BF16), w_down.astype(BF16)]
    out_specs, out_shape = tok(D_MODEL), jax.ShapeDtypeStruct((t, D_MODEL), F32)
    if positions is not None:
        dense = pl.BlockSpec((ROT_ROWS, FFN_TM), lambda i: (0, i))
        in_specs += [pl.BlockSpec((1, FFN_TM), lambda i: (0, i)), _const_spec((ROT_ROWS, 1))]
        args += [positions, _rope_freqs()]
        out_specs = (out_specs, dense, dense)
        out_shape = (out_shape,) + (jax.ShapeDtypeStruct((ROT_ROWS, t), F32),) * 2
    return pl.pallas_call(
        functools.partial(_ffn_kernel, with_rotary=positions is not None),
        name="ffn",
        grid=(t // FFN_TM,),
        in_specs=in_specs,
        out_specs=out_specs,
        out_shape=out_shape,
        scratch_shapes=[pltpu.VMEM((FFN_TM, D_FF), BF16)],
        compiler_params=pltpu.CompilerParams(dimension_semantics=("arbitrary",), vmem_limit_bytes=VMEM_LIMIT),
    )(*args)


def _rope_tile(x, cos, sin):
    return x * cos + pltpu.roll(x, LANES // 2, 1) * sin


def _mixer_in_kernel(h_ref, cos_ref, sin_ref, pre_ref, wa_ref, wd_ref, wg_ref, bg_ref, qg_ref, kvg_ref,
                     wuq_ref, wuk_ref, wuvt_ref,
                     q_ref, k_ref, vt_ref, gate_ref, *rest):
    dil_refs, dscr_ref = rest[:-1], rest[-1]
    u = _rms(h_ref[0], pre_ref[...]).astype(BF16)
    def token_tile(ref, fill):
        pad = jnp.full((LANES // 2 - HALF_A, ref.shape[1]), fill, F32)
        return jnp.concatenate([ref[0:HALF_A], pad, ref[HALF_A:2 * HALF_A], pad], axis=0).T

    cc = token_tile(cos_ref, 1.0)
    ss = token_tile(sin_ref, 0.0)
    rcc = pltpu.roll(cc, LANES // 2, 1)
    rss = pltpu.roll(ss, LANES // 2, 1)
    low = lax.broadcasted_iota(jnp.int32, cc.shape, 1) < LANES // 2
    cos_a = jnp.where(low, cc, rcc)
    sin_a = jnp.where(low, -ss, rss)
    cos_b = jnp.where(low, rcc, cc)
    sin_b = jnp.where(low, -rss, ss)

    pa = jnp.dot(u, wa_ref[...], preferred_element_type=F32)
    cq = pa[:, :MLA_Q_RANK]
    ckv = pa[:, MLA_Q_RANK:MLA_Q_RANK + MLA_KV_RANK]
    kr = _rope_tile(pa[:, MLA_Q_RANK + MLA_KV_RANK:], cos_a, sin_a)
    qn = _rms(cq, qg_ref[...]).astype(BF16)
    ckvn = _rms(ckv, kvg_ref[...]).astype(BF16)
    q = jnp.dot(qn, wuq_ref[...], preferred_element_type=F32)

    pg = jnp.dot(u, wg_ref[...], preferred_element_type=F32) + bg_ref[...]
    gate_ref[0] = jax.nn.sigmoid(pg).astype(BF16)

    kn = jnp.dot(ckvn, wuk_ref[...], preferred_element_type=F32)
    scale = LOG2E * (MLA_NOPE + MLA_ROPE) ** -0.5
    for h in range(MLA_HEADS):
        sl = slice(HEAD_PAD * h, HEAD_PAD * (h + 1))
        qh = _rope_tile(q[:, sl], cos_a, sin_a)
        q_ref[0, :, sl] = (qh * scale).astype(BF16)
        k_ref[0, :, sl] = (kn[:, sl] + kr).astype(BF16)
    vt = lax.dot_general(wuvt_ref[...], ckvn, (((1,), (1,)), ((), ())), preferred_element_type=F32)
    row = lax.broadcasted_iota(jnp.int32, vt.shape, 0)
    vt_ref[0, 0] = jnp.where(row % VT_ROWS >= MLA_V, 1.0, vt).astype(BF16)

    pd = jnp.dot(u, wd_ref[...], preferred_element_type=F32)
    dscale = LOG2E * DIL_HEAD_DIM ** -0.5
    nc = DIL_WIDTH // LANES
    for c in range(nc):
        sl = slice(LANES * c, LANES * (c + 1))
        ks = slice(DIL_WIDTH + LANES * c, DIL_WIDTH + LANES * (c + 1))
        dscr_ref[c] = _rope_tile(pd[:, sl], cos_b, sin_b) * dscale
        dscr_ref[nc + c] = _rope_tile(pd[:, ks], cos_b, sin_b)
        dscr_ref[2 * nc + c] = pd[:, 2 * DIL_WIDTH + LANES * c:2 * DIL_WIDTH + LANES * (c + 1)]
    tm = dscr_ref.shape[1]
    for pi, (_, dil) in enumerate(DIL_PATTERNS):
        for t in range(3):
            for c in range(nc):
                if dil == 1:
                    dil_refs[3 * pi + t][0, 0, :, LANES * c:LANES * (c + 1)] = dscr_ref[nc * t + c].astype(BF16)
                    continue
                for r in range(dil):
                    rows = dscr_ref[nc * t + c, pl.ds(r, tm // dil, stride=dil), :]
                    dil_refs[3 * pi + t][0, r, :, LANES * c:LANES * (c + 1)] = rows.astype(BF16)


HALF_A = MLA_ROPE // 2
HALF_B = ROPE_DIM // 2
NOPE_LO = LANES // 2 - HALF_A
ROT_ROWS = 2 * HALF_A


def _rope_freqs():
    assert 2 * HALF_B == HALF_A
    inv_a = 1.0 / (jnp.float32(MLA_THETA) ** (jnp.arange(HALF_A, dtype=F32) / HALF_A))
    inv_b = 1.0 / (jnp.float32(ROPE_THETA) ** (jnp.arange(HALF_B, dtype=F32) / HALF_B))
    return jnp.concatenate([inv_a, inv_b, inv_b]).reshape(ROT_ROWS, 1)


def _mla_tile_cols(nope, rope):
    lead = (nope if nope is not None else rope).shape[:-1]
    zeros = lambda n: jnp.zeros(lead + (n,), F32)
    x1, x2 = (rope[..., :HALF_A], rope[..., HALF_A:]) if rope is not None else (zeros(HALF_A), zeros(HALF_A))
    n_lo, n_hi = (nope[..., :NOPE_LO], nope[..., NOPE_LO:]) if nope is not None else (zeros(NOPE_LO), zeros(MLA_NOPE - NOPE_LO))
    return jnp.concatenate([x1, n_lo, x2, n_hi, zeros(HEAD_PAD - MLA_NOPE - MLA_ROPE)], axis=-1)


def _dil_tile_cols(w):
    wh = w.reshape(w.shape[0], DIL_HEADS // 2, 2, DIL_HEAD_DIM)
    a, b = wh[:, :, 0], wh[:, :, 1]
    tile = jnp.concatenate([a[..., :HALF_B], b[..., :HALF_B], a[..., ROPE_DIM:], a[..., HALF_B:ROPE_DIM],
                            b[..., HALF_B:ROPE_DIM], b[..., ROPE_DIM:]], axis=-1)
    return tile.reshape(w.shape[0], DIL_WIDTH)


def _mixer_in(h1, rot_cos, rot_sin, mix_pre_g, w_in, b_gate, q_norm_g, w_uq, kv_norm_g, w_uk, w_uv):
    b, s, _ = h1.shape
    o0 = 0
    o1 = o0 + MLA_Q_RANK
    o2 = o1 + MLA_KV_RANK
    o3 = o2 + MLA_ROPE
    o4 = o3 + 3 * DIL_WIDTH
    w_kr = _mla_tile_cols(None, w_in[:, o2:o3])
    wa = jnp.concatenate([w_in[:, o0:o2], w_kr], axis=1).astype(BF16)
    wd = jnp.concatenate([_dil_tile_cols(w_in[:, o3:o3 + DIL_WIDTH]),
                          _dil_tile_cols(w_in[:, o3 + DIL_WIDTH:o3 + 2 * DIL_WIDTH]),
                          w_in[:, o3 + 2 * DIL_WIDTH:o4]], axis=1).astype(BF16)
    wg = w_in[:, o4:].astype(BF16)
    wuq_h = w_uq.reshape(MLA_Q_RANK, MLA_HEADS, MLA_NOPE + MLA_ROPE)
    wuq = _mla_tile_cols(wuq_h[..., :MLA_NOPE], wuq_h[..., MLA_NOPE:]).reshape(MLA_Q_RANK, -1).astype(BF16)
    wuk = _mla_tile_cols(w_uk.reshape(MLA_KV_RANK, MLA_HEADS, MLA_NOPE), None).reshape(MLA_KV_RANK, -1).astype(BF16)
    wuvt = jnp.pad(w_uv.T.reshape(MLA_HEADS, MLA_V, MLA_KV_RANK), ((0, 0), (0, VT_ROWS - MLA_V), (0, 0))
                   ).reshape(MLA_HEADS * VT_ROWS, MLA_KV_RANK).astype(BF16)
    tm = MIX_TM
    hp = MLA_HEADS * HEAD_PAD
    tok = lambda c: pl.BlockSpec((1, tm, c), lambda bi, i: (bi, i, 0))
    rot = pl.BlockSpec((ROT_ROWS, tm), lambda bi, i: (0, bi * (s // tm) + i))
    out_shapes = (
        jax.ShapeDtypeStruct((b, s, hp), BF16),
        jax.ShapeDtypeStruct((b, s, hp), BF16),
        jax.ShapeDtypeStruct((b, s // FLASH_TK, MLA_HEADS * VT_ROWS, FLASH_TK), BF16),
        jax.ShapeDtypeStruct((b, s, N_BRANCH * D_MODEL), BF16),
    )
    per_chunk = FLASH_TK // tm
    out_specs = (
        tok(hp), tok(hp),
        pl.BlockSpec((1, 1, MLA_HEADS * VT_ROWS, tm), lambda bi, i: (bi, i // per_chunk, 0, i % per_chunk)),
        tok(N_BRANCH * D_MODEL),
    )
    for _, dil in DIL_PATTERNS:
        out_shapes += (jax.ShapeDtypeStruct((b, dil, s // dil, DIL_WIDTH), BF16),) * 3
        out_specs += (pl.BlockSpec((1, dil, tm // dil, DIL_WIDTH), lambda bi, i: (bi, 0, i, 0)),) * 3
    outs = pl.pallas_call(
        _mixer_in_kernel,
        name="mixer_in",
        grid=(b, s // tm),
        in_specs=[
            tok(D_MODEL), rot, rot,
            _const_spec((1, D_MODEL)),
            _const_spec(wa.shape), _const_spec(wd.shape), _const_spec(wg.shape),
            _const_spec((1, N_BRANCH * D_MODEL)),
            _const_spec((1, MLA_Q_RANK)), _const_spec((1, MLA_KV_RANK)),
            _const_spec(wuq.shape), _const_spec(wuk.shape), _const_spec(wuvt.shape),
        ],
        out_specs=out_specs,
        out_shape=out_shapes,
        scratch_shapes=[pltpu.VMEM((3 * DIL_WIDTH // LANES, tm, LANES), F32)],
        compiler_params=pltpu.CompilerParams(dimension_semantics=("arbitrary", "arbitrary"),
                                             vmem_limit_bytes=VMEM_LIMIT),
    )(h1, rot_cos, rot_sin, mix_pre_g.reshape(1, D_MODEL), wa, wd, wg, b_gate.reshape(1, -1),
      q_norm_g.reshape(1, -1), kv_norm_g.reshape(1, -1), wuq, wuk, wuvt)
    q, k, vt, gates = outs[:4]
    dil_qkv = [outs[4 + 3 * i:7 + 3 * i] for i in range(len(DIL_PATTERNS))]
    return q, k, vt, gates, dil_qkv


def _flash_kernel(q_ref, k_ref, vt_ref, o_ref, s0_ref, s1_ref, s2_ref, s3_ref, qt_ref):
    nk = k_ref.shape[1] // FLASH_TK
    nq = q_ref.shape[1] // FLASH_TQ
    assert nk % 4 == 0
    bufs = (s0_ref, s1_ref, s2_ref, s3_ref)

    def scores(t, c, hh, buf):
        cmax = None
        for part in range(FLASH_TK // FLASH_TS):
            k0 = c * FLASH_TK + part * FLASH_TS
            if not isinstance(k0, int):
                k0 = pl.multiple_of(k0, FLASH_TS)
            k = k_ref[0, pl.ds(k0, FLASH_TS), HEAD_PAD * hh:HEAD_PAD * (hh + 1)]
            st = jnp.dot(k, qt_ref[hh, t], preferred_element_type=F32)
            buf[hh, FLASH_TS * part:FLASH_TS * (part + 1)] = st
            pm = jnp.max(st, axis=0, keepdims=True)
            cmax = pm if cmax is None else jnp.maximum(cmax, pm)
        return cmax

    def consume(c, hh, buf, m, cmax, acc):
        m_new = jnp.maximum(m, cmax)
        alpha = jnp.exp2(m - m_new)
        p = jnp.exp2((buf[hh] - m_new).astype(BF16))
        vt = vt_ref[0, c, VT_ROWS * hh:VT_ROWS * (hh + 1), :]
        return m_new, alpha * acc + jnp.dot(vt, p, preferred_element_type=F32)

    def half(c, state, pair_in, pair_out, t_next, c_next):
        new = []
        for hh in range(2):
            m, cm0, cm1, acc = state[hh]
            nxt = [scores(t_next, c_next + i, hh, bufs[pair_out + i]) for i in range(2)]
            m, acc = consume(c, hh, bufs[pair_in], m, cm0, acc)
            m, acc = consume(c + 1, hh, bufs[pair_in + 1], m, cm1, acc)
            new.append((m, nxt[0], nxt[1], acc))
        return tuple(new)

    def tile(t, cmaxes):
        state = tuple((jnp.full((1, FLASH_TQ), -jnp.inf, F32), cmaxes[hh][0], cmaxes[hh][1],
                       jnp.zeros((VT_ROWS, FLASH_TQ), F32)) for hh in range(2))

        def group(j, st):
            c = 4 * j
            return half(c + 2, half(c, st, 0, 2, t, c + 2), 2, 0, t, c + 4)

        state = lax.fori_loop(0, nk // 4 - 1, group, state)
        t_next = jnp.minimum(t + 1, nq - 1)
        state = half(nk - 2, half(nk - 4, state, 0, 2, t, nk - 2), 2, 0, t_next, 0)
        outs = [acc[:MLA_V] / acc[MLA_V:MLA_V + 1] for (_, _, _, acc) in state]
        o_ref[0, pl.ds(pl.multiple_of(t * FLASH_TQ, FLASH_TQ), FLASH_TQ), :] = (
            jnp.concatenate(outs, axis=0).T.astype(BF16))
        return tuple((cm0, cm1) for (_, cm0, cm1, _) in state)

    def transpose_q(t, carry):
        for hh in range(2):
            q = q_ref[0, pl.ds(pl.multiple_of(t * FLASH_TQ, FLASH_TQ), FLASH_TQ), HEAD_PAD * hh:HEAD_PAD * (hh + 1)]
            qt_ref[hh, t] = q.astype(F32).T.astype(BF16)
        return carry

    lax.fori_loop(0, nq, transpose_q, 0)
    first = tuple((scores(0, 0, hh, bufs[0]), scores(0, 1, hh, bufs[1])) for hh in range(2))
    lax.fori_loop(0, nq, tile, first)


def _mla_flash(q, k, vt):
    b, s, _ = q.shape
    nk = s // FLASH_TK
    return pl.pallas_call(
        _flash_kernel,
        name="mla_flash",
        grid=(b, MLA_HEADS // 2),
        in_specs=[
            pl.BlockSpec((1, s, 2 * HEAD_PAD), lambda bi, hp: (bi, 0, hp)),
            pl.BlockSpec((1, s, 2 * HEAD_PAD), lambda bi, hp: (bi, 0, hp)),
            pl.BlockSpec((1, nk, 2 * VT_ROWS, FLASH_TK), lambda bi, hp: (bi, 0, hp, 0)),
        ],
        out_specs=pl.BlockSpec((1, s, 2 * MLA_V), lambda bi, hp: (bi, 0, hp)),
        out_shape=jax.ShapeDtypeStruct((b, s, MLA_WIDTH), BF16),
        scratch_shapes=[pltpu.VMEM((2, FLASH_TK, FLASH_TQ), F32)] * 4
                       + [pltpu.VMEM((2, s // FLASH_TQ, HEAD_PAD, FLASH_TQ), BF16)],
        compiler_params=pltpu.CompilerParams(dimension_semantics=("arbitrary", "arbitrary"),
                                             vmem_limit_bytes=VMEM_LIMIT),
    )(q, k, vt)


def _dilated_kernel(q_ref, kc_ref, kp_ref, kn_ref, vc_ref, vp_ref, vn_ref, o_ref, lse_ref,
                    kw_ref, vt_ref, ot_ref, lt_ref, *sbufs, n_rows):
    lb = q_ref.shape[2]
    l0 = pl.program_id(2) * lb
    wrows = lb + 2 * DIL_HALF
    kw_ref[0:DIL_HALF] = kp_ref[0, 0]
    kw_ref[DIL_HALF:DIL_HALF + lb] = kc_ref[0, 0]
    kw_ref[DIL_HALF + lb:] = kn_ref[0, 0]
    vw = jnp.concatenate([vp_ref[0, 0], vc_ref[0, 0], vn_ref[0, 0]], axis=0).astype(F32)
    vt = vw.T.astype(BF16)
    ones = jnp.ones((VT_ROWS - DIL_HEAD_DIM, wrows), BF16)
    for h in range(DIL_HEADS):
        vt_ref[VT_ROWS * h:VT_ROWS * h + DIL_HEAD_DIM] = vt[DIL_HEAD_DIM * h:DIL_HEAD_DIM * (h + 1)]
        vt_ref[VT_ROWS * h + DIL_HEAD_DIM:VT_ROWS * (h + 1)] = ones

    win = DIL_SUB + 2 * DIL_HALF
    krow = lax.broadcasted_iota(jnp.int32, (win, DIL_SUB), 0)
    qcol = lax.broadcasted_iota(jnp.int32, (win, DIL_SUB), 1)
    in_band = (krow - qcol).astype(jnp.uint32) <= 2 * DIL_HALF
    lane = lax.broadcasted_iota(jnp.int32, (DIL_SUB, LANES), 1)
    odd_lanes = ((lane >= HALF_B) & (lane < ROPE_DIM)) | (lane >= LANES // 2 + HALF_B)
    nbuf = len(sbufs)

    def produce(u):
        j, h = divmod(u, DIL_HEADS)
        ls = slice(LANES * (h // 2), LANES * (h // 2 + 1))
        kpair = kw_ref[j * DIL_SUB:j * DIL_SUB + win, ls]
        qpair = q_ref[0, 0, j * DIL_SUB:(j + 1) * DIL_SUB, ls]
        qh = jnp.where(odd_lanes if h % 2 else ~odd_lanes, qpair, jnp.zeros_like(qpair))
        st = lax.dot_general(kpair, qh, (((1,), (1,)), ((), ())), preferred_element_type=F32)
        kidx = l0 - DIL_HALF + j * DIL_SUB + krow
        st = jnp.where(in_band & (kidx.astype(jnp.uint32) < n_rows), st, NEG)
        sbufs[u % nbuf][...] = st
        return jnp.max(st, axis=0, keepdims=True)

    def consume(u, m):
        j, h = divmod(u, DIL_HEADS)
        p = jnp.exp2((sbufs[u % nbuf][...] - m).astype(BF16))
        r = jnp.dot(vt_ref[VT_ROWS * h:VT_ROWS * (h + 1), j * DIL_SUB:j * DIL_SUB + win], p,
                    preferred_element_type=F32)
        den = r[DIL_HEAD_DIM:DIL_HEAD_DIM + 1]
        hs = slice(DIL_HEAD_DIM * h, DIL_HEAD_DIM * (h + 1))
        ot_ref[hs] = r[:DIL_HEAD_DIM] / den
        lt_ref[h:h + 1] = m + jnp.log2(den)
        if h == DIL_HEADS - 1:
            o_ref[0, 0, j * DIL_SUB:(j + 1) * DIL_SUB, :] = ot_ref[...].T.astype(BF16)
            lse_ref[0, 0, j * DIL_SUB:(j + 1) * DIL_SUB, :] = lt_ref[...].T

    lt_ref[...] = jnp.zeros_like(lt_ref)
    n_units = (lb // DIL_SUB) * DIL_HEADS
    grp = nbuf // 2
    m_cur = [produce(i) for i in range(grp)]
    for u in range(0, n_units, grp):
        m_next = [produce(u + grp + i) for i in range(grp)] if u + grp < n_units else None
        for i in range(grp):
            consume(u + i, m_cur[i])
        m_cur = m_next


def _dilated(dq, dk, dv, dilation):
    b, _, n_rows, w = dq.shape
    lb = min(DIL_LB, n_rows)
    hb = lb // DIL_HALF
    last = n_rows // DIL_HALF - 1
    cur = pl.BlockSpec((1, 1, lb, w), lambda bi, r, l: (bi, r, l, 0))
    prev = pl.BlockSpec((1, 1, DIL_HALF, w), lambda bi, r, l: (bi, r, jnp.maximum(l * hb - 1, 0), 0))
    nxt = pl.BlockSpec((1, 1, DIL_HALF, w), lambda bi, r, l: (bi, r, jnp.minimum((l + 1) * hb, last), 0))
    return pl.pallas_call(
        functools.partial(_dilated_kernel, n_rows=n_rows),
        name=f"dilated_{dilation}",
        grid=(b, dilation, n_rows // lb),
        in_specs=[cur, cur, prev, nxt, cur, prev, nxt],
        out_specs=(cur, pl.BlockSpec((1, 1, lb, LANES), lambda bi, r, l: (bi, r, l, 0))),
        out_shape=(jax.ShapeDtypeStruct((b, dilation, n_rows, w), BF16),
                   jax.ShapeDtypeStruct((b, dilation, n_rows, LANES), F32)),
        scratch_shapes=[pltpu.VMEM((lb + 2 * DIL_HALF, w), BF16),
                        pltpu.VMEM((DIL_HEADS * VT_ROWS, lb + 2 * DIL_HALF), BF16),
                        pltpu.VMEM((w, DIL_SUB), F32),
                        pltpu.VMEM((LANES, DIL_SUB), F32)]
                       + [pltpu.VMEM((DIL_SUB + 2 * DIL_HALF, DIL_SUB), F32)] * (2 * DIL_GROUP),
        compiler_params=pltpu.CompilerParams(dimension_semantics=("arbitrary",) * 3, vmem_limit_bytes=VMEM_LIMIT),
    )(dq, dk, dk, dk, dv, dv, dv)


def _mixer_out_kernel(h_ref, oa_ref, o1_ref, l1_ref, o2_ref, l2_ref, o3_ref, l3_ref, gate_ref,
                      wba_ref, wbb_ref, wo_ref, post_ref, expand_ref, out_ref, *scr):
    tm = h_ref.shape[1]

    def token_major(ref, dil, buf):
        if dil == 1:
            return ref[0, 0].astype(F32)
        nc = ref.shape[-1] // LANES
        for r in range(dil):
            for c in range(nc):
                buf[c, pl.ds(r, tm // dil, stride=dil), :] = ref[0, r, :, LANES * c:LANES * (c + 1)].astype(F32)
        return jnp.concatenate([buf[c] for c in range(nc)], axis=1)

    bufs = iter(scr)
    os_, ls_ = [], []
    for (o_r, l_r), (_, dil) in zip(((o1_ref, l1_ref), (o2_ref, l2_ref), (o3_ref, l3_ref)), DIL_PATTERNS):
        os_.append(token_major(o_r, dil, None if dil == 1 else next(bufs)))
        ls_.append(token_major(l_r, dil, None if dil == 1 else next(bufs)))
    lm = jnp.maximum(jnp.maximum(ls_[0], ls_[1]), ls_[2])
    es = [jnp.exp2(l - lm) for l in ls_]
    inv = 1.0 / (es[0] + es[1] + es[2])
    ob = None
    for e, o in zip(es, os_):
        w = e * inv
        hi = w.astype(BF16)
        lo = (w - hi.astype(F32)).astype(BF16)
        wide = jnp.dot(jnp.concatenate([hi, lo], axis=1), expand_ref[...], preferred_element_type=F32)
        ob = wide * o if ob is None else ob + wide * o
    ya = jnp.dot(oa_ref[0], wba_ref[...], preferred_element_type=F32)
    yb = jnp.dot(ob.astype(BF16), wbb_ref[...], preferred_element_type=F32)
    gate = gate_ref[0].astype(F32)
    merged = gate[:, :D_MODEL] * ya + gate[:, D_MODEL:] * yb
    mix = jnp.dot(merged.astype(BF16), wo_ref[...], preferred_element_type=F32)
    out_ref[0] = h_ref[0] + _rms(mix, post_ref[...])


def _mixer_out(h1, oa, parts, gates, w_branch_a, w_branch_b, w_out, mix_post_g):
    b, s, _ = h1.shape
    tm = MIX_TM
    tok = lambda c: pl.BlockSpec((1, tm, c), lambda bi, i: (bi, i, 0))
    flat, part_specs, scratch = [], [], []
    for (_, dil), pair in zip(DIL_PATTERNS, parts):
        flat += list(pair)
        for width in (DIL_WIDTH, LANES):
            part_specs.append(pl.BlockSpec((1, dil, tm // dil, width), lambda bi, i: (bi, 0, i, 0)))
            if dil > 1:
                scratch.append(pltpu.VMEM((width // LANES, tm, LANES), F32))
    head_of_lane = jnp.arange(DIL_WIDTH) // DIL_HEAD_DIM
    expand = (jnp.arange(2 * LANES)[:, None] % LANES == head_of_lane[None, :]).astype(BF16)
    return pl.pallas_call(
        _mixer_out_kernel,
        name="mixer_out",
        grid=(b, s // tm),
        scratch_shapes=scratch,
        in_specs=[tok(D_MODEL), tok(MLA_WIDTH)] + part_specs + [tok(N_BRANCH * D_MODEL),
                  _const_spec((MLA_WIDTH, D_MODEL)), _const_spec((DIL_WIDTH, D_MODEL)),
                  _const_spec((D_MODEL, D_MODEL)), _const_spec((1, D_MODEL)), _const_spec((2 * LANES, DIL_WIDTH))],
        out_specs=tok(D_MODEL),
        out_shape=jax.ShapeDtypeStruct((b, s, D_MODEL), F32),
        compiler_params=pltpu.CompilerParams(dimension_semantics=("arbitrary", "arbitrary"),
                                             vmem_limit_bytes=VMEM_LIMIT),
    )(h1, oa, *flat, gates, w_branch_a.astype(BF16), w_branch_b.astype(BF16), w_out.astype(BF16),
      mix_post_g.reshape(1, D_MODEL), expand)


def kernel(x, positions, ffn1_pre_g, ffn1_post_g, ffn1_w_gate, ffn1_w_up, ffn1_w_down, mix_pre_g, w_in, b_gate,
           q_norm_g, w_uq, kv_norm_g, w_uk, w_uv, w_branch_a, w_branch_b, w_out, mix_post_g,
           ffn2_pre_g, ffn2_post_g, ffn2_w_gate, ffn2_w_up, ffn2_w_down):
    b, s, d = x.shape
    depth = ffn1_pre_g.shape[0]
    assert all(win // (2 * dil) == DIL_HALF and s % (dil * DIL_SUB) == 0 for win, dil in DIL_PATTERNS)
    h = x
    for l in range(depth):
        h, rot_cos, rot_sin = _ffn(h.reshape(b * s, d), ffn1_pre_g[l], ffn1_post_g[l], ffn1_w_gate[l], ffn1_w_up[l],
                                   ffn1_w_down[l], positions=positions.astype(F32).reshape(1, b * s))
        h = h.reshape(b, s, d)
        q, k, vt, gates, dil_qkv = _mixer_in(h, rot_cos, rot_sin, mix_pre_g[l], w_in[l], b_gate[l], q_norm_g[l],
                                             w_uq[l], kv_norm_g[l], w_uk[l], w_uv[l])
        oa = _mla_flash(q, k, vt)
        parts = [_dilated(*qkv, dil) for qkv, (_, dil) in zip(dil_qkv, DIL_PATTERNS)]
        h = _mixer_out(h, oa, parts, gates, w_branch_a[l], w_branch_b[l], w_out[l], mix_post_g[l])
        h = _ffn(h.reshape(b * s, d), ffn2_pre_g[l], ffn2_post_g[l], ffn2_w_gate[l], ffn2_w_up[l],
                 ffn2_w_down[l]).reshape(b, s, d)
    return h
```

```python
import functools

import jax
import jax.numpy as jnp
from jax import lax
from jax.experimental import pallas as pl
from jax.experimental.pallas import tpu as pltpu

F32 = jnp.float32
BF16 = jnp.bfloat16

D_MODEL = 1024
D_FF = 2816
EPS = 1e-6
MLA_HEADS = 8
MLA_Q_RANK = 384
MLA_KV_RANK = 256
MLA_NOPE = 64
MLA_ROPE = 32
MLA_V = 64
MLA_THETA = 10000.0
MLA_WIDTH = MLA_HEADS * MLA_V
DIL_HEADS = 8
DIL_HEAD_DIM = 64
DIL_PATTERNS = ((128, 1), (512, 4), (2048, 16))
DIL_WIDTH = DIL_HEADS * DIL_HEAD_DIM
ROPE_THETA = 500000.0
ROPE_DIM = DIL_HEAD_DIM // 4
N_BRANCH = 2
NEG = -1e30

LANES = 128
HEAD_PAD = LANES
VMEM_LIMIT = 56 * 1024 * 1024

FFN_TM = 1024
FFN_SPLIT = 2
FFN_FC = 256
MIX_TM = 512
FLASH_TQ = 256
FLASH_TK = MIX_TM
FLASH_TS = FLASH_TK
VT_ROWS = MLA_V + 16
LOG2E = 1.4426950408889634
DIL_LB = 2048
DIL_SUB = 256
DIL_GROUP = 4
DIL_HALF = 64


def _rms(x, g):
    ms = jnp.mean(x * x, axis=-1, keepdims=True)
    return x * lax.rsqrt(ms + EPS) * g


def _const_spec(shape):
    nd = len(shape)
    return pl.BlockSpec(shape, lambda *_: (0,) * nd, pipeline_mode=pl.Buffered(1))


def _ffn_kernel(x_ref, pre_ref, post_ref, wg_ref, wu_ref, wd_ref, *rest, with_rotary):
    if with_rotary:
        pos_ref, freq_ref, o_ref, cos_ref, sin_ref, hm_ref = rest
        ang = freq_ref[...] * pos_ref[...]
        cos_ref[...] = jnp.cos(ang)
        sin_ref[...] = jnp.sin(ang)
    else:
        o_ref, hm_ref = rest
    rows = x_ref.shape[0] // FFN_SPLIT
    for r in range(FFN_SPLIT):
        rs = slice(rows * r, rows * (r + 1))
        x = x_ref[rs]
        xn = _rms(x, pre_ref[...]).astype(BF16)
        for c0 in range(0, D_FF, FFN_FC):
            sl = slice(c0, min(c0 + FFN_FC, D_FF))
            g = jnp.dot(xn, wg_ref[:, sl], preferred_element_type=F32)
            u = jnp.dot(xn, wu_ref[:, sl], preferred_element_type=F32)
            hm_ref[rs, sl] = (g * jax.nn.sigmoid(g) * u).astype(BF16)
        f = jnp.dot(hm_ref[rs], wd_ref[...], preferred_element_type=F32)
        o_ref[rs] = x + 0.5 * _rms(f, post_ref[...])


def _ffn(h, pre_g, post_g, w_gate, w_up, w_down, positions=None):
    t = h.shape[0]
    tok = lambda c: pl.BlockSpec((FFN_TM, c), lambda i: (i, 0))
    in_specs = [tok(D_MODEL), _const_spec((1, D_MODEL)), _const_spec((1, D_MODEL)),
                _const_spec((D_MODEL, D_FF)), _const_spec((D_MODEL, D_FF)), _const_spec((D_FF, D_MODEL))]
    args = [h, pre_g.reshape(1, D_MODEL), post_g.reshape(1, D_MODEL),
            w_gate.astype(BF16), w_up.astype(BF16), w_down.astype(BF16)]
    out_specs, out_shape = tok(D_MODEL), jax.ShapeDtypeStruct((t, D_MODEL), F32)
    if positions is not None:
        dense = pl.BlockSpec((ROT_ROWS, FFN_TM), lambda i: (0, i))
        in_specs += [pl.BlockSpec((1, FFN_TM), lambda i: (0, i)), _const_spec((ROT_ROWS, 1))]
        args += [positions, _rope_freqs()]
        out_specs = (out_specs, dense, dense)
        out_shape = (out_shape,) + (jax.ShapeDtypeStruct((ROT_ROWS, t), F32),) * 2
    return pl.pallas_call(
        functools.partial(_ffn_kernel, with_rotary=positions is not None),
        name="ffn",
        grid=(t // FFN_TM,),
        in_specs=in_specs,
        out_specs=out_specs,
        out_shape=out_shape,
        scratch_shapes=[pltpu.VMEM((FFN_TM, D_FF), BF16)],
        compiler_params=pltpu.CompilerParams(dimension_semantics=("arbitrary",), vmem_limit_bytes=VMEM_LIMIT),
    )(*args)


def _rope_tile(x, cos, sin):
    return x * cos + pltpu.roll(x, LANES // 2, 1) * sin


def _mixer_in_kernel(h_ref, cos_ref, sin_ref, pre_ref, wa_ref, wd_ref, wg_ref, bg_ref, qg_ref, kvg_ref,
                     wuq_ref, wuk_ref, wuvt_ref,
                     q_ref, k_ref, vt_ref, gate_ref, *rest):
    dil_refs, dscr_ref = rest[:-1], rest[-1]
    u = _rms(h_ref[0], pre_ref[...]).astype(BF16)
    def token_tile(ref, fill):
        pad = jnp.full((LANES // 2 - HALF_A, ref.shape[1]), fill, F32)
        return jnp.concatenate([ref[0:HALF_A], pad, ref[HALF_A:2 * HALF_A], pad], axis=0).T

    cc = token_tile(cos_ref, 1.0)
    ss = token_tile(sin_ref, 0.0)
    rcc = pltpu.roll(cc, LANES // 2, 1)
    rss = pltpu.roll(ss, LANES // 2, 1)
    low = lax.broadcasted_iota(jnp.int32, cc.shape, 1) < LANES // 2
    cos_a = jnp.where(low, cc, rcc)
    sin_a = jnp.where(low, -ss, rss)
    cos_b = jnp.where(low, rcc, cc)
    sin_b = jnp.where(low, -rss, ss)

    pa = jnp.dot(u, wa_ref[...], preferred_element_type=F32)
    cq = pa[:, :MLA_Q_RANK]
    ckv = pa[:, MLA_Q_RANK:MLA_Q_RANK + MLA_KV_RANK]
    kr = _rope_tile(pa[:, MLA_Q_RANK + MLA_KV_RANK:], cos_a, sin_a)
    qn = _rms(cq, qg_ref[...]).astype(BF16)
    ckvn = _rms(ckv, kvg_ref[...]).astype(BF16)
    q = jnp.dot(qn, wuq_ref[...], preferred_element_type=F32)

    pg = jnp.dot(u, wg_ref[...], preferred_element_type=F32) + bg_ref[...]
    gate_ref[0] = jax.nn.sigmoid(pg).astype(BF16)

    kn = jnp.dot(ckvn, wuk_ref[...], preferred_element_type=F32)
    scale = LOG2E * (MLA_NOPE + MLA_ROPE) ** -0.5
    for h in range(MLA_HEADS):
        sl = slice(HEAD_PAD * h, HEAD_PAD * (h + 1))
        qh = _rope_tile(q[:, sl], cos_a, sin_a)
        q_ref[0, :, sl] = (qh * scale).astype(BF16)
        k_ref[0, :, sl] = (kn[:, sl] + kr).astype(BF16)
    vt = lax.dot_general(wuvt_ref[...], ckvn, (((1,), (1,)), ((), ())), preferred_element_type=F32)
    row = lax.broadcasted_iota(jnp.int32, vt.shape, 0)
    vt_ref[0, 0] = jnp.where(row % VT_ROWS >= MLA_V, 1.0, vt).astype(BF16)

    pd = jnp.dot(u, wd_ref[...], preferred_element_type=F32)
    dscale = LOG2E * DIL_HEAD_DIM ** -0.5
    nc = DIL_WIDTH // LANES
    for c in range(nc):
        sl = slice(LANES * c, LANES * (c + 1))
        ks = slice(DIL_WIDTH + LANES * c, DIL_WIDTH + LANES * (c + 1))
        dscr_ref[c] = _rope_tile(pd[:, sl], cos_b, sin_b) * dscale
        dscr_ref[nc + c] = _rope_tile(pd[:, ks], cos_b, sin_b)
        dscr_ref[2 * nc + c] = pd[:, 2 * DIL_WIDTH + LANES * c:2 * DIL_WIDTH + LANES * (c + 1)]
    tm = dscr_ref.shape[1]
    for pi, (_, dil) in enumerate(DIL_PATTERNS):
        for t in range(3):
            for c in range(nc):
                if dil == 1:
                    dil_refs[3 * pi + t][0, 0, :, LANES * c:LANES * (c + 1)] = dscr_ref[nc * t + c].astype(BF16)
                    continue
                for r in range(dil):
                    rows = dscr_ref[nc * t + c, pl.ds(r, tm // dil, stride=dil), :]
                    dil_refs[3 * pi + t][0, r, :, LANES * c:LANES * (c + 1)] = rows.astype(BF16)


HALF_A = MLA_ROPE // 2
HALF_B = ROPE_DIM // 2
NOPE_LO = LANES // 2 - HALF_A
ROT_ROWS = 2 * HALF_A


def _rope_freqs():
    assert 2 * HALF_B == HALF_A
    inv_a = 1.0 / (jnp.float32(MLA_THETA) ** (jnp.arange(HALF_A, dtype=F32) / HALF_A))
    inv_b = 1.0 / (jnp.float32(ROPE_THETA) ** (jnp.arange(HALF_B, dtype=F32) / HALF_B))
    return jnp.concatenate([inv_a, inv_b, inv_b]).reshape(ROT_ROWS, 1)


def _mla_tile_cols(nope, rope):
    lead = (nope if nope is not None else rope).shape[:-1]
    zeros = lambda n: jnp.zeros(lead + (n,), F32)
    x1, x2 = (rope[..., :HALF_A], rope[..., HALF_A:]) if rope is not None else (zeros(HALF_A), zeros(HALF_A))
    n_lo, n_hi = (nope[..., :NOPE_LO], nope[..., NOPE_LO:]) if nope is not None else (zeros(NOPE_LO), zeros(MLA_NOPE - NOPE_LO))
    return jnp.concatenate([x1, n_lo, x2, n_hi, zeros(HEAD_PAD - MLA_NOPE - MLA_ROPE)], axis=-1)


def _dil_tile_cols(w):
    wh = w.reshape(w.shape[0], DIL_HEADS // 2, 2, DIL_HEAD_DIM)
    a, b = wh[:, :, 0], wh[:, :, 1]
    tile = jnp.concatenate([a[..., :HALF_B], b[..., :HALF_B], a[..., ROPE_DIM:], a[..., HALF_B:ROPE_DIM],
                            b[..., HALF_B:ROPE_DIM], b[..., ROPE_DIM:]], axis=-1)
    return tile.reshape(w.shape[0], DIL_WIDTH)


def _mixer_in(h1, rot_cos, rot_sin, mix_pre_g, w_in, b_gate, q_norm_g, w_uq, kv_norm_g, w_uk, w_uv):
    b, s, _ = h1.shape
    o0 = 0
    o1 = o0 + MLA_Q_RANK
    o2 = o1 + MLA_KV_RANK
    o3 = o2 + MLA_ROPE
    o4 = o3 + 3 * DIL_WIDTH
    w_kr = _mla_tile_cols(None, w_in[:, o2:o3])
    wa = jnp.concatenate([w_in[:, o0:o2], w_kr], axis=1).astype(BF16)
    wd = jnp.concatenate([_dil_tile_cols(w_in[:, o3:o3 + DIL_WIDTH]),
                          _dil_tile_cols(w_in[:, o3 + DIL_WIDTH:o3 + 2 * DIL_WIDTH]),
                          w_in[:, o3 + 2 * DIL_WIDTH:o4]], axis=1).astype(BF16)
    wg = w_in[:, o4:].astype(BF16)
    wuq_h = w_uq.reshape(MLA_Q_RANK, MLA_HEADS, MLA_NOPE + MLA_ROPE)
    wuq = _mla_tile_cols(wuq_h[..., :MLA_NOPE], wuq_h[..., MLA_NOPE:]).reshape(MLA_Q_RANK, -1).astype(BF16)
    wuk = _mla_tile_cols(w_uk.reshape(MLA_KV_RANK, MLA_HEADS, MLA_NOPE), None).reshape(MLA_KV_RANK, -1).astype(BF16)
    wuvt = jnp.pad(w_uv.T.reshape(MLA_HEADS, MLA_V, MLA_KV_RANK), ((0, 0), (0, VT_ROWS - MLA_V), (0, 0))
                   ).reshape(MLA_HEADS * VT_ROWS, MLA_KV_RANK).astype(BF16)
    tm = MIX_TM
    hp = MLA_HEADS * HEAD_PAD
    tok = lambda c: pl.BlockSpec((1, tm, c), lambda bi, i: (bi, i, 0))
    rot = pl.BlockSpec((ROT_ROWS, tm), lambda bi, i: (0, bi * (s // tm) + i))
    out_shapes = (
        jax.ShapeDtypeStruct((b, s, hp), BF16),
        jax.ShapeDtypeStruct((b, s, hp), BF16),
        jax.ShapeDtypeStruct((b, s // FLASH_TK, MLA_HEADS * VT_ROWS, FLASH_TK), BF16),
        jax.ShapeDtypeStruct((b, s, N_BRANCH * D_MODEL), BF16),
    )
    per_chunk = FLASH_TK // tm
    out_specs = (
        tok(hp), tok(hp),
        pl.BlockSpec((1, 1, MLA_HEADS * VT_ROWS, tm), lambda bi, i: (bi, i // per_chunk, 0, i % per_chunk)),
        tok(N_BRANCH * D_MODEL),
    )
    for _, dil in DIL_PATTERNS:
        out_shapes += (jax.ShapeDtypeStruct((b, dil, s // dil, DIL_WIDTH), BF16),) * 3
        out_specs += (pl.BlockSpec((1, dil, tm // dil, DIL_WIDTH), lambda bi, i: (bi, 0, i, 0)),) * 3
    outs = pl.pallas_call(
        _mixer_in_kernel,
        name="mixer_in",
        grid=(b, s // tm),
        in_specs=[
            tok(D_MODEL), rot, rot,
            _const_spec((1, D_MODEL)),
            _const_spec(wa.shape), _const_spec(wd.shape), _const_spec(wg.shape),
            _const_spec((1, N_BRANCH * D_MODEL)),
            _const_spec((1, MLA_Q_RANK)), _const_spec((1, MLA_KV_RANK)),
            _const_spec(wuq.shape), _const_spec(wuk.shape), _const_spec(wuvt.shape),
        ],
        out_specs=out_specs,
        out_shape=out_shapes,
        scratch_shapes=[pltpu.VMEM((3 * DIL_WIDTH // LANES, tm, LANES), F32)],
        compiler_params=pltpu.CompilerParams(dimension_semantics=("arbitrary", "arbitrary"),
                                             vmem_limit_bytes=VMEM_LIMIT),
    )(h1, rot_cos, rot_sin, mix_pre_g.reshape(1, D_MODEL), wa, wd, wg, b_gate.reshape(1, -1),
      q_norm_g.reshape(1, -1), kv_norm_g.reshape(1, -1), wuq, wuk, wuvt)
    q, k, vt, gates = outs[:4]
    dil_qkv = [outs[4 + 3 * i:7 + 3 * i] for i in range(len(DIL_PATTERNS))]
    return q, k, vt, gates, dil_qkv


def _flash_kernel(q_ref, k_ref, vt_ref, o_ref, s0_ref, s1_ref, s2_ref, s3_ref, qt_ref):
    nk = k_ref.shape[1] // FLASH_TK
    nq = q_ref.shape[1] // FLASH_TQ
    assert nk % 4 == 0
    bufs = (s0_ref, s1_ref, s2_ref, s3_ref)

    def scores(t, c, hh, buf):
        cmax = None
        for part in range(FLASH_TK // FLASH_TS):
            k0 = c * FLASH_TK + part * FLASH_TS
            if not isinstance(k0, int):
                k0 = pl.multiple_of(k0, FLASH_TS)
            k = k_ref[0, pl.ds(k0, FLASH_TS), HEAD_PAD * hh:HEAD_PAD * (hh + 1)]
            st = jnp.dot(k, qt_ref[hh, t], preferred_element_type=F32)
            buf[hh, FLASH_TS * part:FLASH_TS * (part + 1)] = st
            pm = jnp.max(st, axis=0, keepdims=True)
            cmax = pm if cmax is None else jnp.maximum(cmax, pm)
        return cmax

    def consume(c, hh, buf, m, cmax, acc):
        m_new = jnp.maximum(m, cmax)
        alpha = jnp.exp2(m - m_new)
        p = jnp.exp2(buf[hh] - m_new).astype(BF16)
        vt = vt_ref[0, c, VT_ROWS * hh:VT_ROWS * (hh + 1), :]
        return m_new, alpha * acc + jnp.dot(vt, p, preferred_element_type=F32)

    def half(c, state, pair_in, pair_out, t_next, c_next):
        new = []
        for hh in range(2):
            m, cm0, cm1, acc = state[hh]
            nxt = [scores(t_next, c_next + i, hh, bufs[pair_out + i]) for i in range(2)]
            m, acc = consume(c, hh, bufs[pair_in], m, cm0, acc)
            m, acc = consume(c + 1, hh, bufs[pair_in + 1], m, cm1, acc)
            new.append((m, nxt[0], nxt[1], acc))
        return tuple(new)

    def tile(t, cmaxes):
        state = tuple((jnp.full((1, FLASH_TQ), -jnp.inf, F32), cmaxes[hh][0], cmaxes[hh][1],
                       jnp.zeros((VT_ROWS, FLASH_TQ), F32)) for hh in range(2))

        def group(j, st):
            c = 4 * j
            return half(c + 2, half(c, st, 0, 2, t, c + 2), 2, 0, t, c + 4)

        state = lax.fori_loop(0, nk // 4 - 1, group, state)
        t_next = jnp.minimum(t + 1, nq - 1)
        state = half(nk - 2, half(nk - 4, state, 0, 2, t, nk - 2), 2, 0, t_next, 0)
        outs = [acc[:MLA_V] / acc[MLA_V:MLA_V + 1] for (_, _, _, acc) in state]
        o_ref[0, pl.ds(pl.multiple_of(t * FLASH_TQ, FLASH_TQ), FLASH_TQ), :] = (
            jnp.concatenate(outs, axis=0).T.astype(BF16))
        return tuple((cm0, cm1) for (_, cm0, cm1, _) in state)

    def transpose_q(t, carry):
        for hh in range(2):
            q = q_ref[0, pl.ds(pl.multiple_of(t * FLASH_TQ, FLASH_TQ), FLASH_TQ), HEAD_PAD * hh:HEAD_PAD * (hh + 1)]
            qt_ref[hh, t] = q.astype(F32).T.astype(BF16)
        return carry

    lax.fori_loop(0, nq, transpose_q, 0)
    first = tuple((scores(0, 0, hh, bufs[0]), scores(0, 1, hh, bufs[1])) for hh in range(2))
    lax.fori_loop(0, nq, tile, first)


def _mla_flash(q, k, vt):
    b, s, _ = q.shape
    nk = s // FLASH_TK
    return pl.pallas_call(
        _flash_kernel,
        name="mla_flash",
        grid=(b, MLA_HEADS // 2),
        in_specs=[
            pl.BlockSpec((1, s, 2 * HEAD_PAD), lambda bi, hp: (bi, 0, hp)),
            pl.BlockSpec((1, s, 2 * HEAD_PAD), lambda bi, hp: (bi, 0, hp)),
            pl.BlockSpec((1, nk, 2 * VT_ROWS, FLASH_TK), lambda bi, hp: (bi, 0, hp, 0)),
        ],
        out_specs=pl.BlockSpec((1, s, 2 * MLA_V), lambda bi, hp: (bi, 0, hp)),
        out_shape=jax.ShapeDtypeStruct((b, s, MLA_WIDTH), BF16),
        scratch_shapes=[pltpu.VMEM((2, FLASH_TK, FLASH_TQ), F32)] * 4
                       + [pltpu.VMEM((2, s // FLASH_TQ, HEAD_PAD, FLASH_TQ), BF16)],
        compiler_params=pltpu.CompilerParams(dimension_semantics=("arbitrary", "arbitrary"),
                                             vmem_limit_bytes=VMEM_LIMIT),
    )(q, k, vt)


def _dilated_kernel(q_ref, kc_ref, kp_ref, kn_ref, vc_ref, vp_ref, vn_ref, o_ref, lse_ref,
                    kw_ref, vt_ref, ot_ref, lt_ref, *sbufs, n_rows):
    lb = q_ref.shape[2]
    l0 = pl.program_id(2) * lb
    wrows = lb + 2 * DIL_HALF
    kw_ref[0:DIL_HALF] = kp_ref[0, 0]
    kw_ref[DIL_HALF:DIL_HALF + lb] = kc_ref[0, 0]
    kw_ref[DIL_HALF + lb:] = kn_ref[0, 0]
    vw = jnp.concatenate([vp_ref[0, 0], vc_ref[0, 0], vn_ref[0, 0]], axis=0).astype(F32)
    vt = vw.T.astype(BF16)
    ones = jnp.ones((VT_ROWS - DIL_HEAD_DIM, wrows), BF16)
    for h in range(DIL_HEADS):
        vt_ref[VT_ROWS * h:VT_ROWS * h + DIL_HEAD_DIM] = vt[DIL_HEAD_DIM * h:DIL_HEAD_DIM * (h + 1)]
        vt_ref[VT_ROWS * h + DIL_HEAD_DIM:VT_ROWS * (h + 1)] = ones

    win = DIL_SUB + 2 * DIL_HALF
    krow = lax.broadcasted_iota(jnp.int32, (win, DIL_SUB), 0)
    qcol = lax.broadcasted_iota(jnp.int32, (win, DIL_SUB), 1)
    in_band = (krow - qcol).astype(jnp.uint32) <= 2 * DIL_HALF
    lane = lax.broadcasted_iota(jnp.int32, (DIL_SUB, LANES), 1)
    odd_lanes = ((lane >= HALF_B) & (lane < ROPE_DIM)) | (lane >= LANES // 2 + HALF_B)
    nbuf = len(sbufs)

    def produce(u):
        j, h = divmod(u, DIL_HEADS)
        ls = slice(LANES * (h // 2), LANES * (h // 2 + 1))
        kpair = kw_ref[j * DIL_SUB:j * DIL_SUB + win, ls]
        qpair = q_ref[0, 0, j * DIL_SUB:(j + 1) * DIL_SUB, ls]
        qh = jnp.where(odd_lanes if h % 2 else ~odd_lanes, qpair, jnp.zeros_like(qpair))
        st = lax.dot_general(kpair, qh, (((1,), (1,)), ((), ())), preferred_element_type=F32)
        kidx = l0 - DIL_HALF + j * DIL_SUB + krow
        st = jnp.where(in_band & (kidx.astype(jnp.uint32) < n_rows), st, NEG)
        sbufs[u % nbuf][...] = st
        return jnp.max(st, axis=0, keepdims=True)

    def consume(u, m):
        j, h = divmod(u, DIL_HEADS)
        p = jnp.exp2(sbufs[u % nbuf][...] - m).astype(BF16)
        r = jnp.dot(vt_ref[VT_ROWS * h:VT_ROWS * (h + 1), j * DIL_SUB:j * DIL_SUB + win], p,
                    preferred_element_type=F32)
        den = r[DIL_HEAD_DIM:DIL_HEAD_DIM + 1]
        hs = slice(DIL_HEAD_DIM * h, DIL_HEAD_DIM * (h + 1))
        ot_ref[hs] = r[:DIL_HEAD_DIM] / den
        lt_ref[h:h + 1] = m + jnp.log2(den)
        if h == DIL_HEADS - 1:
            o_ref[0, 0, j * DIL_SUB:(j + 1) * DIL_SUB, :] = ot_ref[...].T.astype(BF16)
            lse_ref[0, 0, j * DIL_SUB:(j + 1) * DIL_SUB, :] = lt_ref[...].T

    lt_ref[...] = jnp.zeros_like(lt_ref)
    n_units = (lb // DIL_SUB) * DIL_HEADS
    grp = nbuf // 2
    m_cur = [produce(i) for i in range(grp)]
    for u in range(0, n_units, grp):
        m_next = [produce(u + grp + i) for i in range(grp)] if u + grp < n_units else None
        for i in range(grp):
            consume(u + i, m_cur[i])
        m_cur = m_next


def _dilated(dq, dk, dv, dilation):
    b, _, n_rows, w = dq.shape
    lb = min(DIL_LB, n_rows)
    hb = lb // DIL_HALF
    last = n_rows // DIL_HALF - 1
    cur = pl.BlockSpec((1, 1, lb, w), lambda bi, r, l: (bi, r, l, 0))
    prev = pl.BlockSpec((1, 1, DIL_HALF, w), lambda bi, r, l: (bi, r, jnp.maximum(l * hb - 1, 0), 0))
    nxt = pl.BlockSpec((1, 1, DIL_HALF, w), lambda bi, r, l: (bi, r, jnp.minimum((l + 1) * hb, last), 0))
    return pl.pallas_call(
        functools.partial(_dilated_kernel, n_rows=n_rows),
        name=f"dilated_{dilation}",
        grid=(b, dilation, n_rows // lb),
        in_specs=[cur, cur, prev, nxt, cur, prev, nxt],
        out_specs=(cur, pl.BlockSpec((1, 1, lb, LANES), lambda bi, r, l: (bi, r, l, 0))),
        out_shape=(jax.ShapeDtypeStruct((b, dilation, n_rows, w), BF16),
                   jax.ShapeDtypeStruct((b, dilation, n_rows, LANES), F32)),
        scratch_shapes=[pltpu.VMEM((lb + 2 * DIL_HALF, w), BF16),
                        pltpu.VMEM((DIL_HEADS * VT_ROWS, lb + 2 * DIL_HALF), BF16),
                        pltpu.VMEM((w, DIL_SUB), F32),
                        pltpu.VMEM((LANES, DIL_SUB), F32)]
                       + [pltpu.VMEM((DIL_SUB + 2 * DIL_HALF, DIL_SUB), F32)] * (2 * DIL_GROUP),
        compiler_params=pltpu.CompilerParams(dimension_semantics=("arbitrary",) * 3, vmem_limit_bytes=VMEM_LIMIT),
    )(dq, dk, dk, dk, dv, dv, dv)


def _mixer_out_kernel(h_ref, oa_ref, o1_ref, l1_ref, o2_ref, l2_ref, o3_ref, l3_ref, gate_ref,
                      wba_ref, wbb_ref, wo_ref, post_ref, expand_ref, out_ref, *scr):
    tm = h_ref.shape[1]

    def token_major(ref, dil, buf):
        if dil == 1:
            return ref[0, 0].astype(F32)
        nc = ref.shape[-1] // LANES
        for r in range(dil):
            for c in range(nc):
                buf[c, pl.ds(r, tm // dil, stride=dil), :] = ref[0, r, :, LANES * c:LANES * (c + 1)].astype(F32)
        return jnp.concatenate([buf[c] for c in range(nc)], axis=1)

    bufs = iter(scr)
    os_, ls_ = [], []
    for (o_r, l_r), (_, dil) in zip(((o1_ref, l1_ref), (o2_ref, l2_ref), (o3_ref, l3_ref)), DIL_PATTERNS):
        os_.append(token_major(o_r, dil, None if dil == 1 else next(bufs)))
        ls_.append(token_major(l_r, dil, None if dil == 1 else next(bufs)))
    lm = jnp.maximum(jnp.maximum(ls_[0], ls_[1]), ls_[2])
    es = [jnp.exp2(l - lm) for l in ls_]
    inv = 1.0 / (es[0] + es[1] + es[2])
    ob = None
    for e, o in zip(es, os_):
        w = e * inv
        hi = w.astype(BF16)
        lo = (w - hi.astype(F32)).astype(BF16)
        wide = jnp.dot(jnp.concatenate([hi, lo], axis=1), expand_ref[...], preferred_element_type=F32)
        ob = wide * o if ob is None else ob + wide * o
    ya = jnp.dot(oa_ref[0], wba_ref[...], preferred_element_type=F32)
    yb = jnp.dot(ob.astype(BF16), wbb_ref[...], preferred_element_type=F32)
    gate = gate_ref[0].astype(F32)
    merged = gate[:, :D_MODEL] * ya + gate[:, D_MODEL:] * yb
    mix = jnp.dot(merged.astype(BF16), wo_ref[...], preferred_element_type=F32)
    out_ref[0] = h_ref[0] + _rms(mix, post_ref[...])


def _mixer_out(h1, oa, parts, gates, w_branch_a, w_branch_b, w_out, mix_post_g):
    b, s, _ = h1.shape
    tm = MIX_TM
    tok = lambda c: pl.BlockSpec((1, tm, c), lambda bi, i: (bi, i, 0))
    flat, part_specs, scratch = [], [], []
    for (_, dil), pair in zip(DIL_PATTERNS, parts):
        flat += list(pair)
        for width in (DIL_WIDTH, LANES):
            part_specs.append(pl.BlockSpec((1, dil, tm // dil, width), lambda bi, i: (bi, 0, i, 0)))
            if dil > 1:
                scratch.append(pltpu.VMEM((width // LANES, tm, LANES), F32))
    head_of_lane = jnp.arange(DIL_WIDTH) // DIL_HEAD_DIM
    expand = (jnp.arange(2 * LANES)[:, None] % LANES == head_of_lane[None, :]).astype(BF16)
    return pl.pallas_call(
        _mixer_out_kernel,
        name="mixer_out",
        grid=(b, s // tm),
        scratch_shapes=scratch,
        in_specs=[tok(D_MODEL), tok(MLA_WIDTH)] + part_specs + [tok(N_BRANCH * D_MODEL),
                  _const_spec((MLA_WIDTH, D_MODEL)), _const_spec((DIL_WIDTH, D_MODEL)),
                  _const_spec((D_MODEL, D_MODEL)), _const_spec((1, D_MODEL)), _const_spec((2 * LANES, DIL_WIDTH))],
        out_specs=tok(D_MODEL),
        out_shape=jax.ShapeDtypeStruct((b, s, D_MODEL), F32),
        compiler_params=pltpu.CompilerParams(dimension_semantics=("arbitrary", "arbitrary"),
                                             vmem_limit_bytes=VMEM_LIMIT),
    )(h1, oa, *flat, gates, w_branch_a.astype(BF16), w_branch_b.astype(BF16), w_out.astype(BF16),
      mix_post_g.reshape(1, D_MODEL), expand)


def kernel(x, positions, ffn1_pre_g, ffn1_post_g, ffn1_w_gate, ffn1_w_up, ffn1_w_down, mix_pre_g, w_in, b_gate,
           q_norm_g, w_uq, kv_norm_g, w_uk, w_uv, w_branch_a, w_branch_b, w_out, mix_post_g,
           ffn2_pre_g, ffn2_post_g, ffn2_w_gate, ffn2_w_up, ffn2_w_down):
    b, s, d = x.shape
    depth = ffn1_pre_g.shape[0]
    assert all(win // (2 * dil) == DIL_HALF and s % (dil * DIL_SUB) == 0 for win, dil in DIL_PATTERNS)
    h = x
    for l in range(depth):
        h, rot_cos, rot_sin = _ffn(h.reshape(b * s, d), ffn1_pre_g[l], ffn1_post_g[l], ffn1_w_gate[l], ffn1_w_up[l],
                                   ffn1_w_down[l], positions=positions.astype(F32).reshape(1, b * s))
        h = h.reshape(b, s, d)
        q, k, vt, gates, dil_qkv = _mixer_in(h, rot_cos, rot_sin, mix_pre_g[l], w_in[l], b_gate[l], q_norm_g[l],
                                             w_uq[l], kv_norm_g[l], w_uk[l], w_uv[l])
        oa = _mla_flash(q, k, vt)
        parts = [_dilated(*qkv, dil) for qkv, (_, dil) in zip(dil_qkv, DIL_PATTERNS)]
        h = _mixer_out(h, oa, parts, gates, w_branch_a[l], w_branch_b[l], w_out[l], mix_post_g[l])
        h = _ffn(h.reshape(b * s, d), ffn2_pre_g[l], ffn2_post_g[l], ffn2_w_gate[l], ffn2_w_up[l],
                 ffn2_w_down[l]).reshape(b, s, d)
    return h
```

```python
import functools

import jax
import jax.numpy as jnp
from jax import lax
from jax.experimental import pallas as pl
from jax.experimental.pallas import tpu as pltpu

F32 = jnp.float32
BF16 = jnp.bfloat16

D_MODEL = 1024
D_FF = 2816
EPS = 1e-6
MLA_HEADS = 8
MLA_Q_RANK = 384
MLA_KV_RANK = 256
MLA_NOPE = 64
MLA_ROPE = 32
MLA_V = 64
MLA_THETA = 10000.0
MLA_WIDTH = MLA_HEADS * MLA_V
DIL_HEADS = 8
DIL_HEAD_DIM = 64
DIL_PATTERNS = ((128, 1), (512, 4), (2048, 16))
DIL_WIDTH = DIL_HEADS * DIL_HEAD_DIM
ROPE_THETA = 500000.0
ROPE_DIM = DIL_HEAD_DIM // 4
N_BRANCH = 2
NEG = -1e30

LANES = 128
HEAD_PAD = LANES
VMEM_LIMIT = 56 * 1024 * 1024

FFN_TM = 1024
FFN_SPLIT = 2
FFN_FC = 256
MIX_TM = 512
FLASH_TQ = 256
FLASH_QT = 4
FLASH_TK = MIX_TM
FLASH_TS = FLASH_TK // 2
VT_ROWS = MLA_V + 16
LOG2E = 1.4426950408889634
DIL_LB = 2048
DIL_SUB = 256
DIL_GROUP = 4
DIL_HALF = 64


def _rms(x, g):
    ms = jnp.mean(x * x, axis=-1, keepdims=True)
    return x * lax.rsqrt(ms + EPS) * g


def _const_spec(shape):
    nd = len(shape)
    return pl.BlockSpec(shape, lambda *_: (0,) * nd, pipeline_mode=pl.Buffered(1))


def _ffn_kernel(x_ref, pre_ref, post_ref, wg_ref, wu_ref, wd_ref, *rest, with_rotary):
    if with_rotary:
        pos_ref, freq_ref, o_ref, cos_ref, sin_ref, hm_ref = rest
        ang = freq_ref[...] * pos_ref[...]
        cos_ref[...] = jnp.cos(ang)
        sin_ref[...] = jnp.sin(ang)
    else:
        o_ref, hm_ref = rest
    rows = x_ref.shape[0] // FFN_SPLIT
    for r in range(FFN_SPLIT):
        rs = slice(rows * r, rows * (r + 1))
        x = x_ref[rs]
        xn = _rms(x, pre_ref[...]).astype(BF16)
        for c0 in range(0, D_FF, FFN_FC):
            sl = slice(c0, min(c0 + FFN_FC, D_FF))
            g = jnp.dot(xn, wg_ref[:, sl], preferred_element_type=F32)
            u = jnp.dot(xn, wu_ref[:, sl], preferred_element_type=F32)
            hm_ref[rs, sl] = (g * jax.nn.sigmoid(g) * u).astype(BF16)
        f = jnp.dot(hm_ref[rs], wd_ref[...], preferred_element_type=F32)
        o_ref[rs] = x + 0.5 * _rms(f, post_ref[...])


def _ffn(h, pre_g, post_g, w_gate, w_up, w_down, positions=None):
    t = h.shape[0]
    tok = lambda c: pl.BlockSpec((FFN_TM, c), lambda i: (i, 0))
    in_specs = [tok(D_MODEL), _const_spec((1, D_MODEL)), _const_spec((1, D_MODEL)),
                _const_spec((D_MODEL, D_FF)), _const_spec((D_MODEL, D_FF)), _const_spec((D_FF, D_MODEL))]
    args = [h, pre_g.reshape(1, D_MODEL), post_g.reshape(1, D_MODEL),
            w_gate.astype(BF16), w_up.astype(BF16), w_down.astype(BF16)]
    out_specs, out_shape = tok(D_MODEL), jax.ShapeDtypeStruct((t, D_MODEL), F32)
    if positions is not None:
        dense = pl.BlockSpec((ROT_ROWS, FFN_TM), lambda i: (0, i))
        in_specs += [pl.BlockSpec((1, FFN_TM), lambda i: (0, i)), _const_spec((ROT_ROWS, 1))]
        args += [positions, _rope_freqs()]
        out_specs = (out_specs, dense, dense)
        out_shape = (out_shape,) + (jax.ShapeDtypeStruct((ROT_ROWS, t), F32),) * 2
    return pl.pallas_call(
        functools.partial(_ffn_kernel, with_rotary=positions is not None),
        name="ffn",
        grid=(t // FFN_TM,),
        in_specs=in_specs,
        out_specs=out_specs,
        out_shape=out_shape,
        scratch_shapes=[pltpu.VMEM((FFN_TM, D_FF), BF16)],
        compiler_params=pltpu.CompilerParams(dimension_semantics=("arbitrary",), vmem_limit_bytes=VMEM_LIMIT),
    )(*args)


def _rope_tile(x, cos, sin):
    return x * cos + pltpu.roll(x, LANES // 2, 1) * sin


def _mixer_in_kernel(h_ref, cos_ref, sin_ref, pre_ref, wa_ref, wd_ref, wg_ref, bg_ref, qg_ref, kvg_ref,
                     wuq_ref, wuk_ref, wuvt_ref,
                     q_ref, k_ref, vt_ref, gate_ref, *rest):
    dil_refs, dscr_ref = rest[:-1], rest[-1]
    u = _rms(h_ref[0], pre_ref[...]).astype(BF16)
    def token_tile(ref, fill):
        pad = jnp.full((LANES // 2 - HALF_A, ref.shape[1]), fill, F32)
        return jnp.concatenate([ref[0:HALF_A], pad, ref[HALF_A:2 * HALF_A], pad], axis=0).T

    cc = token_tile(cos_ref, 1.0)
    ss = token_tile(sin_ref, 0.0)
    rcc = pltpu.roll(cc, LANES // 2, 1)
    rss = pltpu.roll(ss, LANES // 2, 1)
    low = lax.broadcasted_iota(jnp.int32, cc.shape, 1) < LANES // 2
    cos_a = jnp.where(low, cc, rcc)
    sin_a = jnp.where(low, -ss, rss)
    cos_b = jnp.where(low, rcc, cc)
    sin_b = jnp.where(low, -rss, ss)

    pa = jnp.dot(u, wa_ref[...], preferred_element_type=F32)
    cq = pa[:, :MLA_Q_RANK]
    ckv = pa[:, MLA_Q_RANK:MLA_Q_RANK + MLA_KV_RANK]
    kr = _rope_tile(pa[:, MLA_Q_RANK + MLA_KV_RANK:], cos_a, sin_a)
    qn = _rms(cq, qg_ref[...]).astype(BF16)
    ckvn = _rms(ckv, kvg_ref[...]).astype(BF16)
    q = jnp.dot(qn, wuq_ref[...], preferred_element_type=F32)

    pg = jnp.dot(u, wg_ref[...], preferred_element_type=F32) + bg_ref[...]
    gate_ref[0] = jax.nn.sigmoid(pg).astype(BF16)

    kn = jnp.dot(ckvn, wuk_ref[...], preferred_element_type=F32)
    scale = LOG2E * (MLA_NOPE + MLA_ROPE) ** -0.5
    for h in range(MLA_HEADS):
        sl = slice(HEAD_PAD * h, HEAD_PAD * (h + 1))
        qh = _rope_tile(q[:, sl], cos_a, sin_a)
        q_ref[0, :, sl] = (qh * scale).astype(BF16)
        k_ref[0, :, sl] = (kn[:, sl] + kr).astype(BF16)
    vt = lax.dot_general(wuvt_ref[...], ckvn, (((1,), (1,)), ((), ())), preferred_element_type=F32)
    row = lax.broadcasted_iota(jnp.int32, vt.shape, 0)
    vt_ref[0, 0] = jnp.where(row % VT_ROWS >= MLA_V, 1.0, vt).astype(BF16)

    pd = jnp.dot(u, wd_ref[...], preferred_element_type=F32)
    dscale = LOG2E * DIL_HEAD_DIM ** -0.5
    nc = DIL_WIDTH // LANES
    for c in range(nc):
        sl = slice(LANES * c, LANES * (c + 1))
        ks = slice(DIL_WIDTH + LANES * c, DIL_WIDTH + LANES * (c + 1))
        dscr_ref[c] = _rope_tile(pd[:, sl], cos_b, sin_b) * dscale
        dscr_ref[nc + c] = _rope_tile(pd[:, ks], cos_b, sin_b)
        dscr_ref[2 * nc + c] = pd[:, 2 * DIL_WIDTH + LANES * c:2 * DIL_WIDTH + LANES * (c + 1)]
    tm = dscr_ref.shape[1]
    for pi, (_, dil) in enumerate(DIL_PATTERNS):
        for t in range(3):
            for c in range(nc):
                if dil == 1:
                    dil_refs[3 * pi + t][0, 0, :, LANES * c:LANES * (c + 1)] = dscr_ref[nc * t + c].astype(BF16)
                    continue
                for r in range(dil):
                    rows = dscr_ref[nc * t + c, pl.ds(r, tm // dil, stride=dil), :]
                    dil_refs[3 * pi + t][0, r, :, LANES * c:LANES * (c + 1)] = rows.astype(BF16)


HALF_A = MLA_ROPE // 2
HALF_B = ROPE_DIM // 2
NOPE_LO = LANES // 2 - HALF_A
ROT_ROWS = 2 * HALF_A


def _rope_freqs():
    assert 2 * HALF_B == HALF_A
    inv_a = 1.0 / (jnp.float32(MLA_THETA) ** (jnp.arange(HALF_A, dtype=F32) / HALF_A))
    inv_b = 1.0 / (jnp.float32(ROPE_THETA) ** (jnp.arange(HALF_B, dtype=F32) / HALF_B))
    return jnp.concatenate([inv_a, inv_b, inv_b]).reshape(ROT_ROWS, 1)


def _mla_tile_cols(nope, rope):
    lead = (nope if nope is not None else rope).shape[:-1]
    zeros = lambda n: jnp.zeros(lead + (n,), F32)
    x1, x2 = (rope[..., :HALF_A], rope[..., HALF_A:]) if rope is not None else (zeros(HALF_A), zeros(HALF_A))
    n_lo, n_hi = (nope[..., :NOPE_LO], nope[..., NOPE_LO:]) if nope is not None else (zeros(NOPE_LO), zeros(MLA_NOPE - NOPE_LO))
    return jnp.concatenate([x1, n_lo, x2, n_hi, zeros(HEAD_PAD - MLA_NOPE - MLA_ROPE)], axis=-1)


def _dil_tile_cols(w):
    wh = w.reshape(w.shape[0], DIL_HEADS // 2, 2, DIL_HEAD_DIM)
    a, b = wh[:, :, 0], wh[:, :, 1]
    tile = jnp.concatenate([a[..., :HALF_B], b[..., :HALF_B], a[..., ROPE_DIM:], a[..., HALF_B:ROPE_DIM],
                            b[..., HALF_B:ROPE_DIM], b[..., ROPE_DIM:]], axis=-1)
    return tile.reshape(w.shape[0], DIL_WIDTH)


def _mixer_in(h1, rot_cos, rot_sin, mix_pre_g, w_in, b_gate, q_norm_g, w_uq, kv_norm_g, w_uk, w_uv):
    b, s, _ = h1.shape
    o0 = 0
    o1 = o0 + MLA_Q_RANK
    o2 = o1 + MLA_KV_RANK
    o3 = o2 + MLA_ROPE
    o4 = o3 + 3 * DIL_WIDTH
    w_kr = _mla_tile_cols(None, w_in[:, o2:o3])
    wa = jnp.concatenate([w_in[:, o0:o2], w_kr], axis=1).astype(BF16)
    wd = jnp.concatenate([_dil_tile_cols(w_in[:, o3:o3 + DIL_WIDTH]),
                          _dil_tile_cols(w_in[:, o3 + DIL_WIDTH:o3 + 2 * DIL_WIDTH]),
                          w_in[:, o3 + 2 * DIL_WIDTH:o4]], axis=1).astype(BF16)
    wg = w_in[:, o4:].astype(BF16)
    wuq_h = w_uq.reshape(MLA_Q_RANK, MLA_HEADS, MLA_NOPE + MLA_ROPE)
    wuq = _mla_tile_cols(wuq_h[..., :MLA_NOPE], wuq_h[..., MLA_NOPE:]).reshape(MLA_Q_RANK, -1).astype(BF16)
    wuk = _mla_tile_cols(w_uk.reshape(MLA_KV_RANK, MLA_HEADS, MLA_NOPE), None).reshape(MLA_KV_RANK, -1).astype(BF16)
    wuvt = jnp.pad(w_uv.T.reshape(MLA_HEADS, MLA_V, MLA_KV_RANK), ((0, 0), (0, VT_ROWS - MLA_V), (0, 0))
                   ).reshape(MLA_HEADS * VT_ROWS, MLA_KV_RANK).astype(BF16)
    tm = MIX_TM
    hp = MLA_HEADS * HEAD_PAD
    tok = lambda c: pl.BlockSpec((1, tm, c), lambda bi, i: (bi, i, 0))
    rot = pl.BlockSpec((ROT_ROWS, tm), lambda bi, i: (0, bi * (s // tm) + i))
    out_shapes = (
        jax.ShapeDtypeStruct((b, s, hp), BF16),
        jax.ShapeDtypeStruct((b, s, hp), BF16),
        jax.ShapeDtypeStruct((b, s // FLASH_TK, MLA_HEADS * VT_ROWS, FLASH_TK), BF16),
        jax.ShapeDtypeStruct((b, s, N_BRANCH * D_MODEL), BF16),
    )
    per_chunk = FLASH_TK // tm
    out_specs = (
        tok(hp), tok(hp),
        pl.BlockSpec((1, 1, MLA_HEADS * VT_ROWS, tm), lambda bi, i: (bi, i // per_chunk, 0, i % per_chunk)),
        tok(N_BRANCH * D_MODEL),
    )
    for _, dil in DIL_PATTERNS:
        out_shapes += (jax.ShapeDtypeStruct((b, dil, s // dil, DIL_WIDTH), BF16),) * 3
        out_specs += (pl.BlockSpec((1, dil, tm // dil, DIL_WIDTH), lambda bi, i: (bi, 0, i, 0)),) * 3
    outs = pl.pallas_call(
        _mixer_in_kernel,
        name="mixer_in",
        grid=(b, s // tm),
        in_specs=[
            tok(D_MODEL), rot, rot,
            _const_spec((1, D_MODEL)),
            _const_spec(wa.shape), _const_spec(wd.shape), _const_spec(wg.shape),
            _const_spec((1, N_BRANCH * D_MODEL)),
            _const_spec((1, MLA_Q_RANK)), _const_spec((1, MLA_KV_RANK)),
            _const_spec(wuq.shape), _const_spec(wuk.shape), _const_spec(wuvt.shape),
        ],
        out_specs=out_specs,
        out_shape=out_shapes,
        scratch_shapes=[pltpu.VMEM((3 * DIL_WIDTH // LANES, tm, LANES), F32)],
        compiler_params=pltpu.CompilerParams(dimension_semantics=("arbitrary", "arbitrary"),
                                             vmem_limit_bytes=VMEM_LIMIT),
    )(h1, rot_cos, rot_sin, mix_pre_g.reshape(1, D_MODEL), wa, wd, wg, b_gate.reshape(1, -1),
      q_norm_g.reshape(1, -1), kv_norm_g.reshape(1, -1), wuq, wuk, wuvt)
    q, k, vt, gates = outs[:4]
    dil_qkv = [outs[4 + 3 * i:7 + 3 * i] for i in range(len(DIL_PATTERNS))]
    return q, k, vt, gates, dil_qkv


def _flash_kernel(q_ref, k_ref, vt_ref, o_ref, s0_ref, s1_ref, s2_ref, s3_ref, qt_ref):
    nk = k_ref.shape[1] // FLASH_TK
    nq = q_ref.shape[1] // FLASH_TQ
    assert nk % 4 == 0
    bufs = (s0_ref, s1_ref, s2_ref, s3_ref)

    chains = [(tt, hh) for tt in range(FLASH_QT) for hh in range(2)]

    def scores(t, c, ch, buf):
        tt, hh = chains[ch]
        cmax = None
        for part in range(FLASH_TK // FLASH_TS):
            k0 = c * FLASH_TK + part * FLASH_TS
            if not isinstance(k0, int):
                k0 = pl.multiple_of(k0, FLASH_TS)
            k = k_ref[0, pl.ds(k0, FLASH_TS), HEAD_PAD * hh:HEAD_PAD * (hh + 1)]
            st = jnp.dot(k, qt_ref[hh, t * FLASH_QT + tt], preferred_element_type=F32)
            buf[ch, FLASH_TS * part:FLASH_TS * (part + 1)] = st
            pm = jnp.max(st, axis=0, keepdims=True)
            cmax = pm if cmax is None else jnp.maximum(cmax, pm)
        return cmax

    def consume(c, ch, buf, m, cmax, acc):
        hh = chains[ch][1]
        m_new = jnp.maximum(m, cmax)
        alpha = jnp.exp2(m - m_new)
        p = jnp.exp2(buf[ch] - m_new).astype(BF16)
        vt = vt_ref[0, c, VT_ROWS * hh:VT_ROWS * (hh + 1), :]
        return m_new, alpha * acc + jnp.dot(vt, p, preferred_element_type=F32)

    def half(c, state, pair_in, pair_out, t_next, c_next):
        new = []
        for ch in range(len(chains)):
            m, cm0, cm1, acc = state[ch]
            nxt = [scores(t_next, c_next + i, ch, bufs[pair_out + i]) for i in range(2)]
            m, acc = consume(c, ch, bufs[pair_in], m, cm0, acc)
            m, acc = consume(c + 1, ch, bufs[pair_in + 1], m, cm1, acc)
            new.append((m, nxt[0], nxt[1], acc))
        return tuple(new)

    def tile(t, cmaxes):
        state = tuple((jnp.full((1, FLASH_TQ), -jnp.inf, F32), cmaxes[ch][0], cmaxes[ch][1],
                       jnp.zeros((VT_ROWS, FLASH_TQ), F32)) for ch in range(len(chains)))

        def group(j, st):
            c = 4 * j
            return half(c + 2, half(c, st, 0, 2, t, c + 2), 2, 0, t, c + 4)

        state = lax.fori_loop(0, nk // 4 - 1, group, state)
        t_next = jnp.minimum(t + 1, nq // FLASH_QT - 1)
        state = half(nk - 2, half(nk - 4, state, 0, 2, t, nk - 2), 2, 0, t_next, 0)
        for tt in range(FLASH_QT):
            outs = [acc[:MLA_V] / acc[MLA_V:MLA_V + 1] for (_, _, _, acc) in state[2 * tt:2 * tt + 2]]
            row0 = pl.multiple_of((t * FLASH_QT + tt) * FLASH_TQ, FLASH_TQ)
            o_ref[0, pl.ds(row0, FLASH_TQ), :] = jnp.concatenate(outs, axis=0).T.astype(BF16)
        return tuple((cm0, cm1) for (_, cm0, cm1, _) in state)

    def transpose_q(t, carry):
        for hh in range(2):
            q = q_ref[0, pl.ds(pl.multiple_of(t * FLASH_TQ, FLASH_TQ), FLASH_TQ), HEAD_PAD * hh:HEAD_PAD * (hh + 1)]
            qt_ref[hh, t] = q.astype(F32).T.astype(BF16)
        return carry

    lax.fori_loop(0, nq, transpose_q, 0)
    first = tuple((scores(0, 0, ch, bufs[0]), scores(0, 1, ch, bufs[1])) for ch in range(len(chains)))
    lax.fori_loop(0, nq // FLASH_QT, tile, first)


def _mla_flash(q, k, vt):
    b, s, _ = q.shape
    nk = s // FLASH_TK
    return pl.pallas_call(
        _flash_kernel,
        name="mla_flash",
        grid=(b, MLA_HEADS // 2),
        in_specs=[
            pl.BlockSpec((1, s, 2 * HEAD_PAD), lambda bi, hp: (bi, 0, hp)),
            pl.BlockSpec((1, s, 2 * HEAD_PAD), lambda bi, hp: (bi, 0, hp)),
            pl.BlockSpec((1, nk, 2 * VT_ROWS, FLASH_TK), lambda bi, hp: (bi, 0, hp, 0)),
        ],
        out_specs=pl.BlockSpec((1, s, 2 * MLA_V), lambda bi, hp: (bi, 0, hp)),
        out_shape=jax.ShapeDtypeStruct((b, s, MLA_WIDTH), BF16),
        scratch_shapes=[pltpu.VMEM((2 * FLASH_QT, FLASH_TK, FLASH_TQ), F32)] * 4
                       + [pltpu.VMEM((2, s // FLASH_TQ, HEAD_PAD, FLASH_TQ), BF16)],
        compiler_params=pltpu.CompilerParams(dimension_semantics=("arbitrary", "arbitrary"),
                                             vmem_limit_bytes=VMEM_LIMIT),
    )(q, k, vt)


def _dilated_kernel(q_ref, kc_ref, kp_ref, kn_ref, vc_ref, vp_ref, vn_ref, o_ref, lse_ref,
                    kw_ref, vt_ref, ot_ref, lt_ref, *sbufs, n_rows):
    lb = q_ref.shape[2]
    l0 = pl.program_id(2) * lb
    wrows = lb + 2 * DIL_HALF
    kw_ref[0:DIL_HALF] = kp_ref[0, 0]
    kw_ref[DIL_HALF:DIL_HALF + lb] = kc_ref[0, 0]
    kw_ref[DIL_HALF + lb:] = kn_ref[0, 0]
    vw = jnp.concatenate([vp_ref[0, 0], vc_ref[0, 0], vn_ref[0, 0]], axis=0).astype(F32)
    vt = vw.T.astype(BF16)
    ones = jnp.ones((VT_ROWS - DIL_HEAD_DIM, wrows), BF16)
    for h in range(DIL_HEADS):
        vt_ref[VT_ROWS * h:VT_ROWS * h + DIL_HEAD_DIM] = vt[DIL_HEAD_DIM * h:DIL_HEAD_DIM * (h + 1)]
        vt_ref[VT_ROWS * h + DIL_HEAD_DIM:VT_ROWS * (h + 1)] = ones

    win = DIL_SUB + 2 * DIL_HALF
    krow = lax.broadcasted_iota(jnp.int32, (win, DIL_SUB), 0)
    qcol = lax.broadcasted_iota(jnp.int32, (win, DIL_SUB), 1)
    in_band = (krow - qcol).astype(jnp.uint32) <= 2 * DIL_HALF
    lane = lax.broadcasted_iota(jnp.int32, (DIL_SUB, LANES), 1)
    odd_lanes = ((lane >= HALF_B) & (lane < ROPE_DIM)) | (lane >= LANES // 2 + HALF_B)
    nbuf = len(sbufs)

    def produce(u):
        j, h = divmod(u, DIL_HEADS)
        ls = slice(LANES * (h // 2), LANES * (h // 2 + 1))
        kpair = kw_ref[j * DIL_SUB:j * DIL_SUB + win, ls]
        qpair = q_ref[0, 0, j * DIL_SUB:(j + 1) * DIL_SUB, ls]
        qh = jnp.where(odd_lanes if h % 2 else ~odd_lanes, qpair, jnp.zeros_like(qpair))
        st = lax.dot_general(kpair, qh, (((1,), (1,)), ((), ())), preferred_element_type=F32)
        kidx = l0 - DIL_HALF + j * DIL_SUB + krow
        st = jnp.where(in_band & (kidx.astype(jnp.uint32) < n_rows), st, NEG)
        sbufs[u % nbuf][...] = st
        return jnp.max(st, axis=0, keepdims=True)

    def consume(u, m):
        j, h = divmod(u, DIL_HEADS)
        p = jnp.exp2(sbufs[u % nbuf][...] - m).astype(BF16)
        r = jnp.dot(vt_ref[VT_ROWS * h:VT_ROWS * (h + 1), j * DIL_SUB:j * DIL_SUB + win], p,
                    preferred_element_type=F32)
        den = r[DIL_HEAD_DIM:DIL_HEAD_DIM + 1]
        hs = slice(DIL_HEAD_DIM * h, DIL_HEAD_DIM * (h + 1))
        ot_ref[hs] = r[:DIL_HEAD_DIM] / den
        lt_ref[h:h + 1] = m + jnp.log2(den)
        if h == DIL_HEADS - 1:
            o_ref[0, 0, j * DIL_SUB:(j + 1) * DIL_SUB, :] = ot_ref[...].T.astype(BF16)
            lse_ref[0, 0, j * DIL_SUB:(j + 1) * DIL_SUB, :] = lt_ref[...].T

    lt_ref[...] = jnp.zeros_like(lt_ref)
    n_units = (lb // DIL_SUB) * DIL_HEADS
    grp = nbuf // 2
    m_cur = [produce(i) for i in range(grp)]
    for u in range(0, n_units, grp):
        m_next = [produce(u + grp + i) for i in range(grp)] if u + grp < n_units else None
        for i in range(grp):
            consume(u + i, m_cur[i])
        m_cur = m_next


def _dilated(dq, dk, dv, dilation):
    b, _, n_rows, w = dq.shape
    lb = min(DIL_LB, n_rows)
    hb = lb // DIL_HALF
    last = n_rows // DIL_HALF - 1
    cur = pl.BlockSpec((1, 1, lb, w), lambda bi, r, l: (bi, r, l, 0))
    prev = pl.BlockSpec((1, 1, DIL_HALF, w), lambda bi, r, l: (bi, r, jnp.maximum(l * hb - 1, 0), 0))
    nxt = pl.BlockSpec((1, 1, DIL_HALF, w), lambda bi, r, l: (bi, r, jnp.minimum((l + 1) * hb, last), 0))
    return pl.pallas_call(
        functools.partial(_dilated_kernel, n_rows=n_rows),
        name=f"dilated_{dilation}",
        grid=(b, dilation, n_rows // lb),
        in_specs=[cur, cur, prev, nxt, cur, prev, nxt],
        out_specs=(cur, pl.BlockSpec((1, 1, lb, LANES), lambda bi, r, l: (bi, r, l, 0))),
        out_shape=(jax.ShapeDtypeStruct((b, dilation, n_rows, w), BF16),
                   jax.ShapeDtypeStruct((b, dilation, n_rows, LANES), F32)),
        scratch_shapes=[pltpu.VMEM((lb + 2 * DIL_HALF, w), BF16),
                        pltpu.VMEM((DIL_HEADS * VT_ROWS, lb + 2 * DIL_HALF), BF16),
                        pltpu.VMEM((w, DIL_SUB), F32),
                        pltpu.VMEM((LANES, DIL_SUB), F32)]
                       + [pltpu.VMEM((DIL_SUB + 2 * DIL_HALF, DIL_SUB), F32)] * (2 * DIL_GROUP),
        compiler_params=pltpu.CompilerParams(dimension_semantics=("arbitrary",) * 3, vmem_limit_bytes=VMEM_LIMIT),
    )(dq, dk, dk, dk, dv, dv, dv)


def _mixer_out_kernel(h_ref, oa_ref, o1_ref, l1_ref, o2_ref, l2_ref, o3_ref, l3_ref, gate_ref,
                      wba_ref, wbb_ref, wo_ref, post_ref, expand_ref, out_ref, *scr):
    tm = h_ref.shape[1]

    def token_major(ref, dil, buf):
        if dil == 1:
            return ref[0, 0].astype(F32)
        nc = ref.shape[-1] // LANES
        for r in range(dil):
            for c in range(nc):
                buf[c, pl.ds(r, tm // dil, stride=dil), :] = ref[0, r, :, LANES * c:LANES * (c + 1)].astype(F32)
        return jnp.concatenate([buf[c] for c in range(nc)], axis=1)

    bufs = iter(scr)
    os_, ls_ = [], []
    for (o_r, l_r), (_, dil) in zip(((o1_ref, l1_ref), (o2_ref, l2_ref), (o3_ref, l3_ref)), DIL_PATTERNS):
        os_.append(token_major(o_r, dil, None if dil == 1 else next(bufs)))
        ls_.append(token_major(l_r, dil, None if dil == 1 else next(bufs)))
    lm = jnp.maximum(jnp.maximum(ls_[0], ls_[1]), ls_[2])
    es = [jnp.exp2(l - lm) for l in ls_]
    inv = 1.0 / (es[0] + es[1] + es[2])
    ob = None
    for e, o in zip(es, os_):
        w = e * inv
        hi = w.astype(BF16)
        lo = (w - hi.astype(F32)).astype(BF16)
        wide = jnp.dot(jnp.concatenate([hi, lo], axis=1), expand_ref[...], preferred_element_type=F32)
        ob = wide * o if ob is None else ob + wide * o
    ya = jnp.dot(oa_ref[0], wba_ref[...], preferred_element_type=F32)
    yb = jnp.dot(ob.astype(BF16), wbb_ref[...], preferred_element_type=F32)
    gate = gate_ref[0].astype(F32)
    merged = gate[:, :D_MODEL] * ya + gate[:, D_MODEL:] * yb
    mix = jnp.dot(merged.astype(BF16), wo_ref[...], preferred_element_type=F32)
    out_ref[0] = h_ref[0] + _rms(mix, post_ref[...])


def _mixer_out(h1, oa, parts, gates, w_branch_a, w_branch_b, w_out, mix_post_g):
    b, s, _ = h1.shape
    tm = MIX_TM
    tok = lambda c: pl.BlockSpec((1, tm, c), lambda bi, i: (bi, i, 0))
    flat, part_specs, scratch = [], [], []
    for (_, dil), pair in zip(DIL_PATTERNS, parts):
        flat += list(pair)
        for width in (DIL_WIDTH, LANES):
            part_specs.append(pl.BlockSpec((1, dil, tm // dil, width), lambda bi, i: (bi, 0, i, 0)))
            if dil > 1:
                scratch.append(pltpu.VMEM((width // LANES, tm, LANES), F32))
    head_of_lane = jnp.arange(DIL_WIDTH) // DIL_HEAD_DIM
    expand = (jnp.arange(2 * LANES)[:, None] % LANES == head_of_lane[None, :]).astype(BF16)
    return pl.pallas_call(
        _mixer_out_kernel,
        name="mixer_out",
        grid=(b, s // tm),
        scratch_shapes=scratch,
        in_specs=[tok(D_MODEL), tok(MLA_WIDTH)] + part_specs + [tok(N_BRANCH * D_MODEL),
                  _const_spec((MLA_WIDTH, D_MODEL)), _const_spec((DIL_WIDTH, D_MODEL)),
                  _const_spec((D_MODEL, D_MODEL)), _const_spec((1, D_MODEL)), _const_spec((2 * LANES, DIL_WIDTH))],
        out_specs=tok(D_MODEL),
        out_shape=jax.ShapeDtypeStruct((b, s, D_MODEL), F32),
        compiler_params=pltpu.CompilerParams(dimension_semantics=("arbitrary", "arbitrary"),
                                             vmem_limit_bytes=VMEM_LIMIT),
    )(h1, oa, *flat, gates, w_branch_a.astype(BF16), w_branch_b.astype(BF16), w_out.astype(BF16),
      mix_post_g.reshape(1, D_MODEL), expand)


def kernel(x, positions, ffn1_pre_g, ffn1_post_g, ffn1_w_gate, ffn1_w_up, ffn1_w_down, mix_pre_g, w_in, b_gate,
           q_norm_g, w_uq, kv_norm_g, w_uk, w_uv, w_branch_a, w_branch_b, w_out, mix_post_g,
           ffn2_pre_g, ffn2_post_g, ffn2_w_gate, ffn2_w_up, ffn2_w_down):
    b, s, d = x.shape
    depth = ffn1_pre_g.shape[0]
    assert all(win // (2 * dil) == DIL_HALF and s % (dil * DIL_SUB) == 0 for win, dil in DIL_PATTERNS)
    h = x
    for l in range(depth):
        h, rot_cos, rot_sin = _ffn(h.reshape(b * s, d), ffn1_pre_g[l], ffn1_post_g[l], ffn1_w_gate[l], ffn1_w_up[l],
                                   ffn1_w_down[l], positions=positions.astype(F32).reshape(1, b * s))
        h = h.reshape(b, s, d)
        q, k, vt, gates, dil_qkv = _mixer_in(h, rot_cos, rot_sin, mix_pre_g[l], w_in[l], b_gate[l], q_norm_g[l],
                                             w_uq[l], kv_norm_g[l], w_uk[l], w_uv[l])
        oa = _mla_flash(q, k, vt)
        parts = [_dilated(*qkv, dil) for qkv, (_, dil) in zip(dil_qkv, DIL_PATTERNS)]
        h = _mixer_out(h, oa, parts, gates, w_branch_a[l], w_branch_b[l], w_out[l], mix_post_g[l])
        h = _ffn(h.reshape(b * s, d), ffn2_pre_g[l], ffn2_post_g[l], ffn2_w_gate[l], ffn2_w_up[l],
                 ffn2_w_down[l]).reshape(b, s, d)
    return h
```

```python
import functools

import jax
import jax.numpy as jnp
from jax import lax
from jax.experimental import pallas as pl
from jax.experimental.pallas import tpu as pltpu

F32 = jnp.float32
BF16 = jnp.bfloat16

D_MODEL = 1024
D_FF = 2816
EPS = 1e-6
MLA_HEADS = 8
MLA_Q_RANK = 384
MLA_KV_RANK = 256
MLA_NOPE = 64
MLA_ROPE = 32
MLA_V = 64
MLA_THETA = 10000.0
MLA_WIDTH = MLA_HEADS * MLA_V
DIL_HEADS = 8
DIL_HEAD_DIM = 64
DIL_PATTERNS = ((128, 1), (512, 4), (2048, 16))
DIL_WIDTH = DIL_HEADS * DIL_HEAD_DIM
ROPE_THETA = 500000.0
ROPE_DIM = DIL_HEAD_DIM // 4
N_BRANCH = 2
NEG = -1e30

LANES = 128
HEAD_PAD = LANES
VMEM_LIMIT = 56 * 1024 * 1024

FFN_TM = 1024
FFN_SPLIT = 2
FFN_FC = 256
MIX_TM = 512
FLASH_TQ = 256
FLASH_QT = 4
FLASH_TK = MIX_TM
FLASH_TS = FLASH_TK // 2
VT_ROWS = MLA_V + 16
LOG2E = 1.4426950408889634
DIL_LB = 2048
DIL_SUB = 256
DIL_GROUP = 4
DIL_HALF = 64


def _rms(x, g):
    ms = jnp.mean(x * x, axis=-1, keepdims=True)
    return x * lax.rsqrt(ms + EPS) * g


def _const_spec(shape):
    nd = len(shape)
    return pl.BlockSpec(shape, lambda *_: (0,) * nd, pipeline_mode=pl.Buffered(1))


def _ffn_kernel(x_ref, pre_ref, post_ref, wg_ref, wu_ref, wd_ref, *rest, with_rotary):
    if with_rotary:
        pos_ref, freq_ref, o_ref, cos_ref, sin_ref, hm_ref = rest
        ang = freq_ref[...] * pos_ref[...]
        cos_ref[...] = jnp.cos(ang)
        sin_ref[...] = jnp.sin(ang)
    else:
        o_ref, hm_ref = rest
    rows = x_ref.shape[0] // FFN_SPLIT
    for r in range(FFN_SPLIT):
        rs = slice(rows * r, rows * (r + 1))
        x = x_ref[rs]
        xn = _rms(x, pre_ref[...]).astype(BF16)
        for c0 in range(0, D_FF, FFN_FC):
            sl = slice(c0, min(c0 + FFN_FC, D_FF))
            g = jnp.dot(xn, wg_ref[:, sl], preferred_element_type=F32)
            u = jnp.dot(xn, wu_ref[:, sl], preferred_element_type=F32)
            hm_ref[rs, sl] = (g * jax.nn.sigmoid(g) * u).astype(BF16)
        f = jnp.dot(hm_ref[rs], wd_ref[...], preferred_element_type=F32)
        o_ref[rs] = x + 0.5 * _rms(f, post_ref[...])


def _ffn(h, pre_g, post_g, w_gate, w_up, w_down, positions=None):
    t = h.shape[0]
    tok = lambda c: pl.BlockSpec((FFN_TM, c), lambda i: (i, 0))
    in_specs = [tok(D_MODEL), _const_spec((1, D_MODEL)), _const_spec((1, D_MODEL)),
                _const_spec((D_MODEL, D_FF)), _const_spec((D_MODEL, D_FF)), _const_spec((D_FF, D_MODEL))]
    args = [h, pre_g.reshape(1, D_MODEL), post_g.reshape(1, D_MODEL),
            w_gate.astype(BF16), w_up.astype(BF16), w_down.astype(BF16)]
    out_specs, out_shape = tok(D_MODEL), jax.ShapeDtypeStruct((t, D_MODEL), F32)
    if positions is not None:
        dense = pl.BlockSpec((ROT_ROWS, FFN_TM), lambda i: (0, i))
        in_specs += [pl.BlockSpec((1, FFN_TM), lambda i: (0, i)), _const_spec((ROT_ROWS, 1))]
        args += [positions, _rope_freqs()]
        out_specs = (out_specs, dense, dense)
        out_shape = (out_shape,) + (jax.ShapeDtypeStruct((ROT_ROWS, t), F32),) * 2
    return pl.pallas_call(
        functools.partial(_ffn_kernel, with_rotary=positions is not None),
        name="ffn",
        grid=(t // FFN_TM,),
        in_specs=in_specs,
        out_specs=out_specs,
        out_shape=out_shape,
        scratch_shapes=[pltpu.VMEM((FFN_TM, D_FF), BF16)],
        compiler_params=pltpu.CompilerParams(dimension_semantics=("arbitrary",), vmem_limit_bytes=VMEM_LIMIT),
    )(*args)


def _rope_tile(x, cos, sin):
    return x * cos + pltpu.roll(x, LANES // 2, 1) * sin


def _mixer_in_kernel(h_ref, cos_ref, sin_ref, pre_ref, wa_ref, wd_ref, wg_ref, bg_ref, qg_ref, kvg_ref,
                     wuq_ref, wuk_ref, wuvt_ref,
                     q_ref, k_ref, vt_ref, gate_ref, *rest):
    dil_refs, dscr_ref = rest[:-1], rest[-1]
    u = _rms(h_ref[0], pre_ref[...]).astype(BF16)
    def token_tile(ref, fill):
        pad = jnp.full((LANES // 2 - HALF_A, ref.shape[1]), fill, F32)
        return jnp.concatenate([ref[0:HALF_A], pad, ref[HALF_A:2 * HALF_A], pad], axis=0).T

    cc = token_tile(cos_ref, 1.0)
    ss = token_tile(sin_ref, 0.0)
    rcc = pltpu.roll(cc, LANES // 2, 1)
    rss = pltpu.roll(ss, LANES // 2, 1)
    low = lax.broadcasted_iota(jnp.int32, cc.shape, 1) < LANES // 2
    cos_a = jnp.where(low, cc, rcc)
    sin_a = jnp.where(low, -ss, rss)
    cos_b = jnp.where(low, rcc, cc)
    sin_b = jnp.where(low, -rss, ss)

    pa = jnp.dot(u, wa_ref[...], preferred_element_type=F32)
    cq = pa[:, :MLA_Q_RANK]
    ckv = pa[:, MLA_Q_RANK:MLA_Q_RANK + MLA_KV_RANK]
    kr = _rope_tile(pa[:, MLA_Q_RANK + MLA_KV_RANK:], cos_a, sin_a)
    qn = _rms(cq, qg_ref[...]).astype(BF16)
    ckvn = _rms(ckv, kvg_ref[...]).astype(BF16)
    q = jnp.dot(qn, wuq_ref[...], preferred_element_type=F32)

    pg = jnp.dot(u, wg_ref[...], preferred_element_type=F32) + bg_ref[...]
    gate_ref[0] = jax.nn.sigmoid(pg).astype(BF16)

    kn = jnp.dot(ckvn, wuk_ref[...], preferred_element_type=F32)
    scale = LOG2E * (MLA_NOPE + MLA_ROPE) ** -0.5
    for h in range(MLA_HEADS):
        sl = slice(HEAD_PAD * h, HEAD_PAD * (h + 1))
        qh = _rope_tile(q[:, sl], cos_a, sin_a)
        q_ref[0, :, sl] = (qh * scale).astype(BF16)
        k_ref[0, :, sl] = (kn[:, sl] + kr).astype(BF16)
    vt = lax.dot_general(wuvt_ref[...], ckvn, (((1,), (1,)), ((), ())), preferred_element_type=F32)
    row = lax.broadcasted_iota(jnp.int32, vt.shape, 0)
    vt_ref[0, 0] = jnp.where(row % VT_ROWS >= MLA_V, 1.0, vt).astype(BF16)

    pd = jnp.dot(u, wd_ref[...], preferred_element_type=F32)
    dscale = LOG2E * DIL_HEAD_DIM ** -0.5
    nc = DIL_WIDTH // LANES
    for c in range(nc):
        sl = slice(LANES * c, LANES * (c + 1))
        ks = slice(DIL_WIDTH + LANES * c, DIL_WIDTH + LANES * (c + 1))
        dscr_ref[c] = _rope_tile(pd[:, sl], cos_b, sin_b) * dscale
        dscr_ref[nc + c] = _rope_tile(pd[:, ks], cos_b, sin_b)
        dscr_ref[2 * nc + c] = pd[:, 2 * DIL_WIDTH + LANES * c:2 * DIL_WIDTH + LANES * (c + 1)]
    tm = dscr_ref.shape[1]
    for pi, (_, dil) in enumerate(DIL_PATTERNS):
        for t in range(3):
            for c in range(nc):
                if dil == 1:
                    dil_refs[3 * pi + t][0, 0, :, LANES * c:LANES * (c + 1)] = dscr_ref[nc * t + c].astype(BF16)
                    continue
                for r in range(dil):
                    rows = dscr_ref[nc * t + c, pl.ds(r, tm // dil, stride=dil), :]
                    dil_refs[3 * pi + t][0, r, :, LANES * c:LANES * (c + 1)] = rows.astype(BF16)


HALF_A = MLA_ROPE // 2
HALF_B = ROPE_DIM // 2
NOPE_LO = LANES // 2 - HALF_A
ROT_ROWS = 2 * HALF_A


def _rope_freqs():
    assert 2 * HALF_B == HALF_A
    inv_a = 1.0 / (jnp.float32(MLA_THETA) ** (jnp.arange(HALF_A, dtype=F32) / HALF_A))
    inv_b = 1.0 / (jnp.float32(ROPE_THETA) ** (jnp.arange(HALF_B, dtype=F32) / HALF_B))
    return jnp.concatenate([inv_a, inv_b, inv_b]).reshape(ROT_ROWS, 1)


def _mla_tile_cols(nope, rope):
    lead = (nope if nope is not None else rope).shape[:-1]
    zeros = lambda n: jnp.zeros(lead + (n,), F32)
    x1, x2 = (rope[..., :HALF_A], rope[..., HALF_A:]) if rope is not None else (zeros(HALF_A), zeros(HALF_A))
    n_lo, n_hi = (nope[..., :NOPE_LO], nope[..., NOPE_LO:]) if nope is not None else (zeros(NOPE_LO), zeros(MLA_NOPE - NOPE_LO))
    return jnp.concatenate([x1, n_lo, x2, n_hi, zeros(HEAD_PAD - MLA_NOPE - MLA_ROPE)], axis=-1)


def _dil_tile_cols(w):
    wh = w.reshape(w.shape[0], DIL_HEADS // 2, 2, DIL_HEAD_DIM)
    a, b = wh[:, :, 0], wh[:, :, 1]
    tile = jnp.concatenate([a[..., :HALF_B], b[..., :HALF_B], a[..., ROPE_DIM:], a[..., HALF_B:ROPE_DIM],
                            b[..., HALF_B:ROPE_DIM], b[..., ROPE_DIM:]], axis=-1)
    return tile.reshape(w.shape[0], DIL_WIDTH)


def _mixer_in(h1, rot_cos, rot_sin, mix_pre_g, w_in, b_gate, q_norm_g, w_uq, kv_norm_g, w_uk, w_uv):
    b, s, _ = h1.shape
    o0 = 0
    o1 = o0 + MLA_Q_RANK
    o2 = o1 + MLA_KV_RANK
    o3 = o2 + MLA_ROPE
    o4 = o3 + 3 * DIL_WIDTH
    w_kr = _mla_tile_cols(None, w_in[:, o2:o3])
    wa = jnp.concatenate([w_in[:, o0:o2], w_kr], axis=1).astype(BF16)
    wd = jnp.concatenate([_dil_tile_cols(w_in[:, o3:o3 + DIL_WIDTH]),
                          _dil_tile_cols(w_in[:, o3 + DIL_WIDTH:o3 + 2 * DIL_WIDTH]),
                          w_in[:, o3 + 2 * DIL_WIDTH:o4]], axis=1).astype(BF16)
    wg = w_in[:, o4:].astype(BF16)
    wuq_h = w_uq.reshape(MLA_Q_RANK, MLA_HEADS, MLA_NOPE + MLA_ROPE)
    wuq = _mla_tile_cols(wuq_h[..., :MLA_NOPE], wuq_h[..., MLA_NOPE:]).reshape(MLA_Q_RANK, -1).astype(BF16)
    wuk = _mla_tile_cols(w_uk.reshape(MLA_KV_RANK, MLA_HEADS, MLA_NOPE), None).reshape(MLA_KV_RANK, -1).astype(BF16)
    wuvt = jnp.pad(w_uv.T.reshape(MLA_HEADS, MLA_V, MLA_KV_RANK), ((0, 0), (0, VT_ROWS - MLA_V), (0, 0))
                   ).reshape(MLA_HEADS * VT_ROWS, MLA_KV_RANK).astype(BF16)
    tm = MIX_TM
    hp = MLA_HEADS * HEAD_PAD
    tok = lambda c: pl.BlockSpec((1, tm, c), lambda bi, i: (bi, i, 0))
    rot = pl.BlockSpec((ROT_ROWS, tm), lambda bi, i: (0, bi * (s // tm) + i))
    out_shapes = (
        jax.ShapeDtypeStruct((b, s, hp), BF16),
        jax.ShapeDtypeStruct((b, s, hp), BF16),
        jax.ShapeDtypeStruct((b, s // FLASH_TK, MLA_HEADS * VT_ROWS, FLASH_TK), BF16),
        jax.ShapeDtypeStruct((b, s, N_BRANCH * D_MODEL), BF16),
    )
    per_chunk = FLASH_TK // tm
    out_specs = (
        tok(hp), tok(hp),
        pl.BlockSpec((1, 1, MLA_HEADS * VT_ROWS, tm), lambda bi, i: (bi, i // per_chunk, 0, i % per_chunk)),
        tok(N_BRANCH * D_MODEL),
    )
    for _, dil in DIL_PATTERNS:
        out_shapes += (jax.ShapeDtypeStruct((b, dil, s // dil, DIL_WIDTH), BF16),) * 3
        out_specs += (pl.BlockSpec((1, dil, tm // dil, DIL_WIDTH), lambda bi, i: (bi, 0, i, 0)),) * 3
    outs = pl.pallas_call(
        _mixer_in_kernel,
        name="mixer_in",
        grid=(b, s // tm),
        in_specs=[
            tok(D_MODEL), rot, rot,
            _const_spec((1, D_MODEL)),
            _const_spec(wa.shape), _const_spec(wd.shape), _const_spec(wg.shape),
            _const_spec((1, N_BRANCH * D_MODEL)),
            _const_spec((1, MLA_Q_RANK)), _const_spec((1, MLA_KV_RANK)),
            _const_spec(wuq.shape), _const_spec(wuk.shape), _const_spec(wuvt.shape),
        ],
        out_specs=out_specs,
        out_shape=out_shapes,
        scratch_shapes=[pltpu.VMEM((3 * DIL_WIDTH // LANES, tm, LANES), F32)],
        compiler_params=pltpu.CompilerParams(dimension_semantics=("arbitrary", "arbitrary"),
                                             vmem_limit_bytes=VMEM_LIMIT),
    )(h1, rot_cos, rot_sin, mix_pre_g.reshape(1, D_MODEL), wa, wd, wg, b_gate.reshape(1, -1),
      q_norm_g.reshape(1, -1), kv_norm_g.reshape(1, -1), wuq, wuk, wuvt)
    q, k, vt, gates = outs[:4]
    dil_qkv = [outs[4 + 3 * i:7 + 3 * i] for i in range(len(DIL_PATTERNS))]
    return q, k, vt, gates, dil_qkv


def _flash_kernel(q_ref, k_ref, vt_ref, o_ref, s0_ref, s1_ref, s2_ref, s3_ref, qt_ref):
    nk = k_ref.shape[1] // FLASH_TK
    nq = q_ref.shape[1] // FLASH_TQ
    assert nk % 4 == 0
    bufs = (s0_ref, s1_ref, s2_ref, s3_ref)

    chains = [(tt, hh) for tt in range(FLASH_QT) for hh in range(2)]

    def scores(t, c, ch, buf):
        tt, hh = chains[ch]
        cmax = None
        for part in range(FLASH_TK // FLASH_TS):
            k0 = c * FLASH_TK + part * FLASH_TS
            if not isinstance(k0, int):
                k0 = pl.multiple_of(k0, FLASH_TS)
            k = k_ref[0, pl.ds(k0, FLASH_TS), HEAD_PAD * hh:HEAD_PAD * (hh + 1)]
            st = jnp.dot(k, qt_ref[hh, t * FLASH_QT + tt], preferred_element_type=F32)
            buf[ch, FLASH_TS * part:FLASH_TS * (part + 1)] = st
            pm = jnp.max(st, axis=0, keepdims=True)
            cmax = pm if cmax is None else jnp.maximum(cmax, pm)
        return cmax

    def consume(c, ch, buf, m, cmax, acc):
        hh = chains[ch][1]
        m_new = jnp.maximum(m, cmax)
        alpha = jnp.exp2(m - m_new)
        p = jnp.exp2(buf[ch] - m_new).astype(BF16)
        vt = vt_ref[0, c, VT_ROWS * hh:VT_ROWS * (hh + 1), :]
        return m_new, alpha * acc + jnp.dot(vt, p, preferred_element_type=F32)

    def half(c, state, pair_in, pair_out, t_next, c_next):
        new = []
        for ch in range(len(chains)):
            m, cm0, cm1, acc = state[ch]
            nxt = [scores(t_next, c_next + i, ch, bufs[pair_out + i]) for i in range(2)]
            m, acc = consume(c, ch, bufs[pair_in], m, cm0, acc)
            m, acc = consume(c + 1, ch, bufs[pair_in + 1], m, cm1, acc)
            new.append((m, nxt[0], nxt[1], acc))
        return tuple(new)

    def tile(t, cmaxes):
        state = tuple((jnp.full((1, FLASH_TQ), -jnp.inf, F32), cmaxes[ch][0], cmaxes[ch][1],
                       jnp.zeros((VT_ROWS, FLASH_TQ), F32)) for ch in range(len(chains)))

        def group(j, st):
            c = 4 * j
            return half(c + 2, half(c, st, 0, 2, t, c + 2), 2, 0, t, c + 4)

        state = lax.fori_loop(0, nk // 4 - 1, group, state)
        t_next = jnp.minimum(t + 1, nq // FLASH_QT - 1)
        state = half(nk - 2, half(nk - 4, state, 0, 2, t, nk - 2), 2, 0, t_next, 0)
        for tt in range(FLASH_QT):
            outs = [acc[:MLA_V] / acc[MLA_V:MLA_V + 1] for (_, _, _, acc) in state[2 * tt:2 * tt + 2]]
            row0 = pl.multiple_of((t * FLASH_QT + tt) * FLASH_TQ, FLASH_TQ)
            o_ref[0, pl.ds(row0, FLASH_TQ), :] = jnp.concatenate(outs, axis=0).T.astype(BF16)
        return tuple((cm0, cm1) for (_, cm0, cm1, _) in state)

    def transpose_q(t, carry):
        for hh in range(2):
            q = q_ref[0, pl.ds(pl.multiple_of(t * FLASH_TQ, FLASH_TQ), FLASH_TQ), HEAD_PAD * hh:HEAD_PAD * (hh + 1)]
            qt_ref[hh, t] = q.astype(F32).T.astype(BF16)
        return carry

    lax.fori_loop(0, nq, transpose_q, 0)
    first = tuple((scores(0, 0, ch, bufs[0]), scores(0, 1, ch, bufs[1])) for ch in range(len(chains)))
    lax.fori_loop(0, nq // FLASH_QT, tile, first)


def _mla_flash(q, k, vt):
    b, s, _ = q.shape
    nk = s // FLASH_TK
    return pl.pallas_call(
        _flash_kernel,
        name="mla_flash",
        grid=(b, MLA_HEADS // 2),
        in_specs=[
            pl.BlockSpec((1, s, 2 * HEAD_PAD), lambda bi, hp: (bi, 0, hp)),
            pl.BlockSpec((1, s, 2 * HEAD_PAD), lambda bi, hp: (bi, 0, hp)),
            pl.BlockSpec((1, nk, 2 * VT_ROWS, FLASH_TK), lambda bi, hp: (bi, 0, hp, 0)),
        ],
        out_specs=pl.BlockSpec((1, s, 2 * MLA_V), lambda bi, hp: (bi, 0, hp)),
        out_shape=jax.ShapeDtypeStruct((b, s, MLA_WIDTH), BF16),
        scratch_shapes=[pltpu.VMEM((2 * FLASH_QT, FLASH_TK, FLASH_TQ), F32)] * 4
                       + [pltpu.VMEM((2, s // FLASH_TQ, HEAD_PAD, FLASH_TQ), BF16)],
        compiler_params=pltpu.CompilerParams(dimension_semantics=("arbitrary", "arbitrary"),
                                             vmem_limit_bytes=VMEM_LIMIT),
    )(q, k, vt)


def _dilated_kernel(q_ref, kc_ref, kp_ref, kn_ref, vc_ref, vp_ref, vn_ref, o_ref, lse_ref,
                    kw_ref, vt_ref, ot_ref, lt_ref, *sbufs, n_rows):
    lb = q_ref.shape[2]
    l0 = pl.program_id(2) * lb
    wrows = lb + 2 * DIL_HALF
    kw_ref[0:DIL_HALF] = kp_ref[0, 0]
    kw_ref[DIL_HALF:DIL_HALF + lb] = kc_ref[0, 0]
    kw_ref[DIL_HALF + lb:] = kn_ref[0, 0]
    vw = jnp.concatenate([vp_ref[0, 0], vc_ref[0, 0], vn_ref[0, 0]], axis=0).astype(F32)
    vt = vw.T.astype(BF16)
    ones = jnp.ones((VT_ROWS - DIL_HEAD_DIM, wrows), BF16)
    for h in range(DIL_HEADS):
        vt_ref[VT_ROWS * h:VT_ROWS * h + DIL_HEAD_DIM] = vt[DIL_HEAD_DIM * h:DIL_HEAD_DIM * (h + 1)]
        vt_ref[VT_ROWS * h + DIL_HEAD_DIM:VT_ROWS * (h + 1)] = ones

    win = DIL_SUB + 2 * DIL_HALF
    stripe_row = lax.broadcasted_iota(jnp.int32, (2 * LANES, LANES), 0)
    stripe_col = lax.broadcasted_iota(jnp.int32, (2 * LANES, LANES), 1)
    assert DIL_SUB == 2 * LANES and 2 * DIL_HALF == LANES
    live = ((slice(0, 2 * LANES), slice(0, LANES)), (slice(LANES, 3 * LANES), slice(LANES, 2 * LANES)))
    lane = lax.broadcasted_iota(jnp.int32, (DIL_SUB, LANES), 1)
    odd_lanes = ((lane >= HALF_B) & (lane < ROPE_DIM)) | (lane >= LANES // 2 + HALF_B)
    nbuf = len(sbufs)

    def produce(u):
        j, h = divmod(u, DIL_HEADS)
        ls = slice(LANES * (h // 2), LANES * (h // 2 + 1))
        kpair = kw_ref[j * DIL_SUB:j * DIL_SUB + win, ls]
        qpair = q_ref[0, 0, j * DIL_SUB:(j + 1) * DIL_SUB, ls]
        qh = jnp.where(odd_lanes if h % 2 else ~odd_lanes, qpair, jnp.zeros_like(qpair))
        st = lax.dot_general(kpair, qh, (((1,), (1,)), ((), ())), preferred_element_type=F32)
        cmax = []
        for rs, cs in live:
            srow = stripe_row + rs.start
            kidx = l0 - DIL_HALF + j * DIL_SUB + srow
            valid = ((srow - (stripe_col + cs.start)).astype(jnp.uint32) <= 2 * DIL_HALF) & (kidx.astype(jnp.uint32) < n_rows)
            sp = jnp.where(valid, st[rs, cs], NEG)
            sbufs[u % nbuf][rs, cs] = sp
            cmax.append(jnp.max(sp, axis=0, keepdims=True))
        return cmax

    def consume(u, m):
        j, h = divmod(u, DIL_HEADS)
        zero = jnp.zeros((LANES, LANES), BF16)
        cols = []
        for q, (rs, cs) in enumerate(live):
            pq = jnp.exp2(sbufs[u % nbuf][rs, cs] - m[q]).astype(BF16)
            cols.append(jnp.concatenate([pq, zero] if q == 0 else [zero, pq], axis=0))
        p = jnp.concatenate(cols, axis=1)
        r = jnp.dot(vt_ref[VT_ROWS * h:VT_ROWS * (h + 1), j * DIL_SUB:j * DIL_SUB + win], p,
                    preferred_element_type=F32)
        hs = slice(DIL_HEAD_DIM * h, DIL_HEAD_DIM * (h + 1))
        ot_ref[hs] = r[:DIL_HEAD_DIM] / r[DIL_HEAD_DIM:DIL_HEAD_DIM + 1]
        den8 = r[DIL_HEAD_DIM:DIL_HEAD_DIM + 8]
        for q, (_, cs) in enumerate(live):
            lt_ref[h:h + 1, cs] = m[q] + jnp.log2(den8[:, cs][0:1])
        if h == DIL_HEADS - 1:
            o_ref[0, 0, j * DIL_SUB:(j + 1) * DIL_SUB, :] = ot_ref[...].T.astype(BF16)
            lse_ref[0, 0, j * DIL_SUB:(j + 1) * DIL_SUB, :] = lt_ref[...].T

    lt_ref[...] = jnp.zeros_like(lt_ref)
    n_units = (lb // DIL_SUB) * DIL_HEADS
    grp = nbuf // 2
    m_cur = [produce(i) for i in range(grp)]
    for u in range(0, n_units, grp):
        m_next = [produce(u + grp + i) for i in range(grp)] if u + grp < n_units else None
        for i in range(grp):
            consume(u + i, m_cur[i])
        m_cur = m_next


def _dilated(dq, dk, dv, dilation):
    b, _, n_rows, w = dq.shape
    lb = min(DIL_LB, n_rows)
    hb = lb // DIL_HALF
    last = n_rows // DIL_HALF - 1
    cur = pl.BlockSpec((1, 1, lb, w), lambda bi, r, l: (bi, r, l, 0))
    prev = pl.BlockSpec((1, 1, DIL_HALF, w), lambda bi, r, l: (bi, r, jnp.maximum(l * hb - 1, 0), 0))
    nxt = pl.BlockSpec((1, 1, DIL_HALF, w), lambda bi, r, l: (bi, r, jnp.minimum((l + 1) * hb, last), 0))
    return pl.pallas_call(
        functools.partial(_dilated_kernel, n_rows=n_rows),
        name=f"dilated_{dilation}",
        grid=(b, dilation, n_rows // lb),
        in_specs=[cur, cur, prev, nxt, cur, prev, nxt],
        out_specs=(cur, pl.BlockSpec((1, 1, lb, LANES), lambda bi, r, l: (bi, r, l, 0))),
        out_shape=(jax.ShapeDtypeStruct((b, dilation, n_rows, w), BF16),
                   jax.ShapeDtypeStruct((b, dilation, n_rows, LANES), F32)),
        scratch_shapes=[pltpu.VMEM((lb + 2 * DIL_HALF, w), BF16),
                        pltpu.VMEM((DIL_HEADS * VT_ROWS, lb + 2 * DIL_HALF), BF16),
                        pltpu.VMEM((w, DIL_SUB), F32),
                        pltpu.VMEM((LANES, DIL_SUB), F32)]
                       + [pltpu.VMEM((DIL_SUB + 2 * DIL_HALF, DIL_SUB), F32)] * (2 * DIL_GROUP),
        compiler_params=pltpu.CompilerParams(dimension_semantics=("arbitrary",) * 3, vmem_limit_bytes=VMEM_LIMIT),
    )(dq, dk, dk, dk, dv, dv, dv)


def _mixer_out_kernel(h_ref, oa_ref, o1_ref, l1_ref, o2_ref, l2_ref, o3_ref, l3_ref, gate_ref,
                      wba_ref, wbb_ref, wo_ref, post_ref, expand_ref, out_ref, *scr):
    tm = h_ref.shape[1]

    def token_major(ref, dil, buf):
        if dil == 1:
            return ref[0, 0].astype(F32)
        nc = ref.shape[-1] // LANES
        for r in range(dil):
            for c in range(nc):
                buf[c, pl.ds(r, tm // dil, stride=dil), :] = ref[0, r, :, LANES * c:LANES * (c + 1)].astype(F32)
        return jnp.concatenate([buf[c] for c in range(nc)], axis=1)

    bufs = iter(scr)
    os_, ls_ = [], []
    for (o_r, l_r), (_, dil) in zip(((o1_ref, l1_ref), (o2_ref, l2_ref), (o3_ref, l3_ref)), DIL_PATTERNS):
        os_.append(token_major(o_r, dil, None if dil == 1 else next(bufs)))
        ls_.append(token_major(l_r, dil, None if dil == 1 else next(bufs)))
    lm = jnp.maximum(jnp.maximum(ls_[0], ls_[1]), ls_[2])
    es = [jnp.exp2(l - lm) for l in ls_]
    inv = 1.0 / (es[0] + es[1] + es[2])
    ob = None
    for e, o in zip(es, os_):
        w = e * inv
        hi = w.astype(BF16)
        lo = (w - hi.astype(F32)).astype(BF16)
        wide = jnp.dot(jnp.concatenate([hi, lo], axis=1), expand_ref[...], preferred_element_type=F32)
        ob = wide * o if ob is None else ob + wide * o
    ya = jnp.dot(oa_ref[0], wba_ref[...], preferred_element_type=F32)
    yb = jnp.dot(ob.astype(BF16), wbb_ref[...], preferred_element_type=F32)
    gate = gate_ref[0].astype(F32)
    merged = gate[:, :D_MODEL] * ya + gate[:, D_MODEL:] * yb
    mix = jnp.dot(merged.astype(BF16), wo_ref[...], preferred_element_type=F32)
    out_ref[0] = h_ref[0] + _rms(mix, post_ref[...])


def _mixer_out(h1, oa, parts, gates, w_branch_a, w_branch_b, w_out, mix_post_g):
    b, s, _ = h1.shape
    tm = MIX_TM
    tok = lambda c: pl.BlockSpec((1, tm, c), lambda bi, i: (bi, i, 0))
    flat, part_specs, scratch = [], [], []
    for (_, dil), pair in zip(DIL_PATTERNS, parts):
        flat += list(pair)
        for width in (DIL_WIDTH, LANES):
            part_specs.append(pl.BlockSpec((1, dil, tm // dil, width), lambda bi, i: (bi, 0, i, 0)))
            if dil > 1:
                scratch.append(pltpu.VMEM((width // LANES, tm, LANES), F32))
    head_of_lane = jnp.arange(DIL_WIDTH) // DIL_HEAD_DIM
    expand = (jnp.arange(2 * LANES)[:, None] % LANES == head_of_lane[None, :]).astype(BF16)
    return pl.pallas_call(
        _mixer_out_kernel,
        name="mixer_out",
        grid=(b, s // tm),
        scratch_shapes=scratch,
        in_specs=[tok(D_MODEL), tok(MLA_WIDTH)] + part_specs + [tok(N_BRANCH * D_MODEL),
                  _const_spec((MLA_WIDTH, D_MODEL)), _const_spec((DIL_WIDTH, D_MODEL)),
                  _const_spec((D_MODEL, D_MODEL)), _const_spec((1, D_MODEL)), _const_spec((2 * LANES, DIL_WIDTH))],
        out_specs=tok(D_MODEL),
        out_shape=jax.ShapeDtypeStruct((b, s, D_MODEL), F32),
        compiler_params=pltpu.CompilerParams(dimension_semantics=("arbitrary", "arbitrary"),
                                             vmem_limit_bytes=VMEM_LIMIT),
    )(h1, oa, *flat, gates, w_branch_a.astype(BF16), w_branch_b.astype(BF16), w_out.astype(BF16),
      mix_post_g.reshape(1, D_MODEL), expand)


def kernel(x, positions, ffn1_pre_g, ffn1_post_g, ffn1_w_gate, ffn1_w_up, ffn1_w_down, mix_pre_g, w_in, b_gate,
           q_norm_g, w_uq, kv_norm_g, w_uk, w_uv, w_branch_a, w_branch_b, w_out, mix_post_g,
           ffn2_pre_g, ffn2_post_g, ffn2_w_gate, ffn2_w_up, ffn2_w_down):
    b, s, d = x.shape
    depth = ffn1_pre_g.shape[0]
    assert all(win // (2 * dil) == DIL_HALF and s % (dil * DIL_SUB) == 0 for win, dil in DIL_PATTERNS)
    h = x
    for l in range(depth):
        h, rot_cos, rot_sin = _ffn(h.reshape(b * s, d), ffn1_pre_g[l], ffn1_post_g[l], ffn1_w_gate[l], ffn1_w_up[l],
                                   ffn1_w_down[l], positions=positions.astype(F32).reshape(1, b * s))
        h = h.reshape(b, s, d)
        q, k, vt, gates, dil_qkv = _mixer_in(h, rot_cos, rot_sin, mix_pre_g[l], w_in[l], b_gate[l], q_norm_g[l],
                                             w_uq[l], kv_norm_g[l], w_uk[l], w_uv[l])
        oa = _mla_flash(q, k, vt)
        parts = [_dilated(*qkv, dil) for qkv, (_, dil) in zip(dil_qkv, DIL_PATTERNS)]
        h = _mixer_out(h, oa, parts, gates, w_branch_a[l], w_branch_b[l], w_out[l], mix_post_g[l])
        h = _ffn(h.reshape(b * s, d), ffn2_pre_g[l], ffn2_post_g[l], ffn2_w_gate[l], ffn2_w_up[l],
                 ffn2_w_down[l]).reshape(b, s, d)
    return h
```

```python
import functools

import jax
import jax.numpy as jnp
from jax import lax
from jax.experimental import pallas as pl
from jax.experimental.pallas import tpu as pltpu

F32 = jnp.float32
BF16 = jnp.bfloat16

D_MODEL = 1024
D_FF = 2816
EPS = 1e-6
MLA_HEADS = 8
MLA_Q_RANK = 384
MLA_KV_RANK = 256
MLA_NOPE = 64
MLA_ROPE = 32
MLA_V = 64
MLA_THETA = 10000.0
MLA_WIDTH = MLA_HEADS * MLA_V
DIL_HEADS = 8
DIL_HEAD_DIM = 64
DIL_PATTERNS = ((128, 1), (512, 4), (2048, 16))
DIL_WIDTH = DIL_HEADS * DIL_HEAD_DIM
ROPE_THETA = 500000.0
ROPE_DIM = DIL_HEAD_DIM // 4
N_BRANCH = 2
NEG = -1e30

LANES = 128
HEAD_PAD = LANES
VMEM_LIMIT = 56 * 1024 * 1024

FFN_TM = 1024
FFN_SPLIT = 2
FFN_FC = 256
MIX_TM = 512
FLASH_TQ = 256
FLASH_QT = 4
FLASH_TK = MIX_TM
FLASH_TS = FLASH_TK // 2
VT_ROWS = MLA_V + 16
LOG2E = 1.4426950408889634
DIL_LB = 2048
DIL_SUB = 256
DIL_GROUP = 4
DIL_HALF = 64


def _rms(x, g):
    ms = jnp.mean(x * x, axis=-1, keepdims=True)
    return x * lax.rsqrt(ms + EPS) * g


def _const_spec(shape):
    nd = len(shape)
    return pl.BlockSpec(shape, lambda *_: (0,) * nd, pipeline_mode=pl.Buffered(1))


def _ffn_kernel(x_ref, pre_ref, post_ref, wg_ref, wu_ref, wd_ref, *rest, with_rotary):
    if with_rotary:
        pos_ref, freq_ref, o_ref, cos_ref, sin_ref, hm_ref = rest
        ang = freq_ref[...] * pos_ref[...]
        cos_ref[...] = jnp.cos(ang)
        sin_ref[...] = jnp.sin(ang)
    else:
        o_ref, hm_ref = rest
    rows = x_ref.shape[0] // FFN_SPLIT
    for r in range(FFN_SPLIT):
        rs = slice(rows * r, rows * (r + 1))
        x = x_ref[rs]
        xn = _rms(x, pre_ref[...]).astype(BF16)
        for c0 in range(0, D_FF, FFN_FC):
            sl = slice(c0, min(c0 + FFN_FC, D_FF))
            g = jnp.dot(xn, wg_ref[:, sl], preferred_element_type=F32)
            u = jnp.dot(xn, wu_ref[:, sl], preferred_element_type=F32)
            hm_ref[rs, sl] = (g * jax.nn.sigmoid(g) * u).astype(BF16)
        f = jnp.dot(hm_ref[rs], wd_ref[...], preferred_element_type=F32)
        o_ref[rs] = x + 0.5 * _rms(f, post_ref[...])


def _ffn(h, pre_g, post_g, w_gate, w_up, w_down, positions=None):
    t = h.shape[0]
    tok = lambda c: pl.BlockSpec((FFN_TM, c), lambda i: (i, 0))
    in_specs = [tok(D_MODEL), _const_spec((1, D_MODEL)), _const_spec((1, D_MODEL)),
                _const_spec((D_MODEL, D_FF)), _const_spec((D_MODEL, D_FF)), _const_spec((D_FF, D_MODEL))]
    args = [h, pre_g.reshape(1, D_MODEL), post_g.reshape(1, D_MODEL),
            w_gate.astype(BF16), w_up.astype(BF16), w_down.astype(BF16)]
    out_specs, out_shape = tok(D_MODEL), jax.ShapeDtypeStruct((t, D_MODEL), F32)
    if positions is not None:
        dense = pl.BlockSpec((ROT_ROWS, FFN_TM), lambda i: (0, i))
        in_specs += [pl.BlockSpec((1, FFN_TM), lambda i: (0, i)), _const_spec((ROT_ROWS, 1))]
        args += [positions, _rope_freqs()]
        out_specs = (out_specs, dense, dense)
        out_shape = (out_shape,) + (jax.ShapeDtypeStruct((ROT_ROWS, t), F32),) * 2
    return pl.pallas_call(
        functools.partial(_ffn_kernel, with_rotary=positions is not None),
        name="ffn",
        grid=(t // FFN_TM,),
        in_specs=in_specs,
        out_specs=out_specs,
        out_shape=out_shape,
        scratch_shapes=[pltpu.VMEM((FFN_TM, D_FF), BF16)],
        compiler_params=pltpu.CompilerParams(dimension_semantics=("arbitrary",), vmem_limit_bytes=VMEM_LIMIT),
    )(*args)


def _rope_tile(x, cos, sin):
    return x * cos + pltpu.roll(x, LANES // 2, 1) * sin


def _mixer_in_kernel(h_ref, cos_ref, sin_ref, pre_ref, wa_ref, wd_ref, wg_ref, bg_ref, qg_ref, kvg_ref,
                     wuq_ref, wuk_ref, wuvt_ref,
                     q_ref, k_ref, vt_ref, gate_ref, *rest):
    dil_refs, dscr_ref = rest[:-1], rest[-1]
    u = _rms(h_ref[0], pre_ref[...]).astype(BF16)
    def token_tile(ref, fill):
        pad = jnp.full((LANES // 2 - HALF_A, ref.shape[1]), fill, F32)
        return jnp.concatenate([ref[0:HALF_A], pad, ref[HALF_A:2 * HALF_A], pad], axis=0).T

    cc = token_tile(cos_ref, 1.0)
    ss = token_tile(sin_ref, 0.0)
    rcc = pltpu.roll(cc, LANES // 2, 1)
    rss = pltpu.roll(ss, LANES // 2, 1)
    low = lax.broadcasted_iota(jnp.int32, cc.shape, 1) < LANES // 2
    cos_a = jnp.where(low, cc, rcc)
    sin_a = jnp.where(low, -ss, rss)
    cos_b = jnp.where(low, rcc, cc)
    sin_b = jnp.where(low, -rss, ss)

    pa = jnp.dot(u, wa_ref[...], preferred_element_type=F32)
    cq = pa[:, :MLA_Q_RANK]
    ckv = pa[:, MLA_Q_RANK:MLA_Q_RANK + MLA_KV_RANK]
    kr = _rope_tile(pa[:, MLA_Q_RANK + MLA_KV_RANK:], cos_a, sin_a)
    qn = _rms(cq, qg_ref[...]).astype(BF16)
    ckvn = _rms(ckv, kvg_ref[...]).astype(BF16)
    q = jnp.dot(qn, wuq_ref[...], preferred_element_type=F32)

    pg = jnp.dot(u, wg_ref[...], preferred_element_type=F32) + bg_ref[...]
    gate_ref[0] = jax.nn.sigmoid(pg).astype(BF16)

    kn = jnp.dot(ckvn, wuk_ref[...], preferred_element_type=F32)
    scale = LOG2E * (MLA_NOPE + MLA_ROPE) ** -0.5
    for h in range(MLA_HEADS):
        sl = slice(HEAD_PAD * h, HEAD_PAD * (h + 1))
        qh = _rope_tile(q[:, sl], cos_a, sin_a)
        qt = (qh * scale).T.astype(BF16)
        for j in range(qt.shape[1] // FLASH_TQ):
            q_ref[0, h, j] = qt[:, FLASH_TQ * j:FLASH_TQ * (j + 1)]
        k_ref[0, :, sl] = (kn[:, sl] + kr).astype(BF16)
    vt = lax.dot_general(wuvt_ref[...], ckvn, (((1,), (1,)), ((), ())), preferred_element_type=F32)
    row = lax.broadcasted_iota(jnp.int32, vt.shape, 0)
    vt_ref[0, 0] = jnp.where(row % VT_ROWS >= MLA_V, 1.0, vt).astype(BF16)

    pd = jnp.dot(u, wd_ref[...], preferred_element_type=F32)
    dscale = LOG2E * DIL_HEAD_DIM ** -0.5
    nc = DIL_WIDTH // LANES
    for c in range(nc):
        sl = slice(LANES * c, LANES * (c + 1))
        ks = slice(DIL_WIDTH + LANES * c, DIL_WIDTH + LANES * (c + 1))
        dscr_ref[c] = _rope_tile(pd[:, sl], cos_b, sin_b) * dscale
        dscr_ref[nc + c] = _rope_tile(pd[:, ks], cos_b, sin_b)
        dscr_ref[2 * nc + c] = pd[:, 2 * DIL_WIDTH + LANES * c:2 * DIL_WIDTH + LANES * (c + 1)]
    tm = dscr_ref.shape[1]
    for pi, (_, dil) in enumerate(DIL_PATTERNS):
        for t in range(3):
            for c in range(nc):
                if dil == 1:
                    dil_refs[3 * pi + t][0, 0, :, LANES * c:LANES * (c + 1)] = dscr_ref[nc * t + c].astype(BF16)
                    continue
                for r in range(dil):
                    rows = dscr_ref[nc * t + c, pl.ds(r, tm // dil, stride=dil), :]
                    dil_refs[3 * pi + t][0, r, :, LANES * c:LANES * (c + 1)] = rows.astype(BF16)


HALF_A = MLA_ROPE // 2
HALF_B = ROPE_DIM // 2
NOPE_LO = LANES // 2 - HALF_A
ROT_ROWS = 2 * HALF_A


def _rope_freqs():
    assert 2 * HALF_B == HALF_A
    inv_a = 1.0 / (jnp.float32(MLA_THETA) ** (jnp.arange(HALF_A, dtype=F32) / HALF_A))
    inv_b = 1.0 / (jnp.float32(ROPE_THETA) ** (jnp.arange(HALF_B, dtype=F32) / HALF_B))
    return jnp.concatenate([inv_a, inv_b, inv_b]).reshape(ROT_ROWS, 1)


def _mla_tile_cols(nope, rope):
    lead = (nope if nope is not None else rope).shape[:-1]
    zeros = lambda n: jnp.zeros(lead + (n,), F32)
    x1, x2 = (rope[..., :HALF_A], rope[..., HALF_A:]) if rope is not None else (zeros(HALF_A), zeros(HALF_A))
    n_lo, n_hi = (nope[..., :NOPE_LO], nope[..., NOPE_LO:]) if nope is not None else (zeros(NOPE_LO), zeros(MLA_NOPE - NOPE_LO))
    return jnp.concatenate([x1, n_lo, x2, n_hi, zeros(HEAD_PAD - MLA_NOPE - MLA_ROPE)], axis=-1)


def _dil_tile_cols(w):
    wh = w.reshape(w.shape[0], DIL_HEADS // 2, 2, DIL_HEAD_DIM)
    a, b = wh[:, :, 0], wh[:, :, 1]
    tile = jnp.concatenate([a[..., :HALF_B], b[..., :HALF_B], a[..., ROPE_DIM:], a[..., HALF_B:ROPE_DIM],
                            b[..., HALF_B:ROPE_DIM], b[..., ROPE_DIM:]], axis=-1)
    return tile.reshape(w.shape[0], DIL_WIDTH)


def _mixer_in(h1, rot_cos, rot_sin, mix_pre_g, w_in, b_gate, q_norm_g, w_uq, kv_norm_g, w_uk, w_uv):
    b, s, _ = h1.shape
    o0 = 0
    o1 = o0 + MLA_Q_RANK
    o2 = o1 + MLA_KV_RANK
    o3 = o2 + MLA_ROPE
    o4 = o3 + 3 * DIL_WIDTH
    w_kr = _mla_tile_cols(None, w_in[:, o2:o3])
    wa = jnp.concatenate([w_in[:, o0:o2], w_kr], axis=1).astype(BF16)
    wd = jnp.concatenate([_dil_tile_cols(w_in[:, o3:o3 + DIL_WIDTH]),
                          _dil_tile_cols(w_in[:, o3 + DIL_WIDTH:o3 + 2 * DIL_WIDTH]),
                          w_in[:, o3 + 2 * DIL_WIDTH:o4]], axis=1).astype(BF16)
    wg = w_in[:, o4:].astype(BF16)
    wuq_h = w_uq.reshape(MLA_Q_RANK, MLA_HEADS, MLA_NOPE + MLA_ROPE)
    wuq = _mla_tile_cols(wuq_h[..., :MLA_NOPE], wuq_h[..., MLA_NOPE:]).reshape(MLA_Q_RANK, -1).astype(BF16)
    wuk = _mla_tile_cols(w_uk.reshape(MLA_KV_RANK, MLA_HEADS, MLA_NOPE), None).reshape(MLA_KV_RANK, -1).astype(BF16)
    wuvt = jnp.pad(w_uv.T.reshape(MLA_HEADS, MLA_V, MLA_KV_RANK), ((0, 0), (0, VT_ROWS - MLA_V), (0, 0))
                   ).reshape(MLA_HEADS * VT_ROWS, MLA_KV_RANK).astype(BF16)
    tm = MIX_TM
    hp = MLA_HEADS * HEAD_PAD
    tok = lambda c: pl.BlockSpec((1, tm, c), lambda bi, i: (bi, i, 0))
    rot = pl.BlockSpec((ROT_ROWS, tm), lambda bi, i: (0, bi * (s // tm) + i))
    out_shapes = (
        jax.ShapeDtypeStruct((b, MLA_HEADS, s // FLASH_TQ, HEAD_PAD, FLASH_TQ), BF16),
        jax.ShapeDtypeStruct((b, s, hp), BF16),
        jax.ShapeDtypeStruct((b, s // FLASH_TK, MLA_HEADS * VT_ROWS, FLASH_TK), BF16),
        jax.ShapeDtypeStruct((b, s, N_BRANCH * D_MODEL), BF16),
    )
    per_chunk = FLASH_TK // tm
    out_specs = (
        pl.BlockSpec((1, MLA_HEADS, tm // FLASH_TQ, HEAD_PAD, FLASH_TQ), lambda bi, i: (bi, 0, i, 0, 0)),
        tok(hp),
        pl.BlockSpec((1, 1, MLA_HEADS * VT_ROWS, tm), lambda bi, i: (bi, i // per_chunk, 0, i % per_chunk)),
        tok(N_BRANCH * D_MODEL),
    )
    for _, dil in DIL_PATTERNS:
        out_shapes += (jax.ShapeDtypeStruct((b, dil, s // dil, DIL_WIDTH), BF16),) * 3
        out_specs += (pl.BlockSpec((1, dil, tm // dil, DIL_WIDTH), lambda bi, i: (bi, 0, i, 0)),) * 3
    outs = pl.pallas_call(
        _mixer_in_kernel,
        name="mixer_in",
        grid=(b, s // tm),
        in_specs=[
            tok(D_MODEL), rot, rot,
            _const_spec((1, D_MODEL)),
            _const_spec(wa.shape), _const_spec(wd.shape), _const_spec(wg.shape),
            _const_spec((1, N_BRANCH * D_MODEL)),
            _const_spec((1, MLA_Q_RANK)), _const_spec((1, MLA_KV_RANK)),
            _const_spec(wuq.shape), _const_spec(wuk.shape), _const_spec(wuvt.shape),
        ],
        out_specs=out_specs,
        out_shape=out_shapes,
        scratch_shapes=[pltpu.VMEM((3 * DIL_WIDTH // LANES, tm, LANES), F32)],
        compiler_params=pltpu.CompilerParams(dimension_semantics=("arbitrary", "arbitrary"),
                                             vmem_limit_bytes=VMEM_LIMIT),
    )(h1, rot_cos, rot_sin, mix_pre_g.reshape(1, D_MODEL), wa, wd, wg, b_gate.reshape(1, -1),
      q_norm_g.reshape(1, -1), kv_norm_g.reshape(1, -1), wuq, wuk, wuvt)
    q, k, vt, gates = outs[:4]
    dil_qkv = [outs[4 + 3 * i:7 + 3 * i] for i in range(len(DIL_PATTERNS))]
    return q, k, vt, gates, dil_qkv


def _flash_kernel(qt_ref, k_ref, vt_ref, o_ref, s0_ref, s1_ref, s2_ref, s3_ref):
    nk = k_ref.shape[1] // FLASH_TK
    nq = qt_ref.shape[2]
    assert nk % 4 == 0 and nq % FLASH_QT == 0
    bufs = (s0_ref, s1_ref, s2_ref, s3_ref)

    chains = [(tt, hh) for tt in range(FLASH_QT) for hh in range(2)]

    def scores(t, c, ch, buf):
        tt, hh = chains[ch]
        cmax = None
        for part in range(FLASH_TK // FLASH_TS):
            k0 = c * FLASH_TK + part * FLASH_TS
            if not isinstance(k0, int):
                k0 = pl.multiple_of(k0, FLASH_TS)
            k = k_ref[0, pl.ds(k0, FLASH_TS), HEAD_PAD * hh:HEAD_PAD * (hh + 1)]
            st = jnp.dot(k, qt_ref[0, hh, t * FLASH_QT + tt], preferred_element_type=F32)
            buf[ch, FLASH_TS * part:FLASH_TS * (part + 1)] = st
            pm = jnp.max(st, axis=0, keepdims=True)
            cmax = pm if cmax is None else jnp.maximum(cmax, pm)
        return cmax

    def consume(c, ch, buf, m, cmax, acc):
        hh = chains[ch][1]
        m_new = jnp.maximum(m, cmax)
        alpha = jnp.exp2(m - m_new)
        p = jnp.exp2(buf[ch] - m_new).astype(BF16)
        vt = vt_ref[0, c, VT_ROWS * hh:VT_ROWS * (hh + 1), :]
        return m_new, alpha * acc + jnp.dot(vt, p, preferred_element_type=F32)

    def half(c, state, pair_in, pair_out, t_next, c_next):
        new = []
        for ch in range(len(chains)):
            m, cm0, cm1, acc = state[ch]
            nxt = [scores(t_next, c_next + i, ch, bufs[pair_out + i]) for i in range(2)]
            m, acc = consume(c, ch, bufs[pair_in], m, cm0, acc)
            m, acc = consume(c + 1, ch, bufs[pair_in + 1], m, cm1, acc)
            new.append((m, nxt[0], nxt[1], acc))
        return tuple(new)

    def tile(t, cmaxes):
        state = tuple((jnp.full((1, FLASH_TQ), -jnp.inf, F32), cmaxes[ch][0], cmaxes[ch][1],
                       jnp.zeros((VT_ROWS, FLASH_TQ), F32)) for ch in range(len(chains)))

        def group(j, st):
            c = 4 * j
            return half(c + 2, half(c, st, 0, 2, t, c + 2), 2, 0, t, c + 4)

        state = lax.fori_loop(0, nk // 4 - 1, group, state)
        t_next = jnp.minimum(t + 1, nq // FLASH_QT - 1)
        state = half(nk - 2, half(nk - 4, state, 0, 2, t, nk - 2), 2, 0, t_next, 0)
        for tt in range(FLASH_QT):
            outs = [acc[:MLA_V] / acc[MLA_V:MLA_V + 1] for (_, _, _, acc) in state[2 * tt:2 * tt + 2]]
            row0 = pl.multiple_of((t * FLASH_QT + tt) * FLASH_TQ, FLASH_TQ)
            o_ref[0, pl.ds(row0, FLASH_TQ), :] = jnp.concatenate(outs, axis=0).T.astype(BF16)
        return tuple((cm0, cm1) for (_, cm0, cm1, _) in state)

    first = tuple((scores(0, 0, ch, bufs[0]), scores(0, 1, ch, bufs[1])) for ch in range(len(chains)))
    lax.fori_loop(0, nq // FLASH_QT, tile, first)


def _mla_flash(qt, k, vt):
    b, s, _ = k.shape
    nk = s // FLASH_TK
    return pl.pallas_call(
        _flash_kernel,
        name="mla_flash",
        grid=(b, MLA_HEADS // 2),
        in_specs=[
            pl.BlockSpec((1, 2, s // FLASH_TQ, HEAD_PAD, FLASH_TQ), lambda bi, hp: (bi, hp, 0, 0, 0)),
            pl.BlockSpec((1, s, 2 * HEAD_PAD), lambda bi, hp: (bi, 0, hp)),
            pl.BlockSpec((1, nk, 2 * VT_ROWS, FLASH_TK), lambda bi, hp: (bi, 0, hp, 0)),
        ],
        out_specs=pl.BlockSpec((1, s, 2 * MLA_V), lambda bi, hp: (bi, 0, hp)),
        out_shape=jax.ShapeDtypeStruct((b, s, MLA_WIDTH), BF16),
        scratch_shapes=[pltpu.VMEM((2 * FLASH_QT, FLASH_TK, FLASH_TQ), F32)] * 4,
        compiler_params=pltpu.CompilerParams(dimension_semantics=("arbitrary", "arbitrary"),
                                             vmem_limit_bytes=VMEM_LIMIT),
    )(qt, k, vt)


def _dilated_kernel(q_ref, kc_ref, kp_ref, kn_ref, vc_ref, vp_ref, vn_ref, o_ref, lse_ref,
                    kw_ref, vt_ref, ot_ref, lt_ref, *sbufs, n_rows):
    lb = q_ref.shape[2]
    l0 = pl.program_id(2) * lb
    wrows = lb + 2 * DIL_HALF
    kw_ref[0:DIL_HALF] = kp_ref[0, 0]
    kw_ref[DIL_HALF:DIL_HALF + lb] = kc_ref[0, 0]
    kw_ref[DIL_HALF + lb:] = kn_ref[0, 0]
    vw = jnp.concatenate([vp_ref[0, 0], vc_ref[0, 0], vn_ref[0, 0]], axis=0).astype(F32)
    vt = vw.T.astype(BF16)
    ones = jnp.ones((VT_ROWS - DIL_HEAD_DIM, wrows), BF16)
    for h in range(DIL_HEADS):
        vt_ref[VT_ROWS * h:VT_ROWS * h + DIL_HEAD_DIM] = vt[DIL_HEAD_DIM * h:DIL_HEAD_DIM * (h + 1)]
        vt_ref[VT_ROWS * h + DIL_HEAD_DIM:VT_ROWS * (h + 1)] = ones

    win = DIL_SUB + 2 * DIL_HALF
    stripe_row = lax.broadcasted_iota(jnp.int32, (2 * LANES, LANES), 0)
    stripe_col = lax.broadcasted_iota(jnp.int32, (2 * LANES, LANES), 1)
    assert DIL_SUB == 2 * LANES and 2 * DIL_HALF == LANES
    live = ((slice(0, 2 * LANES), slice(0, LANES)), (slice(LANES, 3 * LANES), slice(LANES, 2 * LANES)))
    lane = lax.broadcasted_iota(jnp.int32, (DIL_SUB, LANES), 1)
    odd_lanes = ((lane >= HALF_B) & (lane < ROPE_DIM)) | (lane >= LANES // 2 + HALF_B)
    nbuf = len(sbufs)

    def produce(u):
        j, h = divmod(u, DIL_HEADS)
        ls = slice(LANES * (h // 2), LANES * (h // 2 + 1))
        kpair = kw_ref[j * DIL_SUB:j * DIL_SUB + win, ls]
        qpair = q_ref[0, 0, j * DIL_SUB:(j + 1) * DIL_SUB, ls]
        qh = jnp.where(odd_lanes if h % 2 else ~odd_lanes, qpair, jnp.zeros_like(qpair))
        st = lax.dot_general(kpair, qh, (((1,), (1,)), ((), ())), preferred_element_type=F32)
        cmax = []
        for rs, cs in live:
            srow = stripe_row + rs.start
            kidx = l0 - DIL_HALF + j * DIL_SUB + srow
            valid = ((srow - (stripe_col + cs.start)).astype(jnp.uint32) <= 2 * DIL_HALF) & (kidx.astype(jnp.uint32) < n_rows)
            sp = jnp.where(valid, st[rs, cs], NEG)
            sbufs[u % nbuf][rs, cs] = sp
            cmax.append(jnp.max(sp, axis=0, keepdims=True))
        return cmax

    def consume(u, m):
        j, h = divmod(u, DIL_HEADS)
        zero = jnp.zeros((LANES, LANES), BF16)
        cols = []
        for q, (rs, cs) in enumerate(live):
            pq = jnp.exp2(sbufs[u % nbuf][rs, cs] - m[q]).astype(BF16)
            cols.append(jnp.concatenate([pq, zero] if q == 0 else [zero, pq], axis=0))
        p = jnp.concatenate(cols, axis=1)
        r = jnp.dot(vt_ref[VT_ROWS * h:VT_ROWS * (h + 1), j * DIL_SUB:j * DIL_SUB + win], p,
                    preferred_element_type=F32)
        hs = slice(DIL_HEAD_DIM * h, DIL_HEAD_DIM * (h + 1))
        ot_ref[hs] = r[:DIL_HEAD_DIM] / r[DIL_HEAD_DIM:DIL_HEAD_DIM + 1]
        den8 = r[DIL_HEAD_DIM:DIL_HEAD_DIM + 8]
        for q, (_, cs) in enumerate(live):
            lt_ref[h:h + 1, cs] = m[q] + jnp.log2(den8[:, cs][0:1])
        if h == DIL_HEADS - 1:
            o_ref[0, 0, j * DIL_SUB:(j + 1) * DIL_SUB, :] = ot_ref[...].T.astype(BF16)
            lse_ref[0, 0, j * DIL_SUB:(j + 1) * DIL_SUB, :] = lt_ref[...].T

    lt_ref[...] = jnp.zeros_like(lt_ref)
    n_units = (lb // DIL_SUB) * DIL_HEADS
    grp = nbuf // 2
    m_cur = [produce(i) for i in range(grp)]
    for u in range(0, n_units, grp):
        m_next = [produce(u + grp + i) for i in range(grp)] if u + grp < n_units else None
        for i in range(grp):
            consume(u + i, m_cur[i])
        m_cur = m_next


def _dilated(dq, dk, dv, dilation):
    b, _, n_rows, w = dq.shape
    lb = min(DIL_LB, n_rows)
    hb = lb // DIL_HALF
    last = n_rows // DIL_HALF - 1
    cur = pl.BlockSpec((1, 1, lb, w), lambda bi, r, l: (bi, r, l, 0))
    prev = pl.BlockSpec((1, 1, DIL_HALF, w), lambda bi, r, l: (bi, r, jnp.maximum(l * hb - 1, 0), 0))
    nxt = pl.BlockSpec((1, 1, DIL_HALF, w), lambda bi, r, l: (bi, r, jnp.minimum((l + 1) * hb, last), 0))
    return pl.pallas_call(
        functools.partial(_dilated_kernel, n_rows=n_rows),
        name=f"dilated_{dilation}",
        grid=(b, dilation, n_rows // lb),
        in_specs=[cur, cur, prev, nxt, cur, prev, nxt],
        out_specs=(cur, pl.BlockSpec((1, 1, lb, LANES), lambda bi, r, l: (bi, r, l, 0))),
        out_shape=(jax.ShapeDtypeStruct((b, dilation, n_rows, w), BF16),
                   jax.ShapeDtypeStruct((b, dilation, n_rows, LANES), F32)),
        scratch_shapes=[pltpu.VMEM((lb + 2 * DIL_HALF, w), BF16),
                        pltpu.VMEM((DIL_HEADS * VT_ROWS, lb + 2 * DIL_HALF), BF16),
                        pltpu.VMEM((w, DIL_SUB), F32),
                        pltpu.VMEM((LANES, DIL_SUB), F32)]
                       + [pltpu.VMEM((DIL_SUB + 2 * DIL_HALF, DIL_SUB), F32)] * (2 * DIL_GROUP),
        compiler_params=pltpu.CompilerParams(dimension_semantics=("arbitrary",) * 3, vmem_limit_bytes=VMEM_LIMIT),
    )(dq, dk, dk, dk, dv, dv, dv)


def _mixer_out_kernel(h_ref, oa_ref, o1_ref, l1_ref, o2_ref, l2_ref, o3_ref, l3_ref, gate_ref,
                      wba_ref, wbb_ref, wo_ref, post_ref, expand_ref, out_ref, *scr):
    tm = h_ref.shape[1]

    def token_major(ref, dil, buf):
        if dil == 1:
            return ref[0, 0].astype(F32)
        nc = ref.shape[-1] // LANES
        for r in range(dil):
            for c in range(nc):
                buf[c, pl.ds(r, tm // dil, stride=dil), :] = ref[0, r, :, LANES * c:LANES * (c + 1)].astype(F32)
        return jnp.concatenate([buf[c] for c in range(nc)], axis=1)

    bufs = iter(scr)
    os_, ls_ = [], []
    for (o_r, l_r), (_, dil) in zip(((o1_ref, l1_ref), (o2_ref, l2_ref), (o3_ref, l3_ref)), DIL_PATTERNS):
        os_.append(token_major(o_r, dil, None if dil == 1 else next(bufs)))
        ls_.append(token_major(l_r, dil, None if dil == 1 else next(bufs)))
    lm = jnp.maximum(jnp.maximum(ls_[0], ls_[1]), ls_[2])
    es = [jnp.exp2(l - lm) for l in ls_]
    inv = 1.0 / (es[0] + es[1] + es[2])
    ob = None
    for e, o in zip(es, os_):
        w = e * inv
        hi = w.astype(BF16)
        lo = (w - hi.astype(F32)).astype(BF16)
        wide = jnp.dot(jnp.concatenate([hi, lo], axis=1), expand_ref[...], preferred_element_type=F32)
        ob = wide * o if ob is None else ob + wide * o
    ya = jnp.dot(oa_ref[0], wba_ref[...], preferred_element_type=F32)
    yb = jnp.dot(ob.astype(BF16), wbb_ref[...], preferred_element_type=F32)
    gate = gate_ref[0].astype(F32)
    merged = gate[:, :D_MODEL] * ya + gate[:, D_MODEL:] * yb
    mix = jnp.dot(merged.astype(BF16), wo_ref[...], preferred_element_type=F32)
    out_ref[0] = h_ref[0] + _rms(mix, post_ref[...])


def _mixer_out(h1, oa, parts, gates, w_branch_a, w_branch_b, w_out, mix_post_g):
    b, s, _ = h1.shape
    tm = MIX_TM
    tok = lambda c: pl.BlockSpec((1, tm, c), lambda bi, i: (bi, i, 0))
    flat, part_specs, scratch = [], [], []
    for (_, dil), pair in zip(DIL_PATTERNS, parts):
        flat += list(pair)
        for width in (DIL_WIDTH, LANES):
            part_specs.append(pl.BlockSpec((1, dil, tm // dil, width), lambda bi, i: (bi, 0, i, 0)))
            if dil > 1:
                scratch.append(pltpu.VMEM((width // LANES, tm, LANES), F32))
    head_of_lane = jnp.arange(DIL_WIDTH) // DIL_HEAD_DIM
    expand = (jnp.arange(2 * LANES)[:, None] % LANES == head_of_lane[None, :]).astype(BF16)
    return pl.pallas_call(
        _mixer_out_kernel,
        name="mixer_out",
        grid=(b, s // tm),
        scratch_shapes=scratch,
        in_specs=[tok(D_MODEL), tok(MLA_WIDTH)] + part_specs + [tok(N_BRANCH * D_MODEL),
                  _const_spec((MLA_WIDTH, D_MODEL)), _const_spec((DIL_WIDTH, D_MODEL)),
                  _const_spec((D_MODEL, D_MODEL)), _const_spec((1, D_MODEL)), _const_spec((2 * LANES, DIL_WIDTH))],
        out_specs=tok(D_MODEL),
        out_shape=jax.ShapeDtypeStruct((b, s, D_MODEL), F32),
        compiler_params=pltpu.CompilerParams(dimension_semantics=("arbitrary", "arbitrary"),
                                             vmem_limit_bytes=VMEM_LIMIT),
    )(h1, oa, *flat, gates, w_branch_a.astype(BF16), w_branch_b.astype(BF16), w_out.astype(BF16),
      mix_post_g.reshape(1, D_MODEL), expand)


def kernel(x, positions, ffn1_pre_g, ffn1_post_g, ffn1_w_gate, ffn1_w_up, ffn1_w_down, mix_pre_g, w_in, b_gate,
           q_norm_g, w_uq, kv_norm_g, w_uk, w_uv, w_branch_a, w_branch_b, w_out, mix_post_g,
           ffn2_pre_g, ffn2_post_g, ffn2_w_gate, ffn2_w_up, ffn2_w_down):
    b, s, d = x.shape
    depth = ffn1_pre_g.shape[0]
    assert all(win // (2 * dil) == DIL_HALF and s % (dil * DIL_SUB) == 0 for win, dil in DIL_PATTERNS)
    h = x
    for l in range(depth):
        h, rot_cos, rot_sin = _ffn(h.reshape(b * s, d), ffn1_pre_g[l], ffn1_post_g[l], ffn1_w_gate[l], ffn1_w_up[l],
                                   ffn1_w_down[l], positions=positions.astype(F32).reshape(1, b * s))
        h = h.reshape(b, s, d)
        q, k, vt, gates, dil_qkv = _mixer_in(h, rot_cos, rot_sin, mix_pre_g[l], w_in[l], b_gate[l], q_norm_g[l],
                                             w_uq[l], kv_norm_g[l], w_uk[l], w_uv[l])
        oa = _mla_flash(q, k, vt)
        parts = [_dilated(*qkv, dil) for qkv, (_, dil) in zip(dil_qkv, DIL_PATTERNS)]
        h = _mixer_out(h, oa, parts, gates, w_branch_a[l], w_branch_b[l], w_out[l], mix_post_g[l])
        h = _ffn(h.reshape(b * s, d), ffn2_pre_g[l], ffn2_post_g[l], ffn2_w_gate[l], ffn2_w_up[l],
                 ffn2_w_down[l]).reshape(b, s, d)
    return h
```

```python
import functools

import jax
import jax.numpy as jnp
from jax import lax
from jax.experimental import pallas as pl
from jax.experimental.pallas import tpu as pltpu

F32 = jnp.float32
BF16 = jnp.bfloat16

D_MODEL = 1024
D_FF = 2816
EPS = 1e-6
MLA_HEADS = 8
MLA_Q_RANK = 384
MLA_KV_RANK = 256
MLA_NOPE = 64
MLA_ROPE = 32
MLA_V = 64
MLA_THETA = 10000.0
MLA_WIDTH = MLA_HEADS * MLA_V
DIL_HEADS = 8
DIL_HEAD_DIM = 64
DIL_PATTERNS = ((128, 1), (512, 4), (2048, 16))
DIL_WIDTH = DIL_HEADS * DIL_HEAD_DIM
ROPE_THETA = 500000.0
ROPE_DIM = DIL_HEAD_DIM // 4
N_BRANCH = 2
NEG = -1e30

LANES = 128
SUBLANES = 8
BF16_ROWS = 2 * SUBLANES
HEAD_PAD = LANES
VMEM_LIMIT = 56 * 1024 * 1024

FFN_TM = 1024
FFN_SPLIT = 2
FFN_FC = 256
MIX_TM = 512
FLASH_TQ = 256
FLASH_QT = 4
FLASH_TK = MIX_TM
FLASH_TS = FLASH_TK // 2
VT_ROWS = MLA_V + BF16_ROWS
LOG2E = 1.4426950408889634
DIL_LB = 2048
DIL_SUB = 256
DIL_GROUP = 4
DIL_HALF = 64


def _rms(x, g):
    ms = jnp.mean(x * x, axis=-1, keepdims=True)
    return x * lax.rsqrt(ms + EPS) * g


def _const_spec(shape):
    nd = len(shape)
    return pl.BlockSpec(shape, lambda *_: (0,) * nd, pipeline_mode=pl.Buffered(1))


def _ffn_kernel(x_ref, pre_ref, post_ref, wg_ref, wu_ref, wd_ref, *rest, with_rotary):
    if with_rotary:
        pos_ref, freq_ref, o_ref, cos_ref, sin_ref, hm_ref = rest
        ang = freq_ref[...] * pos_ref[...]
        cos_ref[...] = jnp.cos(ang)
        sin_ref[...] = jnp.sin(ang)
    else:
        o_ref, hm_ref = rest
    rows = x_ref.shape[0] // FFN_SPLIT
    for r in range(FFN_SPLIT):
        rs = slice(rows * r, rows * (r + 1))
        x = x_ref[rs]
        xn = _rms(x, pre_ref[...]).astype(BF16)
        for c0 in range(0, D_FF, FFN_FC):
            sl = slice(c0, min(c0 + FFN_FC, D_FF))
            g = jnp.dot(xn, wg_ref[:, sl], preferred_element_type=F32)
            u = jnp.dot(xn, wu_ref[:, sl], preferred_element_type=F32)
            hm_ref[rs, sl] = (g * jax.nn.sigmoid(g) * u).astype(BF16)
        f = jnp.dot(hm_ref[rs], wd_ref[...], preferred_element_type=F32)
        o_ref[rs] = x + 0.5 * _rms(f, post_ref[...])


def _ffn(h, pre_g, post_g, w_gate, w_up, w_down, positions=None):
    t = h.shape[0]
    tok = lambda c: pl.BlockSpec((FFN_TM, c), lambda i: (i, 0))
    in_specs = [tok(D_MODEL), _const_spec((1, D_MODEL)), _const_spec((1, D_MODEL)),
                _const_spec((D_MODEL, D_FF)), _const_spec((D_MODEL, D_FF)), _const_spec((D_FF, D_MODEL))]
    args = [h, pre_g.reshape(1, D_MODEL), post_g.reshape(1, D_MODEL),
            w_gate.astype(BF16), w_up.astype(BF16), w_down.astype(BF16)]
    out_specs, out_shape = tok(D_MODEL), jax.ShapeDtypeStruct((t, D_MODEL), F32)
    if positions is not None:
        dense = pl.BlockSpec((ROT_ROWS, FFN_TM), lambda i: (0, i))
        in_specs += [pl.BlockSpec((1, FFN_TM), lambda i: (0, i)), _const_spec((ROT_ROWS, 1))]
        args += [positions, _rope_freqs()]
        out_specs = (out_specs, dense, dense)
        out_shape = (out_shape,) + (jax.ShapeDtypeStruct((ROT_ROWS, t), F32),) * 2
    return pl.pallas_call(
        functools.partial(_ffn_kernel, with_rotary=positions is not None),
        name="ffn",
        grid=(t // FFN_TM,),
        in_specs=in_specs,
        out_specs=out_specs,
        out_shape=out_shape,
        scratch_shapes=[pltpu.VMEM((FFN_TM, D_FF), BF16)],
        compiler_params=pltpu.CompilerParams(dimension_semantics=("arbitrary",), vmem_limit_bytes=VMEM_LIMIT),
    )(*args)


def _rope_tile(x, cos, sin):
    return x * cos + pltpu.roll(x, LANES // 2, 1) * sin


def _mixer_in_kernel(h_ref, cos_ref, sin_ref, pre_ref, wa_ref, wd_ref, wg_ref, bg_ref, qg_ref, kvg_ref,
                     wuq_ref, wuk_ref, wuvt_ref,
                     q_ref, k_ref, vt_ref, gate_ref, *rest):
    dil_refs, dscr_ref = rest[:-1], rest[-1]
    u = _rms(h_ref[0], pre_ref[...]).astype(BF16)
    def token_tile(ref, fill):
        pad = jnp.full((LANES // 2 - HALF_A, ref.shape[1]), fill, F32)
        return jnp.concatenate([ref[0:HALF_A], pad, ref[HALF_A:2 * HALF_A], pad], axis=0).T

    cc = token_tile(cos_ref, 1.0)
    ss = token_tile(sin_ref, 0.0)
    rcc = pltpu.roll(cc, LANES // 2, 1)
    rss = pltpu.roll(ss, LANES // 2, 1)
    low = lax.broadcasted_iota(jnp.int32, cc.shape, 1) < LANES // 2
    cos_a = jnp.where(low, cc, rcc)
    sin_a = jnp.where(low, -ss, rss)
    cos_b = jnp.where(low, rcc, cc)
    sin_b = jnp.where(low, -rss, ss)

    pa = jnp.dot(u, wa_ref[...], preferred_element_type=F32)
    cq = pa[:, :MLA_Q_RANK]
    ckv = pa[:, MLA_Q_RANK:MLA_Q_RANK + MLA_KV_RANK]
    kr = _rope_tile(pa[:, MLA_Q_RANK + MLA_KV_RANK:], cos_a, sin_a)
    qn = _rms(cq, qg_ref[...]).astype(BF16)
    ckvn = _rms(ckv, kvg_ref[...]).astype(BF16)
    q = jnp.dot(qn, wuq_ref[...], preferred_element_type=F32)

    pg = jnp.dot(u, wg_ref[...], preferred_element_type=F32) + bg_ref[...]
    gate_ref[0] = jax.nn.sigmoid(pg).astype(BF16)

    kn = jnp.dot(ckvn, wuk_ref[...], preferred_element_type=F32)
    scale = LOG2E * (MLA_NOPE + MLA_ROPE) ** -0.5
    for h in range(MLA_HEADS):
        sl = slice(HEAD_PAD * h, HEAD_PAD * (h + 1))
        qh = _rope_tile(q[:, sl], cos_a, sin_a)
        qt = (qh * scale).T.astype(BF16)
        for j in range(qt.shape[1] // FLASH_TQ):
            q_ref[0, h, j] = qt[:, FLASH_TQ * j:FLASH_TQ * (j + 1)]
        k_ref[0, :, sl] = (kn[:, sl] + kr).astype(BF16)
    vt = lax.dot_general(wuvt_ref[...], ckvn, (((1,), (1,)), ((), ())), preferred_element_type=F32)
    row = lax.broadcasted_iota(jnp.int32, vt.shape, 0)
    vt_ref[0, 0] = jnp.where(row % VT_ROWS >= MLA_V, 1.0, vt).astype(BF16)

    pd = jnp.dot(u, wd_ref[...], preferred_element_type=F32)
    dscale = LOG2E * DIL_HEAD_DIM ** -0.5
    nc = DIL_WIDTH // LANES
    for c in range(nc):
        sl = slice(LANES * c, LANES * (c + 1))
        ks = slice(DIL_WIDTH + LANES * c, DIL_WIDTH + LANES * (c + 1))
        dscr_ref[c] = _rope_tile(pd[:, sl], cos_b, sin_b) * dscale
        dscr_ref[nc + c] = _rope_tile(pd[:, ks], cos_b, sin_b)
        dscr_ref[2 * nc + c] = pd[:, 2 * DIL_WIDTH + LANES * c:2 * DIL_WIDTH + LANES * (c + 1)]
    tm = dscr_ref.shape[1]
    for pi, (_, dil) in enumerate(DIL_PATTERNS):
        for t in range(3):
            for c in range(nc):
                if dil == 1:
                    dil_refs[3 * pi + t][0, 0, :, LANES * c:LANES * (c + 1)] = dscr_ref[nc * t + c].astype(BF16)
                    continue
                for r in range(dil):
                    rows = dscr_ref[nc * t + c, pl.ds(r, tm // dil, stride=dil), :]
                    dil_refs[3 * pi + t][0, r, :, LANES * c:LANES * (c + 1)] = rows.astype(BF16)


HALF_A = MLA_ROPE // 2
HALF_B = ROPE_DIM // 2
NOPE_LO = LANES // 2 - HALF_A
ROT_ROWS = 2 * HALF_A


def _rope_freqs():
    assert 2 * HALF_B == HALF_A
    inv_a = 1.0 / (jnp.float32(MLA_THETA) ** (jnp.arange(HALF_A, dtype=F32) / HALF_A))
    inv_b = 1.0 / (jnp.float32(ROPE_THETA) ** (jnp.arange(HALF_B, dtype=F32) / HALF_B))
    return jnp.concatenate([inv_a, inv_b, inv_b]).reshape(ROT_ROWS, 1)


def _mla_tile_cols(nope, rope):
    lead = (nope if nope is not None else rope).shape[:-1]
    zeros = lambda n: jnp.zeros(lead + (n,), F32)
    x1, x2 = (rope[..., :HALF_A], rope[..., HALF_A:]) if rope is not None else (zeros(HALF_A), zeros(HALF_A))
    n_lo, n_hi = (nope[..., :NOPE_LO], nope[..., NOPE_LO:]) if nope is not None else (zeros(NOPE_LO), zeros(MLA_NOPE - NOPE_LO))
    return jnp.concatenate([x1, n_lo, x2, n_hi, zeros(HEAD_PAD - MLA_NOPE - MLA_ROPE)], axis=-1)


def _dil_tile_cols(w):
    wh = w.reshape(w.shape[0], DIL_HEADS // 2, 2, DIL_HEAD_DIM)
    a, b = wh[:, :, 0], wh[:, :, 1]
    tile = jnp.concatenate([a[..., :HALF_B], b[..., :HALF_B], a[..., ROPE_DIM:], a[..., HALF_B:ROPE_DIM],
                            b[..., HALF_B:ROPE_DIM], b[..., ROPE_DIM:]], axis=-1)
    return tile.reshape(w.shape[0], DIL_WIDTH)


def _mixer_in(h1, rot_cos, rot_sin, mix_pre_g, w_in, b_gate, q_norm_g, w_uq, kv_norm_g, w_uk, w_uv):
    b, s, _ = h1.shape
    o0 = 0
    o1 = o0 + MLA_Q_RANK
    o2 = o1 + MLA_KV_RANK
    o3 = o2 + MLA_ROPE
    o4 = o3 + 3 * DIL_WIDTH
    w_kr = _mla_tile_cols(None, w_in[:, o2:o3])
    wa = jnp.concatenate([w_in[:, o0:o2], w_kr], axis=1).astype(BF16)
    wd = jnp.concatenate([_dil_tile_cols(w_in[:, o3:o3 + DIL_WIDTH]),
                          _dil_tile_cols(w_in[:, o3 + DIL_WIDTH:o3 + 2 * DIL_WIDTH]),
                          w_in[:, o3 + 2 * DIL_WIDTH:o4]], axis=1).astype(BF16)
    wg = w_in[:, o4:].astype(BF16)
    wuq_h = w_uq.reshape(MLA_Q_RANK, MLA_HEADS, MLA_NOPE + MLA_ROPE)
    wuq = _mla_tile_cols(wuq_h[..., :MLA_NOPE], wuq_h[..., MLA_NOPE:]).reshape(MLA_Q_RANK, -1).astype(BF16)
    wuk = _mla_tile_cols(w_uk.reshape(MLA_KV_RANK, MLA_HEADS, MLA_NOPE), None).reshape(MLA_KV_RANK, -1).astype(BF16)
    wuvt = jnp.pad(w_uv.T.reshape(MLA_HEADS, MLA_V, MLA_KV_RANK), ((0, 0), (0, VT_ROWS - MLA_V), (0, 0))
                   ).reshape(MLA_HEADS * VT_ROWS, MLA_KV_RANK).astype(BF16)
    tm = MIX_TM
    hp = MLA_HEADS * HEAD_PAD
    tok = lambda c: pl.BlockSpec((1, tm, c), lambda bi, i: (bi, i, 0))
    rot = pl.BlockSpec((ROT_ROWS, tm), lambda bi, i: (0, bi * (s // tm) + i))
    out_shapes = (
        jax.ShapeDtypeStruct((b, MLA_HEADS, s // FLASH_TQ, HEAD_PAD, FLASH_TQ), BF16),
        jax.ShapeDtypeStruct((b, s, hp), BF16),
        jax.ShapeDtypeStruct((b, s // FLASH_TK, MLA_HEADS * VT_ROWS, FLASH_TK), BF16),
        jax.ShapeDtypeStruct((b, s, N_BRANCH * D_MODEL), BF16),
    )
    per_chunk = FLASH_TK // tm
    out_specs = (
        pl.BlockSpec((1, MLA_HEADS, tm // FLASH_TQ, HEAD_PAD, FLASH_TQ), lambda bi, i: (bi, 0, i, 0, 0)),
        tok(hp),
        pl.BlockSpec((1, 1, MLA_HEADS * VT_ROWS, tm), lambda bi, i: (bi, i // per_chunk, 0, i % per_chunk)),
        tok(N_BRANCH * D_MODEL),
    )
    for _, dil in DIL_PATTERNS:
        out_shapes += (jax.ShapeDtypeStruct((b, dil, s // dil, DIL_WIDTH), BF16),) * 3
        out_specs += (pl.BlockSpec((1, dil, tm // dil, DIL_WIDTH), lambda bi, i: (bi, 0, i, 0)),) * 3
    outs = pl.pallas_call(
        _mixer_in_kernel,
        name="mixer_in",
        grid=(b, s // tm),
        in_specs=[
            tok(D_MODEL), rot, rot,
            _const_spec((1, D_MODEL)),
            _const_spec(wa.shape), _const_spec(wd.shape), _const_spec(wg.shape),
            _const_spec((1, N_BRANCH * D_MODEL)),
            _const_spec((1, MLA_Q_RANK)), _const_spec((1, MLA_KV_RANK)),
            _const_spec(wuq.shape), _const_spec(wuk.shape), _const_spec(wuvt.shape),
        ],
        out_specs=out_specs,
        out_shape=out_shapes,
        scratch_shapes=[pltpu.VMEM((3 * DIL_WIDTH // LANES, tm, LANES), F32)],
        compiler_params=pltpu.CompilerParams(dimension_semantics=("arbitrary", "arbitrary"),
                                             vmem_limit_bytes=VMEM_LIMIT),
    )(h1, rot_cos, rot_sin, mix_pre_g.reshape(1, D_MODEL), wa, wd, wg, b_gate.reshape(1, -1),
      q_norm_g.reshape(1, -1), kv_norm_g.reshape(1, -1), wuq, wuk, wuvt)
    q, k, vt, gates = outs[:4]
    dil_qkv = [outs[4 + 3 * i:7 + 3 * i] for i in range(len(DIL_PATTERNS))]
    return q, k, vt, gates, dil_qkv


def _flash_kernel(qt_ref, k_ref, vt_ref, o_ref, s0_ref, s1_ref, s2_ref, s3_ref):
    nk = k_ref.shape[1] // FLASH_TK
    nq = qt_ref.shape[2]
    assert nk % 4 == 0 and nq % FLASH_QT == 0
    bufs = (s0_ref, s1_ref, s2_ref, s3_ref)

    chains = [(tt, hh) for tt in range(FLASH_QT) for hh in range(2)]

    def scores(t, c, ch, buf):
        tt, hh = chains[ch]
        cmax = None
        for part in range(FLASH_TK // FLASH_TS):
            k0 = c * FLASH_TK + part * FLASH_TS
            if not isinstance(k0, int):
                k0 = pl.multiple_of(k0, FLASH_TS)
            k = k_ref[0, pl.ds(k0, FLASH_TS), HEAD_PAD * hh:HEAD_PAD * (hh + 1)]
            st = jnp.dot(k, qt_ref[0, hh, t * FLASH_QT + tt], preferred_element_type=F32)
            buf[ch, FLASH_TS * part:FLASH_TS * (part + 1)] = st
            pm = jnp.max(st, axis=0, keepdims=True)
            cmax = pm if cmax is None else jnp.maximum(cmax, pm)
        return cmax

    def consume(c, ch, buf, m, cmax, acc):
        hh = chains[ch][1]
        m_new = jnp.maximum(m, cmax)
        alpha = jnp.exp2(m - m_new)
        p = jnp.exp2(buf[ch] - m_new).astype(BF16)
        vt = vt_ref[0, c, VT_ROWS * hh:VT_ROWS * (hh + 1), :]
        return m_new, alpha * acc + jnp.dot(vt, p, preferred_element_type=F32)

    def half(c, state, pair_in, pair_out, t_next, c_next):
        new = []
        for ch in range(len(chains)):
            m, cm0, cm1, acc = state[ch]
            nxt = [scores(t_next, c_next + i, ch, bufs[pair_out + i]) for i in range(2)]
            m, acc = consume(c, ch, bufs[pair_in], m, cm0, acc)
            m, acc = consume(c + 1, ch, bufs[pair_in + 1], m, cm1, acc)
            new.append((m, nxt[0], nxt[1], acc))
        return tuple(new)

    def tile(t, cmaxes):
        state = tuple((jnp.full((1, FLASH_TQ), -jnp.inf, F32), cmaxes[ch][0], cmaxes[ch][1],
                       jnp.zeros((VT_ROWS, FLASH_TQ), F32)) for ch in range(len(chains)))

        def group(j, st):
            c = 4 * j
            return half(c + 2, half(c, st, 0, 2, t, c + 2), 2, 0, t, c + 4)

        state = lax.fori_loop(0, nk // 4 - 1, group, state)
        t_next = jnp.minimum(t + 1, nq // FLASH_QT - 1)
        state = half(nk - 2, half(nk - 4, state, 0, 2, t, nk - 2), 2, 0, t_next, 0)
        for tt in range(FLASH_QT):
            outs = [acc[:MLA_V] / acc[MLA_V:MLA_V + 1] for (_, _, _, acc) in state[2 * tt:2 * tt + 2]]
            row0 = pl.multiple_of((t * FLASH_QT + tt) * FLASH_TQ, FLASH_TQ)
            o_ref[0, pl.ds(row0, FLASH_TQ), :] = jnp.concatenate(outs, axis=0).T.astype(BF16)
        return tuple((cm0, cm1) for (_, cm0, cm1, _) in state)

    first = tuple((scores(0, 0, ch, bufs[0]), scores(0, 1, ch, bufs[1])) for ch in range(len(chains)))
    lax.fori_loop(0, nq // FLASH_QT, tile, first)


def _mla_flash(qt, k, vt):
    b, s, _ = k.shape
    nk = s // FLASH_TK
    return pl.pallas_call(
        _flash_kernel,
        name="mla_flash",
        grid=(b, MLA_HEADS // 2),
        in_specs=[
            pl.BlockSpec((1, 2, s // FLASH_TQ, HEAD_PAD, FLASH_TQ), lambda bi, hp: (bi, hp, 0, 0, 0)),
            pl.BlockSpec((1, s, 2 * HEAD_PAD), lambda bi, hp: (bi, 0, hp)),
            pl.BlockSpec((1, nk, 2 * VT_ROWS, FLASH_TK), lambda bi, hp: (bi, 0, hp, 0)),
        ],
        out_specs=pl.BlockSpec((1, s, 2 * MLA_V), lambda bi, hp: (bi, 0, hp)),
        out_shape=jax.ShapeDtypeStruct((b, s, MLA_WIDTH), BF16),
        scratch_shapes=[pltpu.VMEM((2 * FLASH_QT, FLASH_TK, FLASH_TQ), F32)] * 4,
        compiler_params=pltpu.CompilerParams(dimension_semantics=("arbitrary", "arbitrary"),
                                             vmem_limit_bytes=VMEM_LIMIT),
    )(qt, k, vt)


def _dilated_kernel(q_ref, kc_ref, kp_ref, kn_ref, vc_ref, vp_ref, vn_ref, o_ref, lse_ref,
                    kw_ref, vt_ref, ot_ref, lt_ref, *sbufs, n_rows):
    lb = q_ref.shape[2]
    l0 = pl.program_id(2) * lb
    wrows = lb + 2 * DIL_HALF
    kw_ref[0:DIL_HALF] = kp_ref[0, 0]
    kw_ref[DIL_HALF:DIL_HALF + lb] = kc_ref[0, 0]
    kw_ref[DIL_HALF + lb:] = kn_ref[0, 0]
    vw = jnp.concatenate([vp_ref[0, 0], vc_ref[0, 0], vn_ref[0, 0]], axis=0).astype(F32)
    vt = vw.T.astype(BF16)
    ones = jnp.ones((VT_ROWS - DIL_HEAD_DIM, wrows), BF16)
    for h in range(DIL_HEADS):
        vt_ref[VT_ROWS * h:VT_ROWS * h + DIL_HEAD_DIM] = vt[DIL_HEAD_DIM * h:DIL_HEAD_DIM * (h + 1)]
        vt_ref[VT_ROWS * h + DIL_HEAD_DIM:VT_ROWS * (h + 1)] = ones

    win = DIL_SUB + 2 * DIL_HALF
    stripe_row = lax.broadcasted_iota(jnp.int32, (2 * LANES, LANES), 0)
    stripe_col = lax.broadcasted_iota(jnp.int32, (2 * LANES, LANES), 1)
    assert DIL_SUB == 2 * LANES and 2 * DIL_HALF == LANES
    live = ((slice(0, 2 * LANES), slice(0, LANES)), (slice(LANES, 3 * LANES), slice(LANES, 2 * LANES)))
    lane = lax.broadcasted_iota(jnp.int32, (DIL_SUB, LANES), 1)
    odd_lanes = ((lane >= HALF_B) & (lane < ROPE_DIM)) | (lane >= LANES // 2 + HALF_B)
    nbuf = len(sbufs)

    def produce(u):
        j, h = divmod(u, DIL_HEADS)
        ls = slice(LANES * (h // 2), LANES * (h // 2 + 1))
        kpair = kw_ref[j * DIL_SUB:j * DIL_SUB + win, ls]
        qpair = q_ref[0, 0, j * DIL_SUB:(j + 1) * DIL_SUB, ls]
        qh = jnp.where(odd_lanes if h % 2 else ~odd_lanes, qpair, jnp.zeros_like(qpair))
        st = lax.dot_general(kpair, qh, (((1,), (1,)), ((), ())), preferred_element_type=F32)
        cmax = []
        for rs, cs in live:
            srow = stripe_row + rs.start
            kidx = l0 - DIL_HALF + j * DIL_SUB + srow
            valid = ((srow - (stripe_col + cs.start)).astype(jnp.uint32) <= 2 * DIL_HALF) & (kidx.astype(jnp.uint32) < n_rows)
            sp = jnp.where(valid, st[rs, cs], NEG)
            sbufs[u % nbuf][rs, cs] = sp
            cmax.append(jnp.max(sp, axis=0, keepdims=True))
        return cmax

    def consume(u, m):
        j, h = divmod(u, DIL_HEADS)
        zero = jnp.zeros((LANES, LANES), BF16)
        cols = []
        for q, (rs, cs) in enumerate(live):
            pq = jnp.exp2(sbufs[u % nbuf][rs, cs] - m[q]).astype(BF16)
            cols.append(jnp.concatenate([pq, zero] if q == 0 else [zero, pq], axis=0))
        p = jnp.concatenate(cols, axis=1)
        r = jnp.dot(vt_ref[VT_ROWS * h:VT_ROWS * (h + 1), j * DIL_SUB:j * DIL_SUB + win], p,
                    preferred_element_type=F32)
        hs = slice(DIL_HEAD_DIM * h, DIL_HEAD_DIM * (h + 1))
        ot_ref[hs] = r[:DIL_HEAD_DIM] / r[DIL_HEAD_DIM:DIL_HEAD_DIM + 1]
        den = r[DIL_HEAD_DIM:DIL_HEAD_DIM + SUBLANES]
        for q, (_, cs) in enumerate(live):
            lt_ref[h:h + 1, cs] = m[q] + jnp.log2(den[:, cs][0:1])
        if h == DIL_HEADS - 1:
            o_ref[0, 0, j * DIL_SUB:(j + 1) * DIL_SUB, :] = ot_ref[...].T.astype(BF16)
            lse_ref[0, 0, j * DIL_SUB:(j + 1) * DIL_SUB, :] = lt_ref[...].T

    lt_ref[...] = jnp.zeros_like(lt_ref)
    n_units = (lb // DIL_SUB) * DIL_HEADS
    grp = nbuf // 2
    m_cur = [produce(i) for i in range(grp)]
    for u in range(0, n_units, grp):
        m_next = [produce(u + grp + i) for i in range(grp)] if u + grp < n_units else None
        for i in range(grp):
            consume(u + i, m_cur[i])
        m_cur = m_next


def _dilated(dq, dk, dv, dilation):
    b, _, n_rows, w = dq.shape
    lb = min(DIL_LB, n_rows)
    hb = lb // DIL_HALF
    last = n_rows // DIL_HALF - 1
    cur = pl.BlockSpec((1, 1, lb, w), lambda bi, r, l: (bi, r, l, 0))
    prev = pl.BlockSpec((1, 1, DIL_HALF, w), lambda bi, r, l: (bi, r, jnp.maximum(l * hb - 1, 0), 0))
    nxt = pl.BlockSpec((1, 1, DIL_HALF, w), lambda bi, r, l: (bi, r, jnp.minimum((l + 1) * hb, last), 0))
    return pl.pallas_call(
        functools.partial(_dilated_kernel, n_rows=n_rows),
        name=f"dilated_{dilation}",
        grid=(b, dilation, n_rows // lb),
        in_specs=[cur, cur, prev, nxt, cur, prev, nxt],
        out_specs=(cur, pl.BlockSpec((1, 1, lb, LANES), lambda bi, r, l: (bi, r, l, 0))),
        out_shape=(jax.ShapeDtypeStruct((b, dilation, n_rows, w), BF16),
                   jax.ShapeDtypeStruct((b, dilation, n_rows, LANES), F32)),
        scratch_shapes=[pltpu.VMEM((lb + 2 * DIL_HALF, w), BF16),
                        pltpu.VMEM((DIL_HEADS * VT_ROWS, lb + 2 * DIL_HALF), BF16),
                        pltpu.VMEM((w, DIL_SUB), F32),
                        pltpu.VMEM((LANES, DIL_SUB), F32)]
                       + [pltpu.VMEM((DIL_SUB + 2 * DIL_HALF, DIL_SUB), F32)] * (2 * DIL_GROUP),
        compiler_params=pltpu.CompilerParams(dimension_semantics=("arbitrary",) * 3, vmem_limit_bytes=VMEM_LIMIT),
    )(dq, dk, dk, dk, dv, dv, dv)


def _mixer_out_kernel(h_ref, oa_ref, o1_ref, l1_ref, o2_ref, l2_ref, o3_ref, l3_ref, gate_ref,
                      wba_ref, wbb_ref, wo_ref, post_ref, expand_ref, out_ref, *scr):
    tm = h_ref.shape[1]

    def token_major(ref, dil, buf):
        if dil == 1:
            return ref[0, 0].astype(F32)
        nc = ref.shape[-1] // LANES
        for r in range(dil):
            for c in range(nc):
                buf[c, pl.ds(r, tm // dil, stride=dil), :] = ref[0, r, :, LANES * c:LANES * (c + 1)].astype(F32)
        return jnp.concatenate([buf[c] for c in range(nc)], axis=1)

    bufs = iter(scr)
    os_, ls_ = [], []
    for (o_r, l_r), (_, dil) in zip(((o1_ref, l1_ref), (o2_ref, l2_ref), (o3_ref, l3_ref)), DIL_PATTERNS):
        os_.append(token_major(o_r, dil, None if dil == 1 else next(bufs)))
        ls_.append(token_major(l_r, dil, None if dil == 1 else next(bufs)))
    lm = jnp.maximum(jnp.maximum(ls_[0], ls_[1]), ls_[2])
    es = [jnp.exp2(l - lm) for l in ls_]
    inv = 1.0 / (es[0] + es[1] + es[2])
    ob = None
    for e, o in zip(es, os_):
        w = e * inv
        hi = w.astype(BF16)
        lo = (w - hi.astype(F32)).astype(BF16)
        wide = jnp.dot(jnp.concatenate([hi, lo], axis=1), expand_ref[...], preferred_element_type=F32)
        ob = wide * o if ob is None else ob + wide * o
    ya = jnp.dot(oa_ref[0], wba_ref[...], preferred_element_type=F32)
    yb = jnp.dot(ob.astype(BF16), wbb_ref[...], preferred_element_type=F32)
    gate = gate_ref[0].astype(F32)
    merged = gate[:, :D_MODEL] * ya + gate[:, D_MODEL:] * yb
    mix = jnp.dot(merged.astype(BF16), wo_ref[...], preferred_element_type=F32)
    out_ref[0] = h_ref[0] + _rms(mix, post_ref[...])


def _mixer_out(h1, oa, parts, gates, w_branch_a, w_branch_b, w_out, mix_post_g):
    b, s, _ = h1.shape
    tm = MIX_TM
    tok = lambda c: pl.BlockSpec((1, tm, c), lambda bi, i: (bi, i, 0))
    flat, part_specs, scratch = [], [], []
    for (_, dil), pair in zip(DIL_PATTERNS, parts):
        flat += list(pair)
        for width in (DIL_WIDTH, LANES):
            part_specs.append(pl.BlockSpec((1, dil, tm // dil, width), lambda bi, i: (bi, 0, i, 0)))
            if dil > 1:
                scratch.append(pltpu.VMEM((width // LANES, tm, LANES), F32))
    head_of_lane = jnp.arange(DIL_WIDTH) // DIL_HEAD_DIM
    expand = (jnp.arange(2 * LANES)[:, None] % LANES == head_of_lane[None, :]).astype(BF16)
    return pl.pallas_call(
        _mixer_out_kernel,
        name="mixer_out",
        grid=(b, s // tm),
        scratch_shapes=scratch,
        in_specs=[tok(D_MODEL), tok(MLA_WIDTH)] + part_specs + [tok(N_BRANCH * D_MODEL),
                  _const_spec((MLA_WIDTH, D_MODEL)), _const_spec((DIL_WIDTH, D_MODEL)),
                  _const_spec((D_MODEL, D_MODEL)), _const_spec((1, D_MODEL)), _const_spec((2 * LANES, DIL_WIDTH))],
        out_specs=tok(D_MODEL),
        out_shape=jax.ShapeDtypeStruct((b, s, D_MODEL), F32),
        compiler_params=pltpu.CompilerParams(dimension_semantics=("arbitrary", "arbitrary"),
                                             vmem_limit_bytes=VMEM_LIMIT),
    )(h1, oa, *flat, gates, w_branch_a.astype(BF16), w_branch_b.astype(BF16), w_out.astype(BF16),
      mix_post_g.reshape(1, D_MODEL), expand)


def kernel(x, positions, ffn1_pre_g, ffn1_post_g, ffn1_w_gate, ffn1_w_up, ffn1_w_down, mix_pre_g, w_in, b_gate,
           q_norm_g, w_uq, kv_norm_g, w_uk, w_uv, w_branch_a, w_branch_b, w_out, mix_post_g,
           ffn2_pre_g, ffn2_post_g, ffn2_w_gate, ffn2_w_up, ffn2_w_down):
    b, s, d = x.shape
    depth = ffn1_pre_g.shape[0]
    assert all(win // (2 * dil) == DIL_HALF and s % (dil * DIL_SUB) == 0 for win, dil in DIL_PATTERNS)
    h = x
    for l in range(depth):
        h, rot_cos, rot_sin = _ffn(h.reshape(b * s, d), ffn1_pre_g[l], ffn1_post_g[l], ffn1_w_gate[l], ffn1_w_up[l],
                                   ffn1_w_down[l], positions=positions.astype(F32).reshape(1, b * s))
        h = h.reshape(b, s, d)
        q, k, vt, gates, dil_qkv = _mixer_in(h, rot_cos, rot_sin, mix_pre_g[l], w_in[l], b_gate[l], q_norm_g[l],
                                             w_uq[l], kv_norm_g[l], w_uk[l], w_uv[l])
        oa = _mla_flash(q, k, vt)
        parts = [_dilated(*qkv, dil) for qkv, (_, dil) in zip(dil_qkv, DIL_PATTERNS)]
        h = _mixer_out(h, oa, parts, gates, w_branch_a[l], w_branch_b[l], w_out[l], mix_post_g[l])
        h = _ffn(h.reshape(b * s, d), ffn2_pre_g[l], ffn2_post_g[l], ffn2_w_gate[l], ffn2_w_up[l],
                 ffn2_w_down[l]).reshape(b, s, d)
    return h
```

```python
import functools

import jax
import jax.numpy as jnp
from jax import lax
from jax.experimental import pallas as pl
from jax.experimental.pallas import tpu as pltpu

F32 = jnp.float32
BF16 = jnp.bfloat16

D_MODEL = 1024
D_FF = 2816
EPS = 1e-6
MLA_HEADS = 8
MLA_Q_RANK = 384
MLA_KV_RANK = 256
MLA_NOPE = 64
MLA_ROPE = 32
MLA_V = 64
MLA_THETA = 10000.0
MLA_WIDTH = MLA_HEADS * MLA_V
DIL_HEADS = 8
DIL_HEAD_DIM = 64
DIL_PATTERNS = ((128, 1), (512, 4), (2048, 16))
DIL_WIDTH = DIL_HEADS * DIL_HEAD_DIM
ROPE_THETA = 500000.0
ROPE_DIM = DIL_HEAD_DIM // 4
N_BRANCH = 2
NEG = -1e30

LANES = 128
SUBLANES = 8
BF16_ROWS = 2 * SUBLANES
HEAD_PAD = LANES
VMEM_LIMIT = 56 * 1024 * 1024

FFN_TM = 1024
FFN_SPLIT = 4
FFN_FC = 256
MIX_TM = 512
MIXOUT_TM = 1024
FLASH_TQ = 256
FLASH_QT = 4
FLASH_TK = MIX_TM
FLASH_TS = FLASH_TK // 2
VT_ROWS = MLA_V + BF16_ROWS
LOG2E = 1.4426950408889634
DIL_LB = 2048
DIL_SUB = 256
DIL_GROUP = 4
DIL_HALF = 64


def _rms(x, g):
    ms = jnp.mean(x * x, axis=-1, keepdims=True)
    return x * lax.rsqrt(ms + EPS) * g


def _const_spec(shape):
    nd = len(shape)
    return pl.BlockSpec(shape, lambda *_: (0,) * nd, pipeline_mode=pl.Buffered(1))


def _ffn_kernel(x_ref, pre_ref, post_ref, wg_ref, wu_ref, wd_ref, *rest, with_rotary):
    if with_rotary:
        pos_ref, freq_ref, o_ref, cos_ref, sin_ref, hm_ref = rest
        ang = freq_ref[...] * pos_ref[...]
        cos_ref[...] = jnp.cos(ang)
        sin_ref[...] = jnp.sin(ang)
    else:
        o_ref, hm_ref = rest
    rows = x_ref.shape[0] // FFN_SPLIT
    for r in range(FFN_SPLIT):
        rs = slice(rows * r, rows * (r + 1))
        x = x_ref[rs]
        xn = _rms(x, pre_ref[...]).astype(BF16)
        for c0 in range(0, D_FF, FFN_FC):
            sl = slice(c0, min(c0 + FFN_FC, D_FF))
            g = jnp.dot(xn, wg_ref[:, sl], preferred_element_type=F32)
            u = jnp.dot(xn, wu_ref[:, sl], preferred_element_type=F32)
            hm_ref[rs, sl] = (g * jax.nn.sigmoid(g) * u).astype(BF16)
        f = jnp.dot(hm_ref[rs], wd_ref[...], preferred_element_type=F32)
        o_ref[rs] = x + 0.5 * _rms(f, post_ref[...])


def _ffn(h, pre_g, post_g, w_gate, w_up, w_down, positions=None):
    t = h.shape[0]
    tok = lambda c: pl.BlockSpec((FFN_TM, c), lambda i: (i, 0))
    in_specs = [tok(D_MODEL), _const_spec((1, D_MODEL)), _const_spec((1, D_MODEL)),
                _const_spec((D_MODEL, D_FF)), _const_spec((D_MODEL, D_FF)), _const_spec((D_FF, D_MODEL))]
    args = [h, pre_g.reshape(1, D_MODEL), post_g.reshape(1, D_MODEL),
            w_gate.astype(BF16), w_up.astype(BF16), w_down.astype(BF16)]
    out_specs, out_shape = tok(D_MODEL), jax.ShapeDtypeStruct((t, D_MODEL), F32)
    if positions is not None:
        dense = pl.BlockSpec((ROT_ROWS, FFN_TM), lambda i: (0, i))
        in_specs += [pl.BlockSpec((1, FFN_TM), lambda i: (0, i)), _const_spec((ROT_ROWS, 1))]
        args += [positions, _rope_freqs()]
        out_specs = (out_specs, dense, dense)
        out_shape = (out_shape,) + (jax.ShapeDtypeStruct((ROT_ROWS, t), F32),) * 2
    return pl.pallas_call(
        functools.partial(_ffn_kernel, with_rotary=positions is not None),
        name="ffn",
        grid=(t // FFN_TM,),
        in_specs=in_specs,
        out_specs=out_specs,
        out_shape=out_shape,
        scratch_shapes=[pltpu.VMEM((FFN_TM, D_FF), BF16)],
        compiler_params=pltpu.CompilerParams(dimension_semantics=("arbitrary",), vmem_limit_bytes=VMEM_LIMIT),
    )(*args)


def _rope_tile(x, cos, sin):
    return x * cos + pltpu.roll(x, LANES // 2, 1) * sin


def _mixer_in_kernel(h_ref, cos_ref, sin_ref, pre_ref, wa_ref, wd_ref, wg_ref, bg_ref, qg_ref, kvg_ref,
                     wuq_ref, wuk_ref, wuvt_ref,
                     q_ref, k_ref, vt_ref, gate_ref, *rest):
    dil_refs, dscr_ref = rest[:-1], rest[-1]
    u = _rms(h_ref[0], pre_ref[...]).astype(BF16)
    def token_tile(ref, fill):
        pad = jnp.full((LANES // 2 - HALF_A, ref.shape[1]), fill, F32)
        return jnp.concatenate([ref[0:HALF_A], pad, ref[HALF_A:2 * HALF_A], pad], axis=0).T

    cc = token_tile(cos_ref, 1.0)
    ss = token_tile(sin_ref, 0.0)
    rcc = pltpu.roll(cc, LANES // 2, 1)
    rss = pltpu.roll(ss, LANES // 2, 1)
    low = lax.broadcasted_iota(jnp.int32, cc.shape, 1) < LANES // 2
    cos_a = jnp.where(low, cc, rcc)
    sin_a = jnp.where(low, -ss, rss)
    cos_b = jnp.where(low, rcc, cc)
    sin_b = jnp.where(low, -rss, ss)

    pa = jnp.dot(u, wa_ref[...], preferred_element_type=F32)
    cq = pa[:, :MLA_Q_RANK]
    ckv = pa[:, MLA_Q_RANK:MLA_Q_RANK + MLA_KV_RANK]
    kr = _rope_tile(pa[:, MLA_Q_RANK + MLA_KV_RANK:], cos_a, sin_a)
    qn = _rms(cq, qg_ref[...]).astype(BF16)
    ckvn = _rms(ckv, kvg_ref[...]).astype(BF16)
    q = jnp.dot(qn, wuq_ref[...], preferred_element_type=F32)

    pg = jnp.dot(u, wg_ref[...], preferred_element_type=F32) + bg_ref[...]
    gate_ref[0] = jax.nn.sigmoid(pg).astype(BF16)

    kn = jnp.dot(ckvn, wuk_ref[...], preferred_element_type=F32)
    scale = LOG2E * (MLA_NOPE + MLA_ROPE) ** -0.5
    for h in range(MLA_HEADS):
        sl = slice(HEAD_PAD * h, HEAD_PAD * (h + 1))
        qh = _rope_tile(q[:, sl], cos_a, sin_a)
        qt = (qh * scale).T.astype(BF16)
        for j in range(qt.shape[1] // FLASH_TQ):
            q_ref[0, h, j] = qt[:, FLASH_TQ * j:FLASH_TQ * (j + 1)]
        k_ref[0, :, sl] = (kn[:, sl] + kr).astype(BF16)
    vt = lax.dot_general(wuvt_ref[...], ckvn, (((1,), (1,)), ((), ())), preferred_element_type=F32)
    row = lax.broadcasted_iota(jnp.int32, vt.shape, 0)
    vt_ref[0, 0] = jnp.where(row % VT_ROWS >= MLA_V, 1.0, vt).astype(BF16)

    pd = jnp.dot(u, wd_ref[...], preferred_element_type=F32)
    dscale = LOG2E * DIL_HEAD_DIM ** -0.5
    nc = DIL_WIDTH // LANES
    for c in range(nc):
        sl = slice(LANES * c, LANES * (c + 1))
        ks = slice(DIL_WIDTH + LANES * c, DIL_WIDTH + LANES * (c + 1))
        dscr_ref[c] = _rope_tile(pd[:, sl], cos_b, sin_b) * dscale
        dscr_ref[nc + c] = _rope_tile(pd[:, ks], cos_b, sin_b)
        dscr_ref[2 * nc + c] = pd[:, 2 * DIL_WIDTH + LANES * c:2 * DIL_WIDTH + LANES * (c + 1)]
    tm = dscr_ref.shape[1]
    for pi, (_, dil) in enumerate(DIL_PATTERNS):
        for t in range(3):
            for c in range(nc):
                if dil == 1:
                    dil_refs[3 * pi + t][0, 0, :, LANES * c:LANES * (c + 1)] = dscr_ref[nc * t + c].astype(BF16)
                    continue
                for r in range(dil):
                    rows = dscr_ref[nc * t + c, pl.ds(r, tm // dil, stride=dil), :]
                    dil_refs[3 * pi + t][0, r, :, LANES * c:LANES * (c + 1)] = rows.astype(BF16)


HALF_A = MLA_ROPE // 2
HALF_B = ROPE_DIM // 2
NOPE_LO = LANES // 2 - HALF_A
ROT_ROWS = 2 * HALF_A


def _rope_freqs():
    assert 2 * HALF_B == HALF_A
    inv_a = 1.0 / (jnp.float32(MLA_THETA) ** (jnp.arange(HALF_A, dtype=F32) / HALF_A))
    inv_b = 1.0 / (jnp.float32(ROPE_THETA) ** (jnp.arange(HALF_B, dtype=F32) / HALF_B))
    return jnp.concatenate([inv_a, inv_b, inv_b]).reshape(ROT_ROWS, 1)


def _mla_tile_cols(nope, rope):
    lead = (nope if nope is not None else rope).shape[:-1]
    zeros = lambda n: jnp.zeros(lead + (n,), F32)
    x1, x2 = (rope[..., :HALF_A], rope[..., HALF_A:]) if rope is not None else (zeros(HALF_A), zeros(HALF_A))
    n_lo, n_hi = (nope[..., :NOPE_LO], nope[..., NOPE_LO:]) if nope is not None else (zeros(NOPE_LO), zeros(MLA_NOPE - NOPE_LO))
    return jnp.concatenate([x1, n_lo, x2, n_hi, zeros(HEAD_PAD - MLA_NOPE - MLA_ROPE)], axis=-1)


def _dil_tile_cols(w):
    wh = w.reshape(w.shape[0], DIL_HEADS // 2, 2, DIL_HEAD_DIM)
    a, b = wh[:, :, 0], wh[:, :, 1]
    tile = jnp.concatenate([a[..., :HALF_B], b[..., :HALF_B], a[..., ROPE_DIM:], a[..., HALF_B:ROPE_DIM],
                            b[..., HALF_B:ROPE_DIM], b[..., ROPE_DIM:]], axis=-1)
    return tile.reshape(w.shape[0], DIL_WIDTH)


def _mixer_in(h1, rot_cos, rot_sin, mix_pre_g, w_in, b_gate, q_norm_g, w_uq, kv_norm_g, w_uk, w_uv):
    b, s, _ = h1.shape
    o0 = 0
    o1 = o0 + MLA_Q_RANK
    o2 = o1 + MLA_KV_RANK
    o3 = o2 + MLA_ROPE
    o4 = o3 + 3 * DIL_WIDTH
    w_kr = _mla_tile_cols(None, w_in[:, o2:o3])
    wa = jnp.concatenate([w_in[:, o0:o2], w_kr], axis=1).astype(BF16)
    wd = jnp.concatenate([_dil_tile_cols(w_in[:, o3:o3 + DIL_WIDTH]),
                          _dil_tile_cols(w_in[:, o3 + DIL_WIDTH:o3 + 2 * DIL_WIDTH]),
                          w_in[:, o3 + 2 * DIL_WIDTH:o4]], axis=1).astype(BF16)
    wg = w_in[:, o4:].astype(BF16)
    wuq_h = w_uq.reshape(MLA_Q_RANK, MLA_HEADS, MLA_NOPE + MLA_ROPE)
    wuq = _mla_tile_cols(wuq_h[..., :MLA_NOPE], wuq_h[..., MLA_NOPE:]).reshape(MLA_Q_RANK, -1).astype(BF16)
    wuk = _mla_tile_cols(w_uk.reshape(MLA_KV_RANK, MLA_HEADS, MLA_NOPE), None).reshape(MLA_KV_RANK, -1).astype(BF16)
    wuvt = jnp.pad(w_uv.T.reshape(MLA_HEADS, MLA_V, MLA_KV_RANK), ((0, 0), (0, VT_ROWS - MLA_V), (0, 0))
                   ).reshape(MLA_HEADS * VT_ROWS, MLA_KV_RANK).astype(BF16)
    tm = MIX_TM
    hp = MLA_HEADS * HEAD_PAD
    tok = lambda c: pl.BlockSpec((1, tm, c), lambda bi, i: (bi, i, 0))
    rot = pl.BlockSpec((ROT_ROWS, tm), lambda bi, i: (0, bi * (s // tm) + i))
    out_shapes = (
        jax.ShapeDtypeStruct((b, MLA_HEADS, s // FLASH_TQ, HEAD_PAD, FLASH_TQ), BF16),
        jax.ShapeDtypeStruct((b, s, hp), BF16),
        jax.ShapeDtypeStruct((b, s // FLASH_TK, MLA_HEADS * VT_ROWS, FLASH_TK), BF16),
        jax.ShapeDtypeStruct((b, s, N_BRANCH * D_MODEL), BF16),
    )
    per_chunk = FLASH_TK // tm
    out_specs = (
        pl.BlockSpec((1, MLA_HEADS, tm // FLASH_TQ, HEAD_PAD, FLASH_TQ), lambda bi, i: (bi, 0, i, 0, 0)),
        tok(hp),
        pl.BlockSpec((1, 1, MLA_HEADS * VT_ROWS, tm), lambda bi, i: (bi, i // per_chunk, 0, i % per_chunk)),
        tok(N_BRANCH * D_MODEL),
    )
    for _, dil in DIL_PATTERNS:
        out_shapes += (jax.ShapeDtypeStruct((b, dil, s // dil, DIL_WIDTH), BF16),) * 3
        out_specs += (pl.BlockSpec((1, dil, tm // dil, DIL_WIDTH), lambda bi, i: (bi, 0, i, 0)),) * 3
    outs = pl.pallas_call(
        _mixer_in_kernel,
        name="mixer_in",
        grid=(b, s // tm),
        in_specs=[
            tok(D_MODEL), rot, rot,
            _const_spec((1, D_MODEL)),
            _const_spec(wa.shape), _const_spec(wd.shape), _const_spec(wg.shape),
            _const_spec((1, N_BRANCH * D_MODEL)),
            _const_spec((1, MLA_Q_RANK)), _const_spec((1, MLA_KV_RANK)),
            _const_spec(wuq.shape), _const_spec(wuk.shape), _const_spec(wuvt.shape),
        ],
        out_specs=out_specs,
        out_shape=out_shapes,
        scratch_shapes=[pltpu.VMEM((3 * DIL_WIDTH // LANES, tm, LANES), F32)],
        compiler_params=pltpu.CompilerParams(dimension_semantics=("arbitrary", "arbitrary"),
                                             vmem_limit_bytes=VMEM_LIMIT),
    )(h1, rot_cos, rot_sin, mix_pre_g.reshape(1, D_MODEL), wa, wd, wg, b_gate.reshape(1, -1),
      q_norm_g.reshape(1, -1), kv_norm_g.reshape(1, -1), wuq, wuk, wuvt)
    q, k, vt, gates = outs[:4]
    dil_qkv = [outs[4 + 3 * i:7 + 3 * i] for i in range(len(DIL_PATTERNS))]
    return q, k, vt, gates, dil_qkv


def _flash_kernel(qt_ref, k_ref, vt_ref, o_ref, s0_ref, s1_ref, s2_ref, s3_ref):
    nk = k_ref.shape[1] // FLASH_TK
    nq = qt_ref.shape[2]
    assert nk % 4 == 0 and nq % FLASH_QT == 0
    bufs = (s0_ref, s1_ref, s2_ref, s3_ref)

    chains = [(tt, hh) for tt in range(FLASH_QT) for hh in range(2)]

    def scores(t, c, ch, buf):
        tt, hh = chains[ch]
        cmax = None
        for part in range(FLASH_TK // FLASH_TS):
            k0 = c * FLASH_TK + part * FLASH_TS
            if not isinstance(k0, int):
                k0 = pl.multiple_of(k0, FLASH_TS)
            k = k_ref[0, pl.ds(k0, FLASH_TS), HEAD_PAD * hh:HEAD_PAD * (hh + 1)]
            st = jnp.dot(k, qt_ref[0, hh, t * FLASH_QT + tt], preferred_element_type=F32)
            buf[ch, FLASH_TS * part:FLASH_TS * (part + 1)] = st
            pm = jnp.max(st, axis=0, keepdims=True)
            cmax = pm if cmax is None else jnp.maximum(cmax, pm)
        return cmax

    def consume(c, ch, buf, m, cmax, acc):
        hh = chains[ch][1]
        m_new = jnp.maximum(m, cmax)
        alpha = jnp.exp2(m - m_new)
        p = jnp.exp2(buf[ch] - m_new).astype(BF16)
        vt = vt_ref[0, c, VT_ROWS * hh:VT_ROWS * (hh + 1), :]
        return m_new, alpha * acc + jnp.dot(vt, p, preferred_element_type=F32)

    def half(c, state, pair_in, pair_out, t_next, c_next):
        new = []
        for ch in range(len(chains)):
            m, cm0, cm1, acc = state[ch]
            nxt = [scores(t_next, c_next + i, ch, bufs[pair_out + i]) for i in range(2)]
            m, acc = consume(c, ch, bufs[pair_in], m, cm0, acc)
            m, acc = consume(c + 1, ch, bufs[pair_in + 1], m, cm1, acc)
            new.append((m, nxt[0], nxt[1], acc))
        return tuple(new)

    def tile(t, cmaxes):
        state = tuple((jnp.full((1, FLASH_TQ), -jnp.inf, F32), cmaxes[ch][0], cmaxes[ch][1],
                       jnp.zeros((VT_ROWS, FLASH_TQ), F32)) for ch in range(len(chains)))

        def group(j, st):
            c = 4 * j
            return half(c + 2, half(c, st, 0, 2, t, c + 2), 2, 0, t, c + 4)

        state = lax.fori_loop(0, nk // 4 - 1, group, state)
        t_next = jnp.minimum(t + 1, nq // FLASH_QT - 1)
        state = half(nk - 2, half(nk - 4, state, 0, 2, t, nk - 2), 2, 0, t_next, 0)
        for tt in range(FLASH_QT):
            outs = [acc[:MLA_V] / acc[MLA_V:MLA_V + 1] for (_, _, _, acc) in state[2 * tt:2 * tt + 2]]
            row0 = pl.multiple_of((t * FLASH_QT + tt) * FLASH_TQ, FLASH_TQ)
            o_ref[0, pl.ds(row0, FLASH_TQ), :] = jnp.concatenate(outs, axis=0).T.astype(BF16)
        return tuple((cm0, cm1) for (_, cm0, cm1, _) in state)

    first = tuple((scores(0, 0, ch, bufs[0]), scores(0, 1, ch, bufs[1])) for ch in range(len(chains)))
    lax.fori_loop(0, nq // FLASH_QT, tile, first)


def _mla_flash(qt, k, vt):
    b, s, _ = k.shape
    nk = s // FLASH_TK
    return pl.pallas_call(
        _flash_kernel,
        name="mla_flash",
        grid=(b, MLA_HEADS // 2),
        in_specs=[
            pl.BlockSpec((1, 2, s // FLASH_TQ, HEAD_PAD, FLASH_TQ), lambda bi, hp: (bi, hp, 0, 0, 0)),
            pl.BlockSpec((1, s, 2 * HEAD_PAD), lambda bi, hp: (bi, 0, hp)),
            pl.BlockSpec((1, nk, 2 * VT_ROWS, FLASH_TK), lambda bi, hp: (bi, 0, hp, 0)),
        ],
        out_specs=pl.BlockSpec((1, s, 2 * MLA_V), lambda bi, hp: (bi, 0, hp)),
        out_shape=jax.ShapeDtypeStruct((b, s, MLA_WIDTH), BF16),
        scratch_shapes=[pltpu.VMEM((2 * FLASH_QT, FLASH_TK, FLASH_TQ), F32)] * 4,
        compiler_params=pltpu.CompilerParams(dimension_semantics=("arbitrary", "arbitrary"),
                                             vmem_limit_bytes=VMEM_LIMIT),
    )(qt, k, vt)


def _dilated_kernel(q_ref, kc_ref, kp_ref, kn_ref, vc_ref, vp_ref, vn_ref, o_ref, lse_ref,
                    kw_ref, vt_ref, ot_ref, lt_ref, *sbufs, n_rows):
    lb = q_ref.shape[2]
    l0 = pl.program_id(2) * lb
    wrows = lb + 2 * DIL_HALF
    kw_ref[0:DIL_HALF] = kp_ref[0, 0]
    kw_ref[DIL_HALF:DIL_HALF + lb] = kc_ref[0, 0]
    kw_ref[DIL_HALF + lb:] = kn_ref[0, 0]
    vw = jnp.concatenate([vp_ref[0, 0], vc_ref[0, 0], vn_ref[0, 0]], axis=0).astype(F32)
    vt = vw.T.astype(BF16)
    ones = jnp.ones((VT_ROWS - DIL_HEAD_DIM, wrows), BF16)
    for h in range(DIL_HEADS):
        vt_ref[VT_ROWS * h:VT_ROWS * h + DIL_HEAD_DIM] = vt[DIL_HEAD_DIM * h:DIL_HEAD_DIM * (h + 1)]
        vt_ref[VT_ROWS * h + DIL_HEAD_DIM:VT_ROWS * (h + 1)] = ones

    win = DIL_SUB + 2 * DIL_HALF
    stripe_row = lax.broadcasted_iota(jnp.int32, (2 * LANES, LANES), 0)
    stripe_col = lax.broadcasted_iota(jnp.int32, (2 * LANES, LANES), 1)
    assert DIL_SUB == 2 * LANES and 2 * DIL_HALF == LANES
    live = ((slice(0, 2 * LANES), slice(0, LANES)), (slice(LANES, 3 * LANES), slice(LANES, 2 * LANES)))
    lane = lax.broadcasted_iota(jnp.int32, (DIL_SUB, LANES), 1)
    odd_lanes = ((lane >= HALF_B) & (lane < ROPE_DIM)) | (lane >= LANES // 2 + HALF_B)
    nbuf = len(sbufs)

    def produce(u):
        j, h = divmod(u, DIL_HEADS)
        ls = slice(LANES * (h // 2), LANES * (h // 2 + 1))
        kpair = kw_ref[j * DIL_SUB:j * DIL_SUB + win, ls]
        qpair = q_ref[0, 0, j * DIL_SUB:(j + 1) * DIL_SUB, ls]
        qh = jnp.where(odd_lanes if h % 2 else ~odd_lanes, qpair, jnp.zeros_like(qpair))
        st = lax.dot_general(kpair, qh, (((1,), (1,)), ((), ())), preferred_element_type=F32)
        cmax = []
        for rs, cs in live:
            srow = stripe_row + rs.start
            kidx = l0 - DIL_HALF + j * DIL_SUB + srow
            valid = ((srow - (stripe_col + cs.start)).astype(jnp.uint32) <= 2 * DIL_HALF) & (kidx.astype(jnp.uint32) < n_rows)
            sp = jnp.where(valid, st[rs, cs], NEG)
            sbufs[u % nbuf][rs, cs] = sp
            cmax.append(jnp.max(sp, axis=0, keepdims=True))
        return cmax

    def consume(u, m):
        j, h = divmod(u, DIL_HEADS)
        zero = jnp.zeros((LANES, LANES), BF16)
        cols = []
        for q, (rs, cs) in enumerate(live):
            pq = jnp.exp2(sbufs[u % nbuf][rs, cs] - m[q]).astype(BF16)
            cols.append(jnp.concatenate([pq, zero] if q == 0 else [zero, pq], axis=0))
        p = jnp.concatenate(cols, axis=1)
        r = jnp.dot(vt_ref[VT_ROWS * h:VT_ROWS * (h + 1), j * DIL_SUB:j * DIL_SUB + win], p,
                    preferred_element_type=F32)
        hs = slice(DIL_HEAD_DIM * h, DIL_HEAD_DIM * (h + 1))
        ot_ref[hs] = r[:DIL_HEAD_DIM] / r[DIL_HEAD_DIM:DIL_HEAD_DIM + 1]
        den = r[DIL_HEAD_DIM:DIL_HEAD_DIM + SUBLANES]
        for q, (_, cs) in enumerate(live):
            lt_ref[h:h + 1, cs] = m[q] + jnp.log2(den[:, cs][0:1])
        if h == DIL_HEADS - 1:
            o_ref[0, 0, j * DIL_SUB:(j + 1) * DIL_SUB, :] = ot_ref[...].T.astype(BF16)
            lse_ref[0, 0, j * DIL_SUB:(j + 1) * DIL_SUB, :] = lt_ref[...].T

    lt_ref[...] = jnp.zeros_like(lt_ref)
    n_units = (lb // DIL_SUB) * DIL_HEADS
    grp = nbuf // 2
    m_cur = [produce(i) for i in range(grp)]
    for u in range(0, n_units, grp):
        m_next = [produce(u + grp + i) for i in range(grp)] if u + grp < n_units else None
        for i in range(grp):
            consume(u + i, m_cur[i])
        m_cur = m_next


def _dilated(dq, dk, dv, dilation):
    b, _, n_rows, w = dq.shape
    lb = min(DIL_LB, n_rows)
    hb = lb // DIL_HALF
    last = n_rows // DIL_HALF - 1
    cur = pl.BlockSpec((1, 1, lb, w), lambda bi, r, l: (bi, r, l, 0))
    prev = pl.BlockSpec((1, 1, DIL_HALF, w), lambda bi, r, l: (bi, r, jnp.maximum(l * hb - 1, 0), 0))
    nxt = pl.BlockSpec((1, 1, DIL_HALF, w), lambda bi, r, l: (bi, r, jnp.minimum((l + 1) * hb, last), 0))
    return pl.pallas_call(
        functools.partial(_dilated_kernel, n_rows=n_rows),
        name=f"dilated_{dilation}",
        grid=(b, dilation, n_rows // lb),
        in_specs=[cur, cur, prev, nxt, cur, prev, nxt],
        out_specs=(cur, pl.BlockSpec((1, 1, lb, LANES), lambda bi, r, l: (bi, r, l, 0))),
        out_shape=(jax.ShapeDtypeStruct((b, dilation, n_rows, w), BF16),
                   jax.ShapeDtypeStruct((b, dilation, n_rows, LANES), F32)),
        scratch_shapes=[pltpu.VMEM((lb + 2 * DIL_HALF, w), BF16),
                        pltpu.VMEM((DIL_HEADS * VT_ROWS, lb + 2 * DIL_HALF), BF16),
                        pltpu.VMEM((w, DIL_SUB), F32),
                        pltpu.VMEM((LANES, DIL_SUB), F32)]
                       + [pltpu.VMEM((DIL_SUB + 2 * DIL_HALF, DIL_SUB), F32)] * (2 * DIL_GROUP),
        compiler_params=pltpu.CompilerParams(dimension_semantics=("arbitrary",) * 3, vmem_limit_bytes=VMEM_LIMIT),
    )(dq, dk, dk, dk, dv, dv, dv)


def _mixer_out_kernel(h_ref, oa_ref, o1_ref, l1_ref, o2_ref, l2_ref, o3_ref, l3_ref, gate_ref,
                      wba_ref, wbb_ref, wo_ref, post_ref, expand_ref, out_ref, *scr):
    tm = h_ref.shape[1]

    def token_major(ref, dil, buf):
        if dil == 1:
            return ref[0, 0].astype(F32)
        nc = ref.shape[-1] // LANES
        for r in range(dil):
            for c in range(nc):
                buf[c, pl.ds(r, tm // dil, stride=dil), :] = ref[0, r, :, LANES * c:LANES * (c + 1)].astype(F32)
        return jnp.concatenate([buf[c] for c in range(nc)], axis=1)

    bufs = iter(scr)
    os_, ls_ = [], []
    for (o_r, l_r), (_, dil) in zip(((o1_ref, l1_ref), (o2_ref, l2_ref), (o3_ref, l3_ref)), DIL_PATTERNS):
        os_.append(token_major(o_r, dil, None if dil == 1 else next(bufs)))
        ls_.append(token_major(l_r, dil, None if dil == 1 else next(bufs)))
    lm = jnp.maximum(jnp.maximum(ls_[0], ls_[1]), ls_[2])
    es = [jnp.exp2(l - lm) for l in ls_]
    inv = 1.0 / (es[0] + es[1] + es[2])
    ob = None
    for e, o in zip(es, os_):
        w = e * inv
        hi = w.astype(BF16)
        lo = (w - hi.astype(F32)).astype(BF16)
        wide = jnp.dot(jnp.concatenate([hi, lo], axis=1), expand_ref[...], preferred_element_type=F32)
        ob = wide * o if ob is None else ob + wide * o
    ya = jnp.dot(oa_ref[0], wba_ref[...], preferred_element_type=F32)
    yb = jnp.dot(ob.astype(BF16), wbb_ref[...], preferred_element_type=F32)
    gate = gate_ref[0].astype(F32)
    merged = gate[:, :D_MODEL] * ya + gate[:, D_MODEL:] * yb
    mix = jnp.dot(merged.astype(BF16), wo_ref[...], preferred_element_type=F32)
    out_ref[0] = h_ref[0] + _rms(mix, post_ref[...])


def _mixer_out(h1, oa, parts, gates, w_branch_a, w_branch_b, w_out, mix_post_g):
    b, s, _ = h1.shape
    tm = MIXOUT_TM
    tok = lambda c: pl.BlockSpec((1, tm, c), lambda bi, i: (bi, i, 0))
    flat, part_specs, scratch = [], [], []
    for (_, dil), pair in zip(DIL_PATTERNS, parts):
        flat += list(pair)
        for width in (DIL_WIDTH, LANES):
            part_specs.append(pl.BlockSpec((1, dil, tm // dil, width), lambda bi, i: (bi, 0, i, 0)))
            if dil > 1:
                scratch.append(pltpu.VMEM((width // LANES, tm, LANES), F32))
    head_of_lane = jnp.arange(DIL_WIDTH) // DIL_HEAD_DIM
    expand = (jnp.arange(2 * LANES)[:, None] % LANES == head_of_lane[None, :]).astype(BF16)
    return pl.pallas_call(
        _mixer_out_kernel,
        name="mixer_out",
        grid=(b, s // tm),
        scratch_shapes=scratch,
        in_specs=[tok(D_MODEL), tok(MLA_WIDTH)] + part_specs + [tok(N_BRANCH * D_MODEL),
                  _const_spec((MLA_WIDTH, D_MODEL)), _const_spec((DIL_WIDTH, D_MODEL)),
                  _const_spec((D_MODEL, D_MODEL)), _const_spec((1, D_MODEL)), _const_spec((2 * LANES, DIL_WIDTH))],
        out_specs=tok(D_MODEL),
        out_shape=jax.ShapeDtypeStruct((b, s, D_MODEL), F32),
        compiler_params=pltpu.CompilerParams(dimension_semantics=("arbitrary", "arbitrary"),
                                             vmem_limit_bytes=VMEM_LIMIT),
    )(h1, oa, *flat, gates, w_branch_a.astype(BF16), w_branch_b.astype(BF16), w_out.astype(BF16),
      mix_post_g.reshape(1, D_MODEL), expand)


def kernel(x, positions, ffn1_pre_g, ffn1_post_g, ffn1_w_gate, ffn1_w_up, ffn1_w_down, mix_pre_g, w_in, b_gate,
           q_norm_g, w_uq, kv_norm_g, w_uk, w_uv, w_branch_a, w_branch_b, w_out, mix_post_g,
           ffn2_pre_g, ffn2_post_g, ffn2_w_gate, ffn2_w_up, ffn2_w_down):
    b, s, d = x.shape
    depth = ffn1_pre_g.shape[0]
    assert all(win // (2 * dil) == DIL_HALF and s % (dil * DIL_SUB) == 0 for win, dil in DIL_PATTERNS)
    h = x
    for l in range(depth):
        h, rot_cos, rot_sin = _ffn(h.reshape(b * s, d), ffn1_pre_g[l], ffn1_post_g[l], ffn1_w_gate[l], ffn1_w_up[l],
                                   ffn1_w_down[l], positions=positions.astype(F32).reshape(1, b * s))
        h = h.reshape(b, s, d)
        q, k, vt, gates, dil_qkv = _mixer_in(h, rot_cos, rot_sin, mix_pre_g[l], w_in[l], b_gate[l], q_norm_g[l],
                                             w_uq[l], kv_norm_g[l], w_uk[l], w_uv[l])
        oa = _mla_flash(q, k, vt)
        parts = [_dilated(*qkv, dil) for qkv, (_, dil) in zip(dil_qkv, DIL_PATTERNS)]
        h = _mixer_out(h, oa, parts, gates, w_branch_a[l], w_branch_b[l], w_out[l], mix_post_g[l])
        h = _ffn(h.reshape(b * s, d), ffn2_pre_g[l], ffn2_post_g[l], ffn2_w_gate[l], ffn2_w_up[l],
                 ffn2_w_down[l]).reshape(b, s, d)
    return h
```
